```python
import functools
import jax
import jax.numpy as jnp
from jax import lax
import numpy as np

D_MODEL = 2048
BATCH = 4
SEQ = 2048
DEPTH = 4
DEC_BATCH = 8
DEC_SEQ = 4
PAST_LEN = 16384
PAGE_SIZE = 128

HEAD_DIM = 128
N_HEADS = D_MODEL // 256
N_KV_HEADS = N_HEADS // 2
GROUPS = N_HEADS // N_KV_HEADS
ATTN_WIDTH = N_HEADS * HEAD_DIM
KV_WIDTH = N_KV_HEADS * HEAD_DIM
ROT_DIM = HEAD_DIM // 4
ROPE_THETA = 500000.0
IDX_HEADS = D_MODEL // 128
IDX_DIM = 64
IDX_ROT = IDX_DIM // 4
IDX_TOPK = 256
MOBA_BLOCK = 256
MOBA_TOPK = 3
CHUNK = 128
C_GROUPS = D_MODEL // 256
C_GROUP_DIM = 128
C_WIDTH = C_GROUPS * C_GROUP_DIM
N_BRANCH = 3
D_FF = 4 * D_MODEL
EPS = 1e-6
QBLOCK_A = 64
QBLOCK_B = 16
COL_SIZES = (ATTN_WIDTH, KV_WIDTH, KV_WIDTH, IDX_HEADS * IDX_DIM, IDX_DIM, IDX_HEADS,
             ATTN_WIDTH, KV_WIDTH, KV_WIDTH, C_WIDTH, C_WIDTH, N_BRANCH * D_MODEL)
N_COLS = sum(COL_SIZES)

kernel_name = 'hybrid_dsa_moba_gmlp_decode_step'


def _rms_norm(x, g):
    xf = x.astype(jnp.float32)
    y = xf * lax.rsqrt(jnp.mean(xf * xf, axis=-1, keepdims=True) + EPS)
    return (y * g.astype(jnp.float32)).astype(x.dtype)


def _layer_norm(x, g, b):
    xf = x.astype(jnp.float32)
    mu = jnp.mean(xf, axis=-1, keepdims=True)
    xc = xf - mu
    y = xc * lax.rsqrt(jnp.mean(xc * xc, axis=-1, keepdims=True) + EPS)
    return (y * g.astype(jnp.float32) + b.astype(jnp.float32)).astype(x.dtype)


def _rope(x, pos, rot_dim):
    half = rot_dim // 2
    inv = jnp.power(ROPE_THETA, -jnp.arange(half, dtype=jnp.float32) / half)
    ang = pos.astype(jnp.float32)[:, None] * inv[None, :]
    cos = jnp.cos(ang)[None, :, None, :].astype(x.dtype)
    sin = jnp.sin(ang)[None, :, None, :].astype(x.dtype)
    x1, x2, xp = x[..., :half], x[..., half:rot_dim], x[..., rot_dim:]
    return jnp.concatenate([x1 * cos - x2 * sin, x2 * cos + x1 * sin, xp], axis=-1)


def _bidx(pos):
    return jnp.arange(pos.shape[0]).reshape((-1,) + (1,) * (pos.ndim - 1))


def _dense_rows(arr, pos, h=None):
    b = _bidx(pos)
    p = jnp.clip(pos, 0, arr.shape[1] - 1)
    if h is None:
        return arr[b, p]
    return arr[b, p, h]


def _paged_rows(pool, new, page_table, pos, h=None):
    b = _bidx(pos)
    pp = jnp.clip(pos, 0, PAST_LEN - 1)
    phys = page_table[b, pp // PAGE_SIZE] * PAGE_SIZE + pp % PAGE_SIZE
    npos = jnp.clip(pos - PAST_LEN, 0, new.shape[1] - 1)
    if h is None:
        old, nw = pool[phys], new[b, npos]
    else:
        old, nw = pool[phys, h], new[b, npos, h]
    keep = (pos < PAST_LEN).reshape(pos.shape + (1,) * (old.ndim - pos.ndim))
    return jnp.where(keep, old, nw)


def _sweep(fn, block, pos, *arrays):
    T = pos.shape[0]
    blk = min(block, T)
    nb = T // blk

    def split(a):
        return jnp.moveaxis(a.reshape((a.shape[0], nb, blk) + a.shape[2:]), 1, 0)

    out = lax.map(lambda args: fn(*args), (pos.reshape(nb, blk),) + tuple(split(a) for a in arrays))
    out = jnp.moveaxis(out, 0, 1)
    return out.reshape((out.shape[0], T) + out.shape[3:])


def _project(x, pos, w_in_l, g1_l):
    B, T = x.shape[:2]
    xn = _rms_norm(x, g1_l)
    h = jnp.einsum('btd,dc->btc', xn, w_in_l)
    cuts = list(np.cumsum(COL_SIZES)[:-1])
    qa, ka, va, qi, ki, wi, qb, kb, vb, cu, cv, gl = jnp.split(h, cuts, axis=-1)
    qa = _rope(qa.reshape(B, T, N_HEADS, HEAD_DIM), pos, ROT_DIM)
    ka = _rope(ka.reshape(B, T, N_KV_HEADS, HEAD_DIM), pos, ROT_DIM)
    va = va.reshape(B, T, N_KV_HEADS, HEAD_DIM)
    qi = _rope(qi.reshape(B, T, IDX_HEADS, IDX_DIM), pos, IDX_ROT)
    ki = _rope(ki[:, :, None, :], pos, IDX_ROT)[:, :, 0]
    qb = _rope(qb.reshape(B, T, N_HEADS, HEAD_DIM), pos, ROT_DIM)
    kb = _rope(kb.reshape(B, T, N_KV_HEADS, HEAD_DIM), pos, ROT_DIM)
    vb = vb.reshape(B, T, N_KV_HEADS, HEAD_DIM)
    return qa, ka, va, qi, ki, wi, qb, kb, vb, cu, cv, gl


def _dsa_block(q_pos, q, qi, wi, k_idx, fetch_k, fetch_v, n_sel):
    B, Q = q.shape[:2]
    L = k_idx.shape[1]
    dots = jnp.einsum('bqhd,bsd->bqhs', qi, k_idx).astype(jnp.float32) * (IDX_DIM ** -0.5)
    score = jnp.einsum('bqhs,bqh->bqs', jax.nn.relu(dots), wi.astype(jnp.float32)) * (IDX_HEADS ** -0.5)
    causal = jnp.arange(L)[None, :] <= q_pos[:, None]
    score = jnp.where(causal[None], score, -jnp.inf)
    _, sel = lax.top_k(score, n_sel)
    k_sel, v_sel = fetch_k(sel), fetch_v(sel)
    qg = q.reshape(B, Q, N_KV_HEADS, GROUPS, HEAD_DIM)
    logits = jnp.einsum('bqkgd,bqskd->bqkgs', qg, k_sel).astype(jnp.float32) * (HEAD_DIM ** -0.5)
    ok = sel <= q_pos[None, :, None]
    logits = jnp.where(ok[:, :, None, None, :], logits, -jnp.inf)
    p = jax.nn.softmax(logits, axis=-1).astype(v_sel.dtype)
    o = jnp.einsum('bqkgs,bqskd->bqkgd', p, v_sel)
    return o.reshape(B, Q, ATTN_WIDTH)


def _block_means(k):
    B, L = k.shape[:2]
    nb = L // MOBA_BLOCK
    kb = k[:, :nb * MOBA_BLOCK].reshape(B, nb, MOBA_BLOCK, N_KV_HEADS, HEAD_DIM)
    return jnp.mean(kb.astype(jnp.float32), axis=2).astype(k.dtype)


def _moba_block(q_pos, q, k_means, fetch_k, fetch_v):
    B, Q = q.shape[:2]
    scale = HEAD_DIM ** -0.5
    qg = q.reshape(B, Q, N_KV_HEADS, GROUPS, HEAD_DIM)
    own = q_pos // MOBA_BLOCK
    own_pos = own[:, None] * MOBA_BLOCK + jnp.arange(MOBA_BLOCK)[None, :]
    own_ok = own_pos <= q_pos[:, None]
    own_pos_b = jnp.broadcast_to(own_pos[None], (B, Q, MOBA_BLOCK))
    k_own, v_own = fetch_k(own_pos_b), fetch_v(own_pos_b)
    l_own = jnp.einsum('bqkgd,bqnkd->bqkgn', qg, k_own).astype(jnp.float32) * scale
    l_own = jnp.where(own_ok[None, :, None, None, :], l_own, -jnp.inf)
    nb = k_means.shape[1]
    n_sel = min(MOBA_TOPK, nb)
    if n_sel == 0:
        p = jax.nn.softmax(l_own, axis=-1).astype(v_own.dtype)
        o = jnp.einsum('bqkgn,bqnkd->bqkgd', p, v_own)
        return o.reshape(B, Q, ATTN_WIDTH)
    gate = jnp.einsum('bqkgd,bnkd->bqkgn', qg, k_means).astype(jnp.float32)
    past = jnp.arange(nb)[None, :] < own[:, None]
    gate = jnp.where(past[None, :, None, None, :], gate, -jnp.inf)
    _, sel = lax.top_k(gate, n_sel)
    sel_ok = sel < own[None, :, None, None, None]
    sel_pos = sel[..., None] * MOBA_BLOCK + jnp.arange(MOBA_BLOCK)
    hk = jnp.arange(N_KV_HEADS).reshape(1, 1, -1, 1, 1, 1)
    k_sel, v_sel = fetch_k(sel_pos, hk), fetch_v(sel_pos, hk)
    l_sel = jnp.einsum('bqkgd,bqkgsnd->bqkgsn', qg, k_sel).astype(jnp.float32) * scale
    l_sel = jnp.where(sel_ok[..., None], l_sel, -jnp.inf).reshape(B, Q, N_KV_HEADS, GROUPS, n_sel * MOBA_BLOCK)
    p = jax.nn.softmax(jnp.concatenate([l_sel, l_own], axis=-1), axis=-1).astype(v_own.dtype)
    p_sel = p[..., :n_sel * MOBA_BLOCK].reshape(B, Q, N_KV_HEADS, GROUPS, n_sel, MOBA_BLOCK)
    p_own = p[..., n_sel * MOBA_BLOCK:]
    o = jnp.einsum('bqkgsn,bqkgsnd->bqkgd', p_sel, v_sel) + jnp.einsum('bqkgn,bqnkd->bqkgd', p_own, v_own)
    return o.reshape(B, Q, ATTN_WIDTH)


def _chunk_mlp(u, v, ln_g, ln_b, w_s, b_s):
    B, T, _ = v.shape
    vn = _layer_norm(v, ln_g, ln_b)
    c = min(CHUNK, T)
    vb = vn.reshape(B, T // c, c, C_GROUPS, C_GROUP_DIM)
    mask = jnp.tril(jnp.ones((c, c), dtype=bool))
    ws = jnp.where(mask[None], w_s[:, :c, :c], 0).astype(v.dtype)
    z = jnp.einsum('gij,bnjgd->bnigd', ws, vb) + b_s[:, :c].T[None, None, :, :, None]
    return u * z.reshape(B, T, C_WIDTH), vn


def _merge_ffn(x, o_a, o_b, o_c, gl, w_branch_l, w_out_l, g2_l, w1_l, w2_l):
    B, T = x.shape[:2]
    gates = jax.nn.sigmoid(gl.astype(jnp.float32)).astype(x.dtype).reshape(B, T, N_BRANCH, D_MODEL)
    br = jnp.einsum('btnw,nwd->btnd', jnp.stack([o_a, o_b, o_c], axis=2), w_branch_l)
    merged = jnp.sum(gates * br, axis=2)
    x = x + jnp.einsum('btd,de->bte', merged, w_out_l)
    hn = _rms_norm(x, g2_l)
    h = jnp.square(jax.nn.relu(jnp.einsum('btd,df->btf', hn, w1_l)))
    return x + jnp.einsum('btf,fd->btd', h, w2_l)


def setup_inputs(seed: int = 0) -> dict:
    key = jax.random.key(seed)
    ks = jax.random.split(key, 24)
    f32 = jnp.float32
    n_pages = PAST_LEN // PAGE_SIZE
    used = DEC_BATCH * n_pages
    n_phys = used + max(1, used // 4)

    def nrm(k, shape, scale=1.0):
        return jax.random.normal(k, shape, f32) * scale

    page_table = jax.random.permutation(ks[7], n_phys)[:used].reshape(DEC_BATCH, n_pages).astype(jnp.int32)
    return {
        'x_prompt': nrm(ks[0], (BATCH, SEQ, D_MODEL)),
        'x_sample': nrm(ks[1], (DEC_BATCH, DEC_SEQ, D_MODEL)),
        'cache_a_k': nrm(ks[2], (DEPTH, n_phys, PAGE_SIZE, N_KV_HEADS, HEAD_DIM)),
        'cache_a_v': nrm(ks[3], (DEPTH, n_phys, PAGE_SIZE, N_KV_HEADS, HEAD_DIM)),
        'cache_a_idx': nrm(ks[4], (DEPTH, n_phys, PAGE_SIZE, IDX_DIM)),
        'cache_b_k': nrm(ks[5], (DEPTH, n_phys, PAGE_SIZE, N_KV_HEADS, HEAD_DIM)),
        'cache_b_v': nrm(ks[6], (DEPTH, n_phys, PAGE_SIZE, N_KV_HEADS, HEAD_DIM)),
        'page_table': page_table,
        'w_in': nrm(ks[8], (DEPTH, D_MODEL, N_COLS), D_MODEL ** -0.5),
        'norm1_g': 1.0 + nrm(ks[9], (DEPTH, D_MODEL), 0.02),
        'c_ln_g': 1.0 + nrm(ks[10], (DEPTH, C_WIDTH), 0.02),
        'c_ln_b': nrm(ks[11], (DEPTH, C_WIDTH), 0.02),
        'w_spatial': nrm(ks[12], (DEPTH, C_GROUPS, CHUNK, CHUNK), CHUNK ** -0.5),
        'b_spatial': 1.0 + nrm(ks[13], (DEPTH, C_GROUPS, CHUNK), 0.02),
        'w_branch': nrm(ks[14], (DEPTH, N_BRANCH, ATTN_WIDTH, D_MODEL), ATTN_WIDTH ** -0.5),
        'w_out': nrm(ks[15], (DEPTH, D_MODEL, D_MODEL), D_MODEL ** -0.5),
        'norm2_g': 1.0 + nrm(ks[16], (DEPTH, D_MODEL), 0.02),
        'w_ff1': nrm(ks[17], (DEPTH, D_MODEL, D_FF), D_MODEL ** -0.5),
        'w_ff2': nrm(ks[18], (DEPTH, D_FF, D_MODEL), D_FF ** -0.5),
        'norm_f_g': 1.0 + nrm(ks[19], (D_MODEL,), 0.02),
    }


def reference(x_prompt, x_sample, cache_a_k, cache_a_v, cache_a_idx, cache_b_k, cache_b_v, page_table,
              w_in, norm1_g, c_ln_g, c_ln_b, w_spatial, b_spatial, w_branch, w_out, norm2_g,
              w_ff1, w_ff2, norm_f_g):
    pos_p = jnp.arange(SEQ, dtype=jnp.int32)
    pos_s = PAST_LEN + jnp.arange(DEC_SEQ, dtype=jnp.int32)
    sel_p = min(IDX_TOPK, SEQ // 4)
    sel_s = min(IDX_TOPK, (PAST_LEN + DEC_SEQ) // 4)
    n_db = page_table.shape[0]
    xp, xs = x_prompt, x_sample
    pa_k, pa_v, pa_i, pb_k, pb_v = [], [], [], [], []
    sa_k, sa_v, sa_i, sb_k, sb_v, sc_v = [], [], [], [], [], []
    for l in range(DEPTH):
        qa, ka, va, qi, ki, wi, qb, kb, vb, cu, cv, gl = _project(xp, pos_p, w_in[l], norm1_g[l])
        dsa_p = functools.partial(_dsa_block, k_idx=ki, fetch_k=functools.partial(_dense_rows, ka),
                                  fetch_v=functools.partial(_dense_rows, va), n_sel=sel_p)
        o_a = _sweep(dsa_p, QBLOCK_A, pos_p, qa, qi, wi)
        moba_p = functools.partial(_moba_block, k_means=_block_means(kb), fetch_k=functools.partial(_dense_rows, kb),
                                   fetch_v=functools.partial(_dense_rows, vb))
        o_b = _sweep(moba_p, QBLOCK_B, pos_p, qb)
        o_c, _ = _chunk_mlp(cu, cv, c_ln_g[l], c_ln_b[l], w_spatial[l], b_spatial[l])
        xp = _merge_ffn(xp, o_a, o_b, o_c, gl, w_branch[l], w_out[l], norm2_g[l], w_ff1[l], w_ff2[l])
        pa_k.append(ka)
        pa_v.append(va)
        pa_i.append(ki)
        pb_k.append(kb)
        pb_v.append(vb)

        qa, ka, va, qi, ki, wi, qb, kb, vb, cu, cv, gl = _project(xs, pos_s, w_in[l], norm1_g[l])
        pool_ak = cache_a_k[l].reshape((-1,) + cache_a_k.shape[3:])
        pool_av = cache_a_v[l].reshape((-1,) + cache_a_v.shape[3:])
        pool_bk = cache_b_k[l].reshape((-1,) + cache_b_k.shape[3:])
        pool_bv = cache_b_v[l].reshape((-1,) + cache_b_v.shape[3:])
        ki_full = jnp.concatenate([cache_a_idx[l][page_table].reshape(n_db, -1, IDX_DIM), ki], axis=1)
        o_a = _dsa_block(pos_s, qa, qi, wi, ki_full,
                         functools.partial(_paged_rows, pool_ak, ka, page_table),
                         functools.partial(_paged_rows, pool_av, va, page_table), sel_s)
        kb_full = jnp.concatenate(
            [cache_b_k[l][page_table].reshape(n_db, -1, N_KV_HEADS, HEAD_DIM), kb], axis=1)
        o_b = _moba_block(pos_s, qb, _block_means(kb_full),
                          functools.partial(_paged_rows, pool_bk, kb, page_table),
                          functools.partial(_paged_rows, pool_bv, vb, page_table))
        o_c, vn = _chunk_mlp(cu, cv, c_ln_g[l], c_ln_b[l], w_spatial[l], b_spatial[l])
        xs = _merge_ffn(xs, o_a, o_b, o_c, gl, w_branch[l], w_out[l], norm2_g[l], w_ff1[l], w_ff2[l])
        sa_k.append(ka)
        sa_v.append(va)
        sa_i.append(ki)
        sb_k.append(kb)
        sb_v.append(vb)
        sc_v.append(vn)

    y_prompt = _rms_norm(xp, norm_f_g)
    y_sample = _rms_norm(xs, norm_f_g)
    return (y_prompt, y_sample,
            jnp.stack(pa_k), jnp.stack(pa_v), jnp.stack(pa_i), jnp.stack(pb_k), jnp.stack(pb_v),
            jnp.stack(sa_k), jnp.stack(sa_v), jnp.stack(sa_i), jnp.stack(sb_k), jnp.stack(sb_v),
            jnp.stack(sc_v))
```

```python
import functools

import jax
import jax.numpy as jnp
from jax import lax
from jax.experimental import pallas as pl
from jax.experimental.pallas import tpu as pltpu

HEAD_DIM = 128
ROPE_THETA = 500000.0
IDX_DIM = 64
IDX_TOPK = 256
MOBA_BLOCK = 256
MOBA_TOPK = 3
CHUNK = 128
C_GROUP_DIM = 128
N_BRANCH = 3
EPS = 1e-6
LANES = 128
INT32_MIN = -(2 ** 31)

F32 = jnp.float32
BF16 = jnp.bfloat16
NEG_INF = float("-inf")
VMEM_LIMIT = 56 * 1024 * 1024

_NT = (((1,), (1,)), ((), ()))


def _params(sem):
    return pltpu.CompilerParams(dimension_semantics=sem, vmem_limit_bytes=VMEM_LIMIT)


def _rms_kernel(x_ref, g_ref, o_ref):
    x = x_ref[...]
    y = x * lax.rsqrt(jnp.mean(x * x, axis=-1, keepdims=True) + EPS)
    o_ref[...] = (y * g_ref[...]).astype(o_ref.dtype)


def _rms_norm(x, g, out_dtype):
    m, d = x.shape
    tm = min(256, m)
    return pl.pallas_call(
        _rms_kernel,
        out_shape=jax.ShapeDtypeStruct((m, d), out_dtype),
        grid=(m // tm,),
        in_specs=[pl.BlockSpec((tm, d), lambda i: (i, 0)),
                  pl.BlockSpec((1, d), lambda i: (0, 0))],
        out_specs=pl.BlockSpec((tm, d), lambda i: (i, 0)),
        compiler_params=_params(("parallel",)),
        name="rms_norm",
    )(x, g.reshape(1, d))


def _rope_epilogue(r, c, sa, sb, half):
    outs = []
    for s in range(r.shape[1] // LANES):
        a = r[:, s * LANES:(s + 1) * LANES]
        outs.append(a * c + pltpu.roll(a, LANES - half, 1) * sa + pltpu.roll(a, half, 1) * sb)
    return outs[0] if len(outs) == 1 else jnp.concatenate(outs, axis=1)


def _mm_kernel(*refs, epi, nk, n_out, half):
    a_ref, w_ref = refs[0], refs[1]
    n_extra = {"none": 0, "relu2": 0, "rope": 3, "residual": 1}[epi]
    extra = refs[2:2 + n_extra]
    outs = refs[2 + n_extra:2 + n_extra + n_out]
    acc_ref = refs[2 + n_extra + n_out] if nk > 1 else None

    def finish(r):
        if epi == "rope":
            r = _rope_epilogue(r, extra[0][...], extra[1][...], extra[2][...], half)
        elif epi == "relu2":
            r = jnp.square(jnp.maximum(r, 0.0))
        elif epi == "residual":
            r = extra[0][...] + r
        for o_ref in outs:
            o_ref[...] = r.astype(o_ref.dtype)

    part = jnp.dot(a_ref[...], w_ref[...], preferred_element_type=F32)
    if nk == 1:
        finish(part)
        return
    k = pl.program_id(2)

    @pl.when(k == 0)
    def _():
        acc_ref[...] = part

    @pl.when(k > 0)
    def _():
        acc_ref[...] += part

    @pl.when(k == nk - 1)
    def _():
        finish(acc_ref[...])


def _matmul(a, w, out_dtypes, epi="none", extra=(), half=0, tm=1024, tn=512, tk=2048):
    m, kdim = a.shape
    n = w.shape[1]
    tm, tn, tk = min(tm, m), min(tn, n), min(tk, kdim)
    if epi == "rope":
        tm = min(tm, extra[0].shape[0])
    assert m % tm == 0 and n % tn == 0 and kdim % tk == 0, (a.shape, w.shape, tm, tn, tk)
    nk = kdim // tk
    in_specs = [pl.BlockSpec((tm, tk), lambda i, j, k: (i, k)),
                pl.BlockSpec((tk, tn), lambda i, j, k: (k, j))]
    if epi == "rope":
        nt = extra[0].shape[0] // tm
        in_specs += [pl.BlockSpec((tm, LANES), lambda i, j, k: (i % nt, 0))] * 3
    elif epi == "residual":
        in_specs += [pl.BlockSpec((tm, tn), lambda i, j, k: (i, j))]
    outs = pl.pallas_call(
        functools.partial(_mm_kernel, epi=epi, nk=nk, n_out=len(out_dtypes), half=half),
        out_shape=[jax.ShapeDtypeStruct((m, n), dt) for dt in out_dtypes],
        grid=(m // tm, n // tn, nk),
        in_specs=in_specs,
        out_specs=[pl.BlockSpec((tm, tn), lambda i, j, k: (i, j)) for _ in out_dtypes],
        scratch_shapes=[pltpu.VMEM((tm, tn), F32)] if nk > 1 else [],
        compiler_params=_params(("parallel", "parallel", "arbitrary")),
        name="matmul_" + epi,
    )(a, w, *extra)
    return tuple(outs)


def _ordered_key(score):
    bits = pltpu.bitcast(score, jnp.int32)
    return jnp.where(bits < 0, bits ^ jnp.int32(0x7FFFFFFF), bits)


def _kth_largest_key(key_ref, n_sel):
    rows = key_ref.shape[0]

    def body(b, t):
        cand = t + lax.shift_left(jnp.int32(1), jnp.int32(31) - b)
        cnt = jnp.sum((key_ref[...] >= cand).astype(F32), axis=1, keepdims=True)
        return jnp.where(cnt >= float(n_sel), cand, t)

    return lax.fori_loop(0, 32, body, jnp.full((rows, 1), INT32_MIN, jnp.int32))


def _topk_lanes(g, valid, n_sel):
    lane = lax.broadcasted_iota(jnp.int32, g.shape, 1).astype(F32)
    gm = jnp.where(valid, g, NEG_INF)
    sel = jnp.zeros(g.shape, jnp.bool_)
    for _ in range(n_sel):
        mx = jnp.max(gm, axis=1, keepdims=True)
        first = jnp.min(jnp.where(gm == mx, lane, float(LANES)), axis=1, keepdims=True)
        pick = lane == first
        sel = sel | pick
        gm = jnp.where(pick, NEG_INF, gm)
    return sel & valid


def _dsa_prompt_kernel(qa_ref, ka_ref, va_ref, qi_ref, ki_ref, wi_ref, o_ref, ki_s, key_s,
                       *, tq, seq, n_sel, n_heads, n_idx_heads, groups):
    i = pl.program_id(1)

    @pl.when(i == 0)
    def _():
        ki_s[...] = ki_ref[:, :IDX_DIM].astype(BF16)

    q_pos = i * tq + lax.broadcasted_iota(jnp.int32, (tq, 1), 0)
    k_pos = lax.broadcasted_iota(jnp.int32, (1, seq), 1)
    causal = k_pos <= q_pos
    ki = ki_s[...]
    score = jnp.zeros((tq, seq), F32)
    for h in range(n_idx_heads):
        d = lax.dot_general(qi_ref[:, h * IDX_DIM:(h + 1) * IDX_DIM], ki, _NT,
                            preferred_element_type=F32) * (IDX_DIM ** -0.5)
        score = score + jnp.maximum(d, 0.0) * wi_ref[:, IDX_DIM + h:IDX_DIM + h + 1]
    score = jnp.where(causal, score * (n_idx_heads ** -0.5), NEG_INF)
    key_s[...] = _ordered_key(score)
    thr = _kth_largest_key(key_s, n_sel)
    mask = (key_s[...] >= thr) & causal
    for h in range(n_heads):
        kv = h // groups
        q = qa_ref[:, h * HEAD_DIM:(h + 1) * HEAD_DIM]
        logits = lax.dot_general(q, ka_ref[:, kv * HEAD_DIM:(kv + 1) * HEAD_DIM], _NT,
                                 preferred_element_type=F32) * (HEAD_DIM ** -0.5)
        logits = jnp.where(mask, logits, NEG_INF)
        p = jnp.exp(logits - jnp.max(logits, axis=1, keepdims=True))
        l = jnp.sum(p, axis=1, keepdims=True)
        o = jnp.dot(p.astype(BF16), va_ref[:, kv * HEAD_DIM:(kv + 1) * HEAD_DIM],
                    preferred_element_type=F32)
        o_ref[:, h * HEAD_DIM:(h + 1) * HEAD_DIM] = (o / l).astype(o_ref.dtype)


def _dsa_prompt(hq, hk_bf, hv_bf, hqi, hkiwi, *, batch, seq, n_heads, n_idx_heads, groups):
    tq = min(128, seq)
    nq = seq // tq
    aw = n_heads * HEAD_DIM
    kvw = aw // groups
    n_sel = min(IDX_TOPK, seq // 4)
    kern = functools.partial(_dsa_prompt_kernel, tq=tq, seq=seq, n_sel=n_sel, n_heads=n_heads,
                             n_idx_heads=n_idx_heads, groups=groups)
    return pl.pallas_call(
        kern,
        out_shape=jax.ShapeDtypeStruct((batch * seq, aw), BF16),
        grid=(batch, nq),
        in_specs=[pl.BlockSpec((tq, aw), lambda b, i: (b * nq + i, 0)),
                  pl.BlockSpec((seq, kvw), lambda b, i: (b, 0)),
                  pl.BlockSpec((seq, kvw), lambda b, i: (b, 0)),
                  pl.BlockSpec((tq, n_idx_heads * IDX_DIM), lambda b, i: (b * nq + i, 0)),
                  pl.BlockSpec((seq, LANES), lambda b, i: (b, 0)),
                  pl.BlockSpec((tq, LANES), lambda b, i: (b * nq + i, 0))],
        out_specs=pl.BlockSpec((tq, aw), lambda b, i: (b * nq + i, 0)),
        scratch_shapes=[pltpu.VMEM((seq, IDX_DIM), BF16), pltpu.VMEM((tq, seq), jnp.int32)],
        compiler_params=_params(("parallel", "arbitrary")),
        name="dsa_prompt",
    )(hq, hk_bf, hv_bf, hqi, hkiwi, hkiwi)


def _moba_prompt_kernel(qb_ref, kb_ref, kbf_ref, vb_ref, o_ref, km_s, m_s, l_s, acc_s,
                        *, nb, n_sel, n_heads, groups):
    i = pl.program_id(1)
    blk = MOBA_BLOCK
    kvw = kbf_ref.shape[1]

    @pl.when(i == 0)
    def _():
        km_s[...] = jnp.zeros_like(km_s)
        for j in range(nb):
            km_s[j:j + 1, :] = jnp.mean(kbf_ref[j * blk:(j + 1) * blk, :], axis=0, keepdims=True)

    lane = lax.broadcasted_iota(jnp.int32, (blk, LANES), 1)
    past = lane < i
    tril = (lax.broadcasted_iota(jnp.int32, (blk, blk), 1)
            <= lax.broadcasted_iota(jnp.int32, (blk, blk), 0))
    scale = HEAD_DIM ** -0.5
    row0 = pl.multiple_of(i * blk, blk)
    for h in range(n_heads):
        kv = h // groups
        cols = slice(kv * HEAD_DIM, (kv + 1) * HEAD_DIM)
        q = qb_ref[:, h * HEAD_DIM:(h + 1) * HEAD_DIM]
        gate = lax.dot_general(q, km_s[:, cols].astype(BF16), _NT, preferred_element_type=F32)
        selw = _topk_lanes(gate, past, n_sel).astype(F32)
        s = lax.dot_general(q, kb_ref[pl.ds(row0, blk), cols], _NT, preferred_element_type=F32) * scale
        s = jnp.where(tril, s, NEG_INF)
        m = jnp.max(s, axis=1, keepdims=True)
        p = jnp.exp(s - m)
        m_s[...] = m
        l_s[...] = jnp.sum(p, axis=1, keepdims=True)
        acc_s[...] = jnp.dot(p.astype(BF16), vb_ref[pl.ds(row0, blk), cols], preferred_element_type=F32)
        for j in range(nb - 1):
            @pl.when(j < i)
            def _(j=j, selw=selw, q=q, cols=cols):
                sj = lax.dot_general(q, kb_ref[j * blk:(j + 1) * blk, cols], _NT,
                                     preferred_element_type=F32) * scale
                sj = jnp.where(selw[:, j:j + 1] > 0.0, sj, NEG_INF)
                m_old = m_s[...]
                m_new = jnp.maximum(m_old, jnp.max(sj, axis=1, keepdims=True))
                alpha = jnp.exp(m_old - m_new)
                pj = jnp.exp(sj - m_new)
                m_s[...] = m_new
                l_s[...] = alpha * l_s[...] + jnp.sum(pj, axis=1, keepdims=True)
                acc_s[...] = alpha * acc_s[...] + jnp.dot(
                    pj.astype(BF16), vb_ref[j * blk:(j + 1) * blk, cols], preferred_element_type=F32)
        o_ref[:, h * HEAD_DIM:(h + 1) * HEAD_DIM] = (acc_s[...] / l_s[...]).astype(o_ref.dtype)


def _moba_prompt(hq, hk_bf, hk_f32, hv_bf, *, batch, seq, n_heads, groups):
    blk = MOBA_BLOCK
    assert seq % blk == 0
    nb = seq // blk
    assert 1 <= nb <= LANES
    aw = n_heads * HEAD_DIM
    kvw = aw // groups
    kern = functools.partial(_moba_prompt_kernel, nb=nb, n_sel=min(MOBA_TOPK, nb), n_heads=n_heads,
                             groups=groups)
    return pl.pallas_call(
        kern,
        out_shape=jax.ShapeDtypeStruct((batch * seq, aw), BF16),
        grid=(batch, nb),
        in_specs=[pl.BlockSpec((blk, aw), lambda b, i: (b * nb + i, 1)),
                  pl.BlockSpec((seq, kvw), lambda b, i: (b, 1)),
                  pl.BlockSpec((seq, kvw), lambda b, i: (b, 1)),
                  pl.BlockSpec((seq, kvw), lambda b, i: (b, 1))],
        out_specs=pl.BlockSpec((blk, aw), lambda b, i: (b * nb + i, 0)),
        scratch_shapes=[pltpu.VMEM((LANES, kvw), F32), pltpu.VMEM((blk, 1), F32),
                        pltpu.VMEM((blk, 1), F32), pltpu.VMEM((blk, HEAD_DIM), F32)],
        compiler_params=_params(("parallel", "arbitrary")),
        name="moba_prompt",
    )(hq, hk_bf, hk_f32, hv_bf)


def _gmlp_kernel(cu_ref, cv_ref, g_ref, b_ref, ws_ref, bs_ref, o_ref, *vn_out, n_groups):
    v = cv_ref[...]
    mu = jnp.mean(v, axis=-1, keepdims=True)
    vc = v - mu
    vn = vc * lax.rsqrt(jnp.mean(vc * vc, axis=-1, keepdims=True) + EPS) * g_ref[...] + b_ref[...]
    if vn_out:
        vn_out[0][...] = vn
    for g in range(n_groups):
        cols = slice(g * C_GROUP_DIM, (g + 1) * C_GROUP_DIM)
        z = jnp.dot(ws_ref[g], vn[:, cols].astype(BF16), preferred_element_type=F32) + bs_ref[:, g:g + 1]
        o_ref[:, cols] = (cu_ref[:, cols] * z).astype(o_ref.dtype)


def _gmlp(hc, ln_g, ln_b, ws, bs, *, want_vn):
    m = hc.shape[0]
    n_groups, r = ws.shape[0], ws.shape[1]
    cw = n_groups * C_GROUP_DIM
    out_shape = [jax.ShapeDtypeStruct((m, cw), BF16)]
    out_specs = [pl.BlockSpec((r, cw), lambda i: (i, 0))]
    if want_vn:
        out_shape.append(jax.ShapeDtypeStruct((m, cw), F32))
        out_specs.append(pl.BlockSpec((r, cw), lambda i: (i, 0)))
    return pl.pallas_call(
        functools.partial(_gmlp_kernel, n_groups=n_groups),
        out_shape=out_shape,
        grid=(m // r,),
        in_specs=[pl.BlockSpec((r, cw), lambda i: (i, 0)),
                  pl.BlockSpec((r, cw), lambda i: (i, 1)),
                  pl.BlockSpec((1, cw), lambda i: (0, 0)),
                  pl.BlockSpec((1, cw), lambda i: (0, 0)),
                  pl.BlockSpec((n_groups, r, r), lambda i: (0, 0, 0)),
                  pl.BlockSpec((r, n_groups), lambda i: (0, 0))],
        out_specs=out_specs,
        compiler_params=_params(("parallel",)),
        name="gmlp",
    )(hc, hc, ln_g.reshape(1, cw), ln_b.reshape(1, cw), ws, bs)


def _merge_kernel(oa_ref, ob_ref, oc_ref, wb_ref, g0_ref, g1_ref, g2_ref, o_ref):
    acc = None
    for o_n, g_n, n in ((oa_ref, g0_ref, 0), (ob_ref, g1_ref, 1), (oc_ref, g2_ref, 2)):
        br = jnp.dot(o_n[...], wb_ref[n], preferred_element_type=F32)
        term = (1.0 / (1.0 + jnp.exp(-g_n[...]))) * br
        acc = term if acc is None else acc + term
    o_ref[...] = acc.astype(o_ref.dtype)


def _merge(o_a, o_b, o_c, wb, hc, *, d_model, gate_col0):
    m, aw = o_a.shape
    tm, tn = min(512, m), 512
    nj = d_model // tn
    g0 = gate_col0 // tn
    gate_specs = [pl.BlockSpec((tm, tn), functools.partial(lambda i, j, n: (i, g0 + n * nj + j), n=n))
                  for n in range(N_BRANCH)]
    return pl.pallas_call(
        _merge_kernel,
        out_shape=jax.ShapeDtypeStruct((m, d_model), BF16),
        grid=(m // tm, nj),
        in_specs=[pl.BlockSpec((tm, aw), lambda i, j: (i, 0))] * 3
        + [pl.BlockSpec((N_BRANCH, aw, tn), lambda i, j: (0, 0, j))] + gate_specs,
        out_specs=pl.BlockSpec((tm, tn), lambda i, j: (i, j)),
        compiler_params=_params(("parallel", "parallel")),
        name="merge",
    )(o_a, o_b, o_c, wb, hc, hc, hc)


def _dsa_sample_score_kernel(pt_ref, cache_ref, kn_ref, qi_ref, wi_ref, o_ref, *, n_pages, n_new, n_idx_heads):
    p = pl.program_id(1)
    rows = n_idx_heads * n_new

    def scores(k, valid):
        d = lax.dot_general(qi_ref[...], k, _NT, preferred_element_type=F32) * (IDX_DIM ** -0.5)
        dw = jnp.maximum(d, 0.0) * wi_ref[...]
        acc = dw[0:8]
        for r in range(8, rows, 8):
            acc = acc + dw[r:r + 8]
        s = acc[0:n_new]
        for r in range(n_new, 8, n_new):
            s = s + acc[r:r + n_new]
        s = s * (n_idx_heads ** -0.5)
        return s if valid is None else jnp.where(valid, s, NEG_INF)

    @pl.when(p < n_pages)
    def _():
        o_ref[...] = scores(cache_ref[...], None)

    @pl.when(p == n_pages)
    def _():
        t = lax.broadcasted_iota(jnp.int32, (n_new, LANES), 0)
        c = lax.broadcasted_iota(jnp.int32, (n_new, LANES), 1)
        o_ref[...] = scores(kn_ref[...], c <= t)


def _dsa_sample_scores(pt_flat, cache_idx, layer, ki_new_pad, qi_rows, wi_rows, *, dec_batch, n_pages, n_new,
                       n_idx_heads):
    page = cache_idx.shape[2]
    assert page == LANES and 8 % n_new == 0 and (n_idx_heads * n_new) % 8 == 0
    rows = n_idx_heads * n_new
    kern = functools.partial(_dsa_sample_score_kernel, n_pages=n_pages, n_new=n_new, n_idx_heads=n_idx_heads)
    grid_spec = pltpu.PrefetchScalarGridSpec(
        num_scalar_prefetch=1,
        grid=(dec_batch, n_pages + 1),
        in_specs=[pl.BlockSpec((None, None, page, IDX_DIM),
                               lambda b, p, pt: (layer, pt[b * n_pages + jnp.minimum(p, n_pages - 1)], 0, 0)),
                  pl.BlockSpec((None, page, IDX_DIM), lambda b, p, pt: (b, 0, 0)),
                  pl.BlockSpec((None, rows, IDX_DIM), lambda b, p, pt: (b, 0, 0)),
                  pl.BlockSpec((None, rows, 1), lambda b, p, pt: (b, 0, 0))],
        out_specs=pl.BlockSpec((None, n_new, LANES), lambda b, p, pt: (b, 0, p)),
    )
    return pl.pallas_call(
        kern,
        out_shape=jax.ShapeDtypeStruct((dec_batch, n_new, (n_pages + 1) * LANES), F32),
        grid_spec=grid_spec,
        compiler_params=_params(("parallel", "arbitrary")),
        name="dsa_sample_scores",
    )(pt_flat, cache_idx, ki_new_pad, qi_rows, wi_rows)


def _select_kernel(s_ref, o_ref, key_s, *, n_sel):
    key_s[...] = _ordered_key(s_ref[...])
    thr = _kth_largest_key(key_s, n_sel)
    o_ref[...] = ((key_s[...] >= thr) & (s_ref[...] > NEG_INF)).astype(F32)


def _dsa_sample_select(scores, n_sel):
    db, n_new, length = scores.shape
    return pl.pallas_call(
        functools.partial(_select_kernel, n_sel=n_sel),
        out_shape=jax.ShapeDtypeStruct(scores.shape, F32),
        grid=(db,),
        in_specs=[pl.BlockSpec((None, n_new, length), lambda b: (b, 0, 0))],
        out_specs=pl.BlockSpec((None, n_new, length), lambda b: (b, 0, 0)),
        scratch_shapes=[pltpu.VMEM((n_new, length), jnp.int32)],
        compiler_params=_params(("parallel",)),
        name="dsa_sample_select",
    )(scores)


def _dsa_sample_attn_kernel(pt_ref, kc_ref, vc_ref, kn_ref, vn_ref, q_ref, sel_ref, o_ref, m_s, l_s, acc_s,
                            *, n_pages, n_new, n_kv, groups):
    p = pl.program_id(1)
    page = LANES
    rpk = groups * n_new
    scale = HEAD_DIM ** -0.5

    @pl.when(p == 0)
    def _():
        m_s[...] = jnp.full_like(m_s, NEG_INF)
        l_s[...] = jnp.zeros_like(l_s)
        acc_s[...] = jnp.zeros_like(acc_s)

    mask = jnp.concatenate([sel_ref[...]] * groups, axis=0) > 0.0

    def step(get_k, get_v):
        for kv in range(n_kv):
            rows = slice(kv * rpk, (kv + 1) * rpk)
            s = lax.dot_general(q_ref[rows, :], get_k(kv), _NT, preferred_element_type=F32) * scale
            s = jnp.where(mask, s, NEG_INF)
            m_old = m_s[rows, :]
            m_new = jnp.maximum(m_old, jnp.max(s, axis=1, keepdims=True))
            m_safe = jnp.where(m_new == NEG_INF, 0.0, m_new)
            alpha = jnp.exp(m_old - m_safe)
            pr = jnp.exp(s - m_safe)
            m_s[rows, :] = m_new
            l_s[rows, :] = alpha * l_s[rows, :] + jnp.sum(pr, axis=1, keepdims=True)
            acc_s[rows, :] = alpha * acc_s[rows, :] + jnp.dot(pr, get_v(kv), preferred_element_type=F32)

    @pl.when(p < n_pages)
    def _():
        step(lambda kv: kc_ref[pl.ds(kv, page, stride=n_kv), :],
             lambda kv: vc_ref[pl.ds(kv, page, stride=n_kv), :])

    @pl.when(p == n_pages)
    def _():
        step(lambda kv: kn_ref[kv], lambda kv: vn_ref[kv])
        o_ref[...] = acc_s[...] / l_s[...]


def _dsa_sample_attn(pt_flat, cache_k, cache_v, layer, k_new_pad, v_new_pad, q_rows, sel, *, dec_batch, n_pages,
                     n_new, n_kv, groups):
    rows = n_kv * groups * n_new
    prow = cache_k.shape[2]
    kern = functools.partial(_dsa_sample_attn_kernel, n_pages=n_pages, n_new=n_new, n_kv=n_kv, groups=groups)

    def page_map(b, p, pt):
        return (layer, pt[b * n_pages + jnp.minimum(p, n_pages - 1)], 0, 0)

    grid_spec = pltpu.PrefetchScalarGridSpec(
        num_scalar_prefetch=1,
        grid=(dec_batch, n_pages + 1),
        in_specs=[pl.BlockSpec((None, None, prow, HEAD_DIM), page_map),
                  pl.BlockSpec((None, None, prow, HEAD_DIM), page_map),
                  pl.BlockSpec((None, n_kv, LANES, HEAD_DIM), lambda b, p, pt: (b, 0, 0, 0)),
                  pl.BlockSpec((None, n_kv, LANES, HEAD_DIM), lambda b, p, pt: (b, 0, 0, 0)),
                  pl.BlockSpec((None, rows, HEAD_DIM), lambda b, p, pt: (b, 0, 0)),
                  pl.BlockSpec((None, n_new, LANES), lambda b, p, pt: (b, 0, p))],
        out_specs=pl.BlockSpec((None, rows, HEAD_DIM), lambda b, p, pt: (b, 0, 0)),
        scratch_shapes=[pltpu.VMEM((rows, 1), F32), pltpu.VMEM((rows, 1), F32),
                        pltpu.VMEM((rows, HEAD_DIM), F32)],
    )
    return pl.pallas_call(
        kern,
        out_shape=jax.ShapeDtypeStruct((dec_batch, rows, HEAD_DIM), F32),
        grid_spec=grid_spec,
        compiler_params=_params(("parallel", "arbitrary")),
        name="dsa_sample_attn",
    )(pt_flat, cache_k, cache_v, k_new_pad, v_new_pad, q_rows, sel)


def _moba_sample_kernel(pt_ref, k0_ref, k1_ref, v0_ref, v1_ref, kn_ref, vn_ref, q_ref, o_ref,
                        gate_s, m_s, l_s, acc_s, *, nb, n_sel, n_new, n_kv, groups, past_len):
    j = pl.program_id(1)
    page = LANES
    rpk = groups * n_new
    rows_all = n_kv * rpk
    scale = HEAD_DIM ** -0.5
    lane = lax.broadcasted_iota(jnp.int32, (rows_all, LANES), 1)

    @pl.when(j == 0)
    def _():
        gate_s[...] = jnp.zeros_like(gate_s)
        m_s[...] = jnp.zeros_like(m_s)
        l_s[...] = jnp.zeros_like(l_s)

    g_cols, m_cols, l_cols = [], [], []
    for kv in range(n_kv):
        rows = slice(kv * rpk, (kv + 1) * rpk)
        q = q_ref[rows, :]
        k = jnp.concatenate([k0_ref[pl.ds(kv, page, stride=n_kv), :],
                             k1_ref[pl.ds(kv, page, stride=n_kv), :]], axis=0)
        v = jnp.concatenate([v0_ref[pl.ds(kv, page, stride=n_kv), :],
                             v1_ref[pl.ds(kv, page, stride=n_kv), :]], axis=0)
        kmean = jnp.mean(k, axis=0, keepdims=True)
        g_cols.append(jnp.sum(q * kmean, axis=1, keepdims=True))
        s = lax.dot_general(q, k, _NT, preferred_element_type=F32) * scale
        m = jnp.max(s, axis=1, keepdims=True)
        pr = jnp.exp(s - m)
        m_cols.append(m)
        l_cols.append(jnp.sum(pr, axis=1, keepdims=True))
        acc_s[j, rows, :] = jnp.dot(pr, v, preferred_element_type=F32)
    here = lane == j
    gate_s[...] = jnp.where(here, jnp.concatenate(g_cols, axis=0), gate_s[...])
    m_s[...] = jnp.where(here, jnp.concatenate(m_cols, axis=0), m_s[...])
    l_s[...] = jnp.where(here, jnp.concatenate(l_cols, axis=0), l_s[...])

    @pl.when(j == nb - 1)
    def _():
        t_row = lax.broadcasted_iota(jnp.int32, (rpk, LANES), 0) & (n_new - 1)
        c_col = lax.broadcasted_iota(jnp.int32, (rpk, LANES), 1)
        own_ok = c_col <= t_row
        t_all = lax.broadcasted_iota(jnp.int32, (rows_all, LANES), 0) & (n_new - 1)
        own_blk = jnp.right_shift(past_len + t_all, MOBA_BLOCK.bit_length() - 1)
        valid = (lane < own_blk) & (lane < nb)
        sel = _topk_lanes(gate_s[...], valid, n_sel)
        for kv in range(n_kv):
            rows = slice(kv * rpk, (kv + 1) * rpk)
            q = q_ref[rows, :]
            s = lax.dot_general(q, kn_ref[kv], _NT, preferred_element_type=F32) * scale
            s = jnp.where(own_ok, s, NEG_INF)
            m_o = jnp.max(s, axis=1, keepdims=True)
            m_blk = jnp.where(sel[rows, :], m_s[rows, :], NEG_INF)
            m_all = jnp.maximum(m_o, jnp.max(m_blk, axis=1, keepdims=True))
            pr = jnp.exp(s - m_all)
            w = jnp.where(sel[rows, :], jnp.exp(m_blk - m_all), 0.0)
            den = jnp.sum(pr, axis=1, keepdims=True) + jnp.sum(w * l_s[rows, :], axis=1, keepdims=True)
            num = jnp.dot(pr, vn_ref[kv], preferred_element_type=F32)
            for jb in range(nb):
                num = num + w[:, jb:jb + 1] * acc_s[jb, rows, :]
            o_ref[rows, :] = num / den


def _moba_sample(pt_flat, cache_k, cache_v, layer, k_new_pad, v_new_pad, q_rows, *, dec_batch, n_pages, n_new,
                 n_kv, groups, past_len):
    assert past_len % MOBA_BLOCK == 0 and n_new < MOBA_BLOCK and MOBA_BLOCK == 2 * LANES
    assert n_new & (n_new - 1) == 0
    nb = (past_len + n_new) // MOBA_BLOCK
    assert 1 <= nb <= LANES and n_pages == 2 * nb
    rows = n_kv * groups * n_new
    prow = cache_k.shape[2]
    kern = functools.partial(_moba_sample_kernel, nb=nb, n_sel=min(MOBA_TOPK, nb), n_new=n_new, n_kv=n_kv,
                             groups=groups, past_len=past_len)

    def page_map(half):
        return lambda b, j, pt: (layer, pt[b * n_pages + 2 * j + half], 0, 0)

    grid_spec = pltpu.PrefetchScalarGridSpec(
        num_scalar_prefetch=1,
        grid=(dec_batch, nb),
        in_specs=[pl.BlockSpec((None, None, prow, HEAD_DIM), page_map(0)),
                  pl.BlockSpec((None, None, prow, HEAD_DIM), page_map(1)),
                  pl.BlockSpec((None, None, prow, HEAD_DIM), page_map(0)),
                  pl.BlockSpec((None, None, prow, HEAD_DIM), page_map(1)),
                  pl.BlockSpec((None, n_kv, LANES, HEAD_DIM), lambda b, j, pt: (b, 0, 0, 0)),
                  pl.BlockSpec((None, n_kv, LANES, HEAD_DIM), lambda b, j, pt: (b, 0, 0, 0)),
                  pl.BlockSpec((None, rows, HEAD_DIM), lambda b, j, pt: (b, 0, 0))],
        out_specs=pl.BlockSpec((None, rows, HEAD_DIM), lambda b, j, pt: (b, 0, 0)),
        scratch_shapes=[pltpu.VMEM((rows, LANES), F32), pltpu.VMEM((rows, LANES), F32),
                        pltpu.VMEM((rows, LANES), F32), pltpu.VMEM((nb, rows, HEAD_DIM), F32)],
    )
    return pl.pallas_call(
        kern,
        out_shape=jax.ShapeDtypeStruct((dec_batch, rows, HEAD_DIM), F32),
        grid_spec=grid_spec,
        compiler_params=_params(("parallel", "arbitrary")),
        name="moba_sample",
    )(pt_flat, cache_k, cache_k, cache_v, cache_v, k_new_pad, v_new_pad, q_rows)


def _rope_tables(pos, rot, width, n_rope_lanes=LANES):
    half = rot // 2
    inv = jnp.power(ROPE_THETA, -jnp.arange(half, dtype=F32) / half)
    ang = pos.astype(F32)[:, None] * inv[None, :]
    cos, sin = jnp.cos(ang), jnp.sin(ang)
    t = pos.shape[0]
    zeros = jnp.zeros((t, half), F32)
    rest0 = jnp.zeros((t, width - rot), F32)
    c = jnp.concatenate([cos, cos, jnp.ones((t, width - rot), F32)], axis=1)
    sa = jnp.concatenate([-sin, zeros, rest0], axis=1)
    sb = jnp.concatenate([zeros, sin, rest0], axis=1)
    reps = n_rope_lanes // width
    pad = LANES - n_rope_lanes
    c = jnp.concatenate([jnp.tile(c, (1, reps)), jnp.ones((t, pad), F32)], axis=1)
    sa = jnp.concatenate([jnp.tile(sa, (1, reps)), jnp.zeros((t, pad), F32)], axis=1)
    sb = jnp.concatenate([jnp.tile(sb, (1, reps)), jnp.zeros((t, pad), F32)], axis=1)
    return c, sa, sb


def _kv_major_rows(a, db, n_new, n_kv, groups):
    a = a.reshape(db, n_new, n_kv, groups, HEAD_DIM)
    return jnp.transpose(a, (0, 2, 3, 1, 4)).reshape(db, n_kv * groups * n_new, HEAD_DIM)


def _from_kv_major_rows(o, db, n_new, n_kv, groups):
    o = o.reshape(db, n_kv, groups, n_new, HEAD_DIM)
    return jnp.transpose(o, (0, 3, 1, 2, 4)).reshape(db * n_new, n_kv * groups * HEAD_DIM)


def _new_kv_pad(k, db, n_new, n_kv):
    k = jnp.transpose(k.reshape(db, n_new, n_kv, HEAD_DIM), (0, 2, 1, 3))
    return jnp.pad(k, ((0, 0), (0, 0), (0, LANES - n_new), (0, 0)))


def kernel(x_prompt, x_sample, cache_a_k, cache_a_v, cache_a_idx, cache_b_k, cache_b_v, page_table, w_in, norm1_g,
           c_ln_g, c_ln_b, w_spatial, b_spatial, w_branch, w_out, norm2_g, w_ff1, w_ff2, norm_f_g):
    batch, seq, d_model = x_prompt.shape
    db, n_new, _ = x_sample.shape
    depth, n_phys, page, n_kv, _ = cache_a_k.shape
    n_pages = page_table.shape[1]
    past_len = n_pages * page
    n_heads = d_model // 256
    groups = n_heads // n_kv
    aw = n_heads * HEAD_DIM
    kvw = n_kv * HEAD_DIM
    n_idx_heads = d_model // 128
    iw = n_idx_heads * IDX_DIM
    n_cg = w_spatial.shape[1]
    cw = n_cg * C_GROUP_DIM
    rot, idx_rot = HEAD_DIM // 4, IDX_DIM // 4
    assert page == LANES and aw == 2 * kvw and cw % 512 == 0 and seq % CHUNK == 0 and n_new <= CHUNK

    sizes = (aw, kvw, kvw, iw, IDX_DIM, n_idx_heads, aw, kvw, kvw, cw, cw, N_BRANCH * d_model)
    offs = [0]
    for s in sizes:
        offs.append(offs[-1] + s)
    (o_qa, o_ka, o_va, o_qi, o_ki, o_wi, o_qb, o_kb, o_vb, o_cu, o_cv, o_gl, _) = offs

    pos_p = jnp.arange(seq, dtype=jnp.int32)
    pos_s = jnp.tile(past_len + jnp.arange(n_new, dtype=jnp.int32), db)
    tabs = {
        "p": (_rope_tables(pos_p, rot, HEAD_DIM), _rope_tables(pos_p, idx_rot, IDX_DIM),
              _rope_tables(pos_p, idx_rot, IDX_DIM, IDX_DIM)),
        "s": (_rope_tables(pos_s, rot, HEAD_DIM), _rope_tables(pos_s, idx_rot, IDX_DIM),
              _rope_tables(pos_s, idx_rot, IDX_DIM, IDX_DIM)),
    }

    cache_ak = cache_a_k.reshape(depth, n_phys, page * n_kv, HEAD_DIM)
    cache_av = cache_a_v.reshape(depth, n_phys, page * n_kv, HEAD_DIM)
    cache_bk = cache_b_k.reshape(depth, n_phys, page * n_kv, HEAD_DIM)
    cache_bv = cache_b_v.reshape(depth, n_phys, page * n_kv, HEAD_DIM)
    pt_flat = page_table.reshape(-1).astype(jnp.int32)

    xp = x_prompt.reshape(batch * seq, d_model)
    xs = x_sample.reshape(db * n_new, d_model)
    rs = db * n_new
    tri = jnp.tril(jnp.ones((CHUNK, CHUNK), dtype=bool))
    sel_s = min(IDX_TOPK, (past_len + n_new) // 4)

    outs = {k: [] for k in ("pa_k", "pa_v", "pa_i", "pb_k", "pb_v", "sa_k", "sa_v", "sa_i", "sb_k", "sb_v", "sc_v")}

    def project(x, l, which):
        t128, t64, tkiwi = tabs[which]
        w = w_in[l]
        xn = _rms_norm(x, norm1_g[l], BF16)
        w_q = jnp.concatenate([w[:, o_qa:o_ka], w[:, o_qb:o_kb]], axis=1).astype(BF16)
        w_k = jnp.concatenate([w[:, o_ka:o_va], w[:, o_kb:o_vb]], axis=1).astype(BF16)
        w_v = jnp.concatenate([w[:, o_va:o_qi], w[:, o_vb:o_cu]], axis=1).astype(BF16)
        w_qi = w[:, o_qi:o_ki].astype(BF16)
        w_kiwi = jnp.pad(w[:, o_ki:o_qb], ((0, 0), (0, LANES - IDX_DIM - n_idx_heads))).astype(BF16)
        w_c = w[:, o_cu:].astype(BF16)
        (hq,) = _matmul(xn, w_q, (BF16,), "rope", t128, rot // 2)
        hk, hk_bf = _matmul(xn, w_k, (F32, BF16), "rope", t128, rot // 2)
        hv, hv_bf = _matmul(xn, w_v, (F32, BF16))
        (hqi,) = _matmul(xn, w_qi, (BF16,), "rope", t64, idx_rot // 2)
        (hkiwi,) = _matmul(xn, w_kiwi, (F32,), "rope", tkiwi, idx_rot // 2)
        (hc,) = _matmul(xn, w_c, (F32,))
        return hq, hk, hk_bf, hv, hv_bf, hqi, hkiwi, hc

    def merge_ffn(x, o_a, o_b, o_c, hc, l):
        merged = _merge(o_a, o_b, o_c, w_branch[l].astype(BF16), hc, d_model=d_model, gate_col0=2 * cw)
        (x1,) = _matmul(merged, w_out[l].astype(BF16), (F32,), "residual", (x,))
        hn = _rms_norm(x1, norm2_g[l], BF16)
        (h1,) = _matmul(hn, w_ff1[l].astype(BF16), (BF16,), "relu2")
        (x2,) = _matmul(h1, w_ff2[l].astype(BF16), (F32,), "residual", (x1,))
        return x2

    for l in range(depth):
        hq, hk, hk_bf, hv, hv_bf, hqi, hkiwi, hc = project(xp, l, "p")
        o_a = _dsa_prompt(hq, hk_bf, hv_bf, hqi, hkiwi, batch=batch, seq=seq, n_heads=n_heads,
                          n_idx_heads=n_idx_heads, groups=groups)
        o_b = _moba_prompt(hq, hk_bf, hk, hv_bf, batch=batch, seq=seq, n_heads=n_heads, groups=groups)
        ws_p = jnp.where(tri[None], w_spatial[l], 0).astype(BF16)
        bs_p = b_spatial[l].T
        (o_c,) = _gmlp(hc, c_ln_g[l], c_ln_b[l], ws_p, bs_p, want_vn=False)
        xp = merge_ffn(xp, o_a, o_b, o_c, hc, l)
        outs["pa_k"].append(hk[:, :kvw].reshape(batch, seq, n_kv, HEAD_DIM))
        outs["pa_v"].append(hv[:, :kvw].reshape(batch, seq, n_kv, HEAD_DIM))
        outs["pa_i"].append(hkiwi[:, :IDX_DIM].reshape(batch, seq, IDX_DIM))
        outs["pb_k"].append(hk[:, kvw:].reshape(batch, seq, n_kv, HEAD_DIM))
        outs["pb_v"].append(hv[:, kvw:].reshape(batch, seq, n_kv, HEAD_DIM))

        hq, hk, hk_bf, hv, hv_bf, hqi, hkiwi, hc = project(xs, l, "s")
        hq_f = hq.astype(F32)
        qi_rows = jnp.transpose(hqi.astype(F32).reshape(db, n_new, n_idx_heads, IDX_DIM), (0, 2, 1, 3)).reshape(
            db, n_idx_heads * n_new, IDX_DIM)
        wi_rows = jnp.transpose(hkiwi[:, IDX_DIM:IDX_DIM + n_idx_heads].reshape(db, n_new, n_idx_heads),
                                (0, 2, 1)).reshape(db, n_idx_heads * n_new, 1)
        ki_new = jnp.pad(hkiwi[:, :IDX_DIM].reshape(db, n_new, IDX_DIM), ((0, 0), (0, LANES - n_new), (0, 0)))
        scores = _dsa_sample_scores(pt_flat, cache_a_idx, l, ki_new, qi_rows, wi_rows, dec_batch=db,
                                    n_pages=n_pages, n_new=n_new, n_idx_heads=n_idx_heads)
        sel = _dsa_sample_select(scores, sel_s)
        qa_rows = _kv_major_rows(hq_f[:, :aw], db, n_new, n_kv, groups)
        o_a = _dsa_sample_attn(pt_flat, cache_ak, cache_av, l, _new_kv_pad(hk[:, :kvw], db, n_new, n_kv),
                               _new_kv_pad(hv[:, :kvw], db, n_new, n_kv), qa_rows, sel, dec_batch=db,
                               n_pages=n_pages, n_new=n_new, n_kv=n_kv, groups=groups)
        o_a = _from_kv_major_rows(o_a, db, n_new, n_kv, groups).astype(BF16)
        qb_rows = _kv_major_rows(hq_f[:, aw:], db, n_new, n_kv, groups)
        o_b = _moba_sample(pt_flat, cache_bk, cache_bv, l, _new_kv_pad(hk[:, kvw:], db, n_new, n_kv),
                           _new_kv_pad(hv[:, kvw:], db, n_new, n_kv), qb_rows, dec_batch=db, n_pages=n_pages,
                           n_new=n_new, n_kv=n_kv, groups=groups, past_len=past_len)
        o_b = _from_kv_major_rows(o_b, db, n_new, n_kv, groups).astype(BF16)
        ws_c = jnp.where(tri[None, :n_new, :n_new], w_spatial[l][:, :n_new, :n_new], 0)
        ws_s = jnp.einsum("ab,gij->gaibj", jnp.eye(db, dtype=F32), ws_c).reshape(n_cg, rs, rs).astype(BF16)
        bs_s = jnp.tile(b_spatial[l][:, :n_new].T, (db, 1))
        o_c, vn = _gmlp(hc, c_ln_g[l], c_ln_b[l], ws_s, bs_s, want_vn=True)
        xs = merge_ffn(xs, o_a, o_b, o_c, hc, l)
        outs["sa_k"].append(hk[:, :kvw].reshape(db, n_new, n_kv, HEAD_DIM))
        outs["sa_v"].append(hv[:, :kvw].reshape(db, n_new, n_kv, HEAD_DIM))
        outs["sa_i"].append(hkiwi[:, :IDX_DIM].reshape(db, n_new, IDX_DIM))
        outs["sb_k"].append(hk[:, kvw:].reshape(db, n_new, n_kv, HEAD_DIM))
        outs["sb_v"].append(hv[:, kvw:].reshape(db, n_new, n_kv, HEAD_DIM))
        outs["sc_v"].append(vn.reshape(db, n_new, cw))

    y_prompt = _rms_norm(xp, norm_f_g, F32).reshape(batch, seq, d_model)
    y_sample = _rms_norm(xs, norm_f_g, F32).reshape(db, n_new, d_model)
    st = {k: jnp.stack(v) for k, v in outs.items()}
    return (y_prompt, y_sample, st["pa_k"], st["pa_v"], st["pa_i"], st["pb_k"], st["pb_v"],
            st["sa_k"], st["sa_v"], st["sa_i"], st["sb_k"], st["sb_v"], st["sc_v"])
```

```python
import functools

import jax
import jax.numpy as jnp
from jax import lax
from jax.experimental import pallas as pl
from jax.experimental.pallas import tpu as pltpu

HEAD_DIM = 128
ROPE_THETA = 500000.0
IDX_DIM = 64
IDX_TOPK = 256
MOBA_BLOCK = 256
MOBA_TOPK = 3
CHUNK = 128
C_GROUP_DIM = 128
N_BRANCH = 3
EPS = 1e-6
LANES = 128
INT32_MIN = -(2 ** 31)
PAGES_PER_STEP = 16

F32 = jnp.float32
BF16 = jnp.bfloat16
NEG_INF = float("-inf")
VMEM_LIMIT = 56 * 1024 * 1024

_NT = (((1,), (1,)), ((), ()))


def _params(sem):
    return pltpu.CompilerParams(dimension_semantics=sem, vmem_limit_bytes=VMEM_LIMIT)


def _rms_kernel(x_ref, g_ref, o_ref):
    x = x_ref[...]
    y = x * lax.rsqrt(jnp.mean(x * x, axis=-1, keepdims=True) + EPS)
    o_ref[...] = (y * g_ref[...]).astype(o_ref.dtype)


def _rms_norm(x, g, out_dtype):
    m, d = x.shape
    tm = min(256, m)
    return pl.pallas_call(
        _rms_kernel,
        out_shape=jax.ShapeDtypeStruct((m, d), out_dtype),
        grid=(m // tm,),
        in_specs=[pl.BlockSpec((tm, d), lambda i: (i, 0)),
                  pl.BlockSpec((1, d), lambda i: (0, 0))],
        out_specs=pl.BlockSpec((tm, d), lambda i: (i, 0)),
        compiler_params=_params(("parallel",)),
        name="rms_norm",
    )(x, g.reshape(1, d))


def _rope_epilogue(r, c, sa, sb, half):
    outs = []
    for s in range(r.shape[1] // LANES):
        a = r[:, s * LANES:(s + 1) * LANES]
        outs.append(a * c + pltpu.roll(a, LANES - half, 1) * sa + pltpu.roll(a, half, 1) * sb)
    return outs[0] if len(outs) == 1 else jnp.concatenate(outs, axis=1)


def _mm_kernel(*refs, epi, nk, n_out, half):
    a_ref, w_ref = refs[0], refs[1]
    n_extra = {"none": 0, "relu2": 0, "rope": 3, "residual": 1}[epi]
    extra = refs[2:2 + n_extra]
    outs = refs[2 + n_extra:2 + n_extra + n_out]
    acc_ref = refs[2 + n_extra + n_out] if nk > 1 else None

    def finish(r):
        if epi == "rope":
            r = _rope_epilogue(r, extra[0][...], extra[1][...], extra[2][...], half)
        elif epi == "relu2":
            r = jnp.square(jnp.maximum(r, 0.0))
        elif epi == "residual":
            r = extra[0][...] + r
        for o_ref in outs:
            o_ref[...] = r.astype(o_ref.dtype)

    part = jnp.dot(a_ref[...], w_ref[...], preferred_element_type=F32)
    if nk == 1:
        finish(part)
        return
    k = pl.program_id(2)

    @pl.when(k == 0)
    def _():
        acc_ref[...] = part

    @pl.when(k > 0)
    def _():
        acc_ref[...] += part

    @pl.when(k == nk - 1)
    def _():
        finish(acc_ref[...])


def _matmul(a, w, out_dtypes, epi="none", extra=(), half=0, tm=1024, tn=512, tk=2048):
    m, kdim = a.shape
    n = w.shape[1]
    tm, tn, tk = min(tm, m), min(tn, n), min(tk, kdim)
    if epi == "rope":
        tm = min(tm, extra[0].shape[0])
    assert m % tm == 0 and n % tn == 0 and kdim % tk == 0, (a.shape, w.shape, tm, tn, tk)
    nk = kdim // tk
    in_specs = [pl.BlockSpec((tm, tk), lambda i, j, k: (i, k)),
                pl.BlockSpec((tk, tn), lambda i, j, k: (k, j))]
    if epi == "rope":
        nt = extra[0].shape[0] // tm
        in_specs += [pl.BlockSpec((tm, LANES), lambda i, j, k: (i % nt, 0))] * 3
    elif epi == "residual":
        in_specs += [pl.BlockSpec((tm, tn), lambda i, j, k: (i, j))]
    outs = pl.pallas_call(
        functools.partial(_mm_kernel, epi=epi, nk=nk, n_out=len(out_dtypes), half=half),
        out_shape=[jax.ShapeDtypeStruct((m, n), dt) for dt in out_dtypes],
        grid=(m // tm, n // tn, nk),
        in_specs=in_specs,
        out_specs=[pl.BlockSpec((tm, tn), lambda i, j, k: (i, j)) for _ in out_dtypes],
        scratch_shapes=[pltpu.VMEM((tm, tn), F32)] if nk > 1 else [],
        compiler_params=_params(("parallel", "parallel", "arbitrary")),
        name="matmul_" + epi,
    )(a, w, *extra)
    return tuple(outs)


def _ordered_key(score):
    bits = pltpu.bitcast(score, jnp.int32)
    return jnp.where(bits < 0, bits ^ jnp.int32(0x7FFFFFFF), bits)


def _kth_largest_key(count_ge, rows, n_sel):
    def body(b, t):
        cand = t + lax.shift_left(jnp.int32(1), jnp.int32(31) - b)
        return jnp.where(count_ge(cand) >= float(n_sel), cand, t)

    return lax.fori_loop(0, 32, body, jnp.full((rows, 1), INT32_MIN, jnp.int32))


def _topk_lanes(g, valid, n_sel):
    lane = lax.broadcasted_iota(jnp.int32, g.shape, 1).astype(F32)
    gm = jnp.where(valid, g, NEG_INF)
    sel = jnp.zeros(g.shape, jnp.bool_)
    for _ in range(n_sel):
        mx = jnp.max(gm, axis=1, keepdims=True)
        first = jnp.min(jnp.where(gm == mx, lane, float(LANES)), axis=1, keepdims=True)
        pick = lane == first
        sel = sel | pick
        gm = jnp.where(pick, NEG_INF, gm)
    return sel & valid


def _softmax_pv(logits, v):
    p = jnp.exp(logits - jnp.max(logits, axis=1, keepdims=True))
    l = jnp.sum(p, axis=1, keepdims=True)
    return jnp.dot(p.astype(BF16), v, preferred_element_type=F32) / l


def _dsa_prompt_kernel(qa_ref, ka_ref, va_ref, qi_ref, ki_ref, wi_ref, o_ref, ki_s, key_s, bias_s,
                       *, tq, seq, n_sel, n_heads, n_idx_heads, groups):
    i = pl.program_id(1)
    n_chunks = seq // tq
    n_valid = i + 1

    @pl.when(i == 0)
    def _():
        ki_s[...] = ki_ref[:, :IDX_DIM].astype(BF16)

    row = lax.broadcasted_iota(jnp.int32, (tq, tq), 0)
    col = lax.broadcasted_iota(jnp.int32, (tq, tq), 1)

    def visible(c):
        return col <= row + jnp.where(c == i, 0, tq)

    w_all = wi_ref[:, IDX_DIM:IDX_DIM + n_idx_heads] * ((IDX_DIM ** -0.5) * (n_idx_heads ** -0.5))
    for c in range(n_chunks):
        @pl.when(c < n_valid)
        def _(c=c):
            kic = ki_s[c * tq:(c + 1) * tq, :]
            s = None
            for h in range(n_idx_heads):
                d = lax.dot_general(qi_ref[:, h * IDX_DIM:(h + 1) * IDX_DIM], kic, _NT,
                                    preferred_element_type=F32)
                term = jnp.maximum(d, 0.0) * w_all[:, h:h + 1]
                s = term if s is None else s + term
            key_s[c] = _ordered_key(jnp.where(visible(c), s, NEG_INF))

    def count_ge(cand):
        def body(c, acc):
            hit = (key_s[c] >= cand).astype(F32)
            part = hit[:, 0:LANES]
            for s in range(1, tq // LANES):
                part = part + hit[:, s * LANES:(s + 1) * LANES]
            return acc + part

        acc = lax.fori_loop(0, n_valid, body, jnp.zeros((tq, LANES), F32))
        return jnp.sum(acc, axis=1, keepdims=True)

    thr = _kth_largest_key(count_ge, tq, n_sel)
    for c in range(n_chunks):
        @pl.when(c < n_valid)
        def _(c=c):
            ok = (key_s[c] >= thr) & visible(c)
            bias_s[:, c * tq:(c + 1) * tq] = jnp.where(ok, 0.0, NEG_INF)

        @pl.when(c >= n_valid)
        def _(c=c):
            bias_s[:, c * tq:(c + 1) * tq] = jnp.full((tq, tq), NEG_INF, F32)

    def kv_cols(h):
        return slice((h // groups) * HEAD_DIM, (h // groups + 1) * HEAD_DIM)

    def qk(h):
        return lax.dot_general(qa_ref[:, h * HEAD_DIM:(h + 1) * HEAD_DIM], ka_ref[:, kv_cols(h)], _NT,
                               preferred_element_type=F32)

    raw = qk(0)
    for h in range(n_heads):
        nxt = qk(h + 1) if h + 1 < n_heads else None
        logits = raw * (HEAD_DIM ** -0.5) + bias_s[...]
        o_ref[:, h * HEAD_DIM:(h + 1) * HEAD_DIM] = _softmax_pv(logits, va_ref[:, kv_cols(h)]).astype(o_ref.dtype)
        raw = nxt


def _dsa_prompt(hq, hk_bf, hv_bf, hqi, hkiwi, *, batch, seq, n_heads, n_idx_heads, groups):
    tq = min(256, seq)
    assert seq % tq == 0 and tq % LANES == 0
    nq = seq // tq
    aw = n_heads * HEAD_DIM
    kvw = aw // groups
    n_sel = min(IDX_TOPK, seq // 4)
    kern = functools.partial(_dsa_prompt_kernel, tq=tq, seq=seq, n_sel=n_sel, n_heads=n_heads,
                             n_idx_heads=n_idx_heads, groups=groups)
    return pl.pallas_call(
        kern,
        out_shape=jax.ShapeDtypeStruct((batch * seq, aw), BF16),
        grid=(batch, nq),
        in_specs=[pl.BlockSpec((tq, aw), lambda b, i: (b * nq + i, 0)),
                  pl.BlockSpec((seq, kvw), lambda b, i: (b, 0)),
                  pl.BlockSpec((seq, kvw), lambda b, i: (b, 0)),
                  pl.BlockSpec((tq, n_idx_heads * IDX_DIM), lambda b, i: (b * nq + i, 0)),
                  pl.BlockSpec((seq, LANES), lambda b, i: (b, 0)),
                  pl.BlockSpec((tq, LANES), lambda b, i: (b * nq + i, 0))],
        out_specs=pl.BlockSpec((tq, aw), lambda b, i: (b * nq + i, 0)),
        scratch_shapes=[pltpu.VMEM((seq, IDX_DIM), BF16), pltpu.VMEM((nq, tq, tq), jnp.int32),
                        pltpu.VMEM((tq, seq), F32)],
        compiler_params=_params(("parallel", "arbitrary")),
        name="dsa_prompt",
    )(hq, hk_bf, hv_bf, hqi, hkiwi, hkiwi)


def _moba_prompt_kernel(qb_ref, kb_ref, kbf_ref, vb_ref, o_ref, km_s, *, nb, n_sel, n_heads, groups):
    i = pl.program_id(1)
    blk = MOBA_BLOCK

    @pl.when(i == 0)
    def _():
        km_s[...] = jnp.zeros_like(km_s)
        for j in range(nb):
            km_s[j:j + 1, :] = jnp.mean(kbf_ref[j * blk:(j + 1) * blk, :], axis=0, keepdims=True)

    past = lax.broadcasted_iota(jnp.int32, (blk, LANES), 1) < i
    tril = (lax.broadcasted_iota(jnp.int32, (blk, blk), 1)
            <= lax.broadcasted_iota(jnp.int32, (blk, blk), 0)).astype(F32)
    def kv_cols(h):
        return slice((h // groups) * HEAD_DIM, (h // groups + 1) * HEAD_DIM)

    def qk(h):
        q = qb_ref[:, h * HEAD_DIM:(h + 1) * HEAD_DIM]
        gate = lax.dot_general(q, km_s[:, kv_cols(h)].astype(BF16), _NT, preferred_element_type=F32)
        return gate, lax.dot_general(q, kb_ref[:, kv_cols(h)], _NT, preferred_element_type=F32)

    cur = qk(0)
    for h in range(n_heads):
        nxt = qk(h + 1) if h + 1 < n_heads else None
        gate, raw = cur
        selw = _topk_lanes(gate, past, n_sel).astype(F32)
        pieces = []
        for j in range(nb):
            allow = jnp.where(j == i, tril, selw[:, j:j + 1])
            pieces.append(jnp.where(allow > 0.0, raw[:, j * blk:(j + 1) * blk] * (HEAD_DIM ** -0.5), NEG_INF))
        o = _softmax_pv(jnp.concatenate(pieces, axis=1), vb_ref[:, kv_cols(h)])
        o_ref[:, h * HEAD_DIM:(h + 1) * HEAD_DIM] = o.astype(o_ref.dtype)
        cur = nxt


def _moba_prompt(hq, hk_bf, hk_f32, hv_bf, *, batch, seq, n_heads, groups):
    blk = MOBA_BLOCK
    assert seq % blk == 0
    nb = seq // blk
    assert 1 <= nb <= LANES
    aw = n_heads * HEAD_DIM
    kvw = aw // groups
    kern = functools.partial(_moba_prompt_kernel, nb=nb, n_sel=min(MOBA_TOPK, nb), n_heads=n_heads,
                             groups=groups)
    return pl.pallas_call(
        kern,
        out_shape=jax.ShapeDtypeStruct((batch * seq, aw), BF16),
        grid=(batch, nb),
        in_specs=[pl.BlockSpec((blk, aw), lambda b, i: (b * nb + i, 1)),
                  pl.BlockSpec((seq, kvw), lambda b, i: (b, 1)),
                  pl.BlockSpec((seq, kvw), lambda b, i: (b, 1)),
                  pl.BlockSpec((seq, kvw), lambda b, i: (b, 1))],
        out_specs=pl.BlockSpec((blk, aw), lambda b, i: (b * nb + i, 0)),
        scratch_shapes=[pltpu.VMEM((LANES, kvw), F32)],
        compiler_params=_params(("parallel", "arbitrary")),
        name="moba_prompt",
    )(hq, hk_bf, hk_f32, hv_bf)


def _gmlp_kernel(cu_ref, cv_ref, g_ref, b_ref, ws_ref, bs_ref, o_ref, *vn_out, n_groups):
    v = cv_ref[...]
    mu = jnp.mean(v, axis=-1, keepdims=True)
    vc = v - mu
    vn = vc * lax.rsqrt(jnp.mean(vc * vc, axis=-1, keepdims=True) + EPS) * g_ref[...] + b_ref[...]
    if vn_out:
        vn_out[0][...] = vn
    for g in range(n_groups):
        cols = slice(g * C_GROUP_DIM, (g + 1) * C_GROUP_DIM)
        z = jnp.dot(ws_ref[g], vn[:, cols].astype(BF16), preferred_element_type=F32) + bs_ref[:, g:g + 1]
        o_ref[:, cols] = (cu_ref[:, cols] * z).astype(o_ref.dtype)


def _gmlp(hc, ln_g, ln_b, ws, bs, *, want_vn):
    m = hc.shape[0]
    n_groups, r = ws.shape[0], ws.shape[1]
    cw = n_groups * C_GROUP_DIM
    out_shape = [jax.ShapeDtypeStruct((m, cw), BF16)]
    out_specs = [pl.BlockSpec((r, cw), lambda i: (i, 0))]
    if want_vn:
        out_shape.append(jax.ShapeDtypeStruct((m, cw), F32))
        out_specs.append(pl.BlockSpec((r, cw), lambda i: (i, 0)))
    return pl.pallas_call(
        functools.partial(_gmlp_kernel, n_groups=n_groups),
        out_shape=out_shape,
        grid=(m // r,),
        in_specs=[pl.BlockSpec((r, cw), lambda i: (i, 0)),
                  pl.BlockSpec((r, cw), lambda i: (i, 1)),
                  pl.BlockSpec((1, cw), lambda i: (0, 0)),
                  pl.BlockSpec((1, cw), lambda i: (0, 0)),
                  pl.BlockSpec((n_groups, r, r), lambda i: (0, 0, 0)),
                  pl.BlockSpec((r, n_groups), lambda i: (0, 0))],
        out_specs=out_specs,
        compiler_params=_params(("parallel",)),
        name="gmlp",
    )(hc, hc, ln_g.reshape(1, cw), ln_b.reshape(1, cw), ws, bs)


def _merge_kernel(oa_ref, ob_ref, oc_ref, wb_ref, g0_ref, g1_ref, g2_ref, o_ref):
    acc = None
    for o_n, g_n, n in ((oa_ref, g0_ref, 0), (ob_ref, g1_ref, 1), (oc_ref, g2_ref, 2)):
        br = jnp.dot(o_n[...], wb_ref[n], preferred_element_type=F32)
        term = (1.0 / (1.0 + jnp.exp(-g_n[...]))) * br
        acc = term if acc is None else acc + term
    o_ref[...] = acc.astype(o_ref.dtype)


def _merge(o_a, o_b, o_c, wb, hc, *, d_model, gate_col0):
    m, aw = o_a.shape
    tm, tn = min(512, m), 512
    nj = d_model // tn
    g0 = gate_col0 // tn
    gate_specs = [pl.BlockSpec((tm, tn), functools.partial(lambda i, j, n: (i, g0 + n * nj + j), n=n))
                  for n in range(N_BRANCH)]
    return pl.pallas_call(
        _merge_kernel,
        out_shape=jax.ShapeDtypeStruct((m, d_model), BF16),
        grid=(m // tm, nj),
        in_specs=[pl.BlockSpec((tm, aw), lambda i, j: (i, 0))] * 3
        + [pl.BlockSpec((N_BRANCH, aw, tn), lambda i, j: (0, 0, j))] + gate_specs,
        out_specs=pl.BlockSpec((tm, tn), lambda i, j: (i, j)),
        compiler_params=_params(("parallel", "parallel")),
        name="merge",
    )(o_a, o_b, o_c, wb, hc, hc, hc)


def _page_specs(block, layer, n_pages, per_step):
    def page_map(b, s, pt, slot):
        return (layer, pt[b * n_pages + s * per_step + slot], 0, 0)

    return [pl.BlockSpec(block, functools.partial(page_map, slot=slot)) for slot in range(per_step)]


def _idx_scores(qi, wi, k, n_new, n_idx_heads):
    d = lax.dot_general(qi, k, _NT, preferred_element_type=F32) * (IDX_DIM ** -0.5)
    dw = jnp.maximum(d, 0.0) * wi
    acc = dw[0:8]
    for r in range(8, n_idx_heads * n_new, 8):
        acc = acc + dw[r:r + 8]
    s = acc[0:n_new]
    for r in range(n_new, 8, n_new):
        s = s + acc[r:r + n_new]
    return s * (n_idx_heads ** -0.5)


def _dsa_sample_score_kernel(pt_ref, *refs, per_step, n_new, n_idx_heads):
    pages = refs[:per_step]
    qi_ref, wi_ref, o_ref = refs[per_step:]
    for g in range(per_step):
        o_ref[:, g * LANES:(g + 1) * LANES] = _idx_scores(qi_ref[...], wi_ref[...], pages[g][...], n_new,
                                                         n_idx_heads)


def _dsa_sample_scores(pt_flat, cache_idx, layer, qi_rows, wi_rows, *, dec_batch, n_pages, n_new, n_idx_heads):
    page = cache_idx.shape[2]
    assert page == LANES and 8 % n_new == 0 and (n_idx_heads * n_new) % 8 == 0
    per_step = min(PAGES_PER_STEP, n_pages)
    assert n_pages % per_step == 0
    rows = n_idx_heads * n_new
    kern = functools.partial(_dsa_sample_score_kernel, per_step=per_step, n_new=n_new, n_idx_heads=n_idx_heads)
    grid_spec = pltpu.PrefetchScalarGridSpec(
        num_scalar_prefetch=1,
        grid=(dec_batch, n_pages // per_step),
        in_specs=_page_specs((None, None, page, IDX_DIM), layer, n_pages, per_step)
        + [pl.BlockSpec((None, rows, IDX_DIM), lambda b, s, pt: (b, 0, 0)),
           pl.BlockSpec((None, rows, 1), lambda b, s, pt: (b, 0, 0))],
        out_specs=pl.BlockSpec((None, n_new, per_step * LANES), lambda b, s, pt: (b, 0, s)),
    )
    return pl.pallas_call(
        kern,
        out_shape=jax.ShapeDtypeStruct((dec_batch, n_new, n_pages * LANES), F32),
        grid_spec=grid_spec,
        compiler_params=_params(("parallel", "arbitrary")),
        name="dsa_sample_scores",
    )(pt_flat, *([cache_idx] * per_step), qi_rows, wi_rows)


def _select_kernel(s_ref, qi_ref, wi_ref, kn_ref, o_ref, on_ref, key_s, keyn_s, *, n_sel, n_new, n_idx_heads):
    t = lax.broadcasted_iota(jnp.int32, (n_new, LANES), 0)
    c = lax.broadcasted_iota(jnp.int32, (n_new, LANES), 1)
    new_ok = c <= t
    s_new = _idx_scores(qi_ref[...], wi_ref[...], kn_ref[...], n_new, n_idx_heads)
    key_s[...] = _ordered_key(s_ref[...])
    keyn_s[...] = _ordered_key(jnp.where(new_ok, s_new, NEG_INF))

    def count_ge(cand):
        return (jnp.sum((key_s[...] >= cand).astype(F32), axis=1, keepdims=True)
                + jnp.sum((keyn_s[...] >= cand).astype(F32), axis=1, keepdims=True))

    thr = _kth_largest_key(count_ge, n_new, n_sel)
    o_ref[...] = (key_s[...] >= thr).astype(F32)
    on_ref[...] = ((keyn_s[...] >= thr) & new_ok).astype(F32)


def _dsa_sample_select(scores, qi_rows, wi_rows, ki_new_pad, n_sel, *, n_idx_heads):
    db, n_new, length = scores.shape
    rows = n_idx_heads * n_new
    return pl.pallas_call(
        functools.partial(_select_kernel, n_sel=n_sel, n_new=n_new, n_idx_heads=n_idx_heads),
        out_shape=[jax.ShapeDtypeStruct(scores.shape, F32), jax.ShapeDtypeStruct((db, n_new, LANES), F32)],
        grid=(db,),
        in_specs=[pl.BlockSpec((None, n_new, length), lambda b: (b, 0, 0)),
                  pl.BlockSpec((None, rows, IDX_DIM), lambda b: (b, 0, 0)),
                  pl.BlockSpec((None, rows, 1), lambda b: (b, 0, 0)),
                  pl.BlockSpec((None, LANES, IDX_DIM), lambda b: (b, 0, 0))],
        out_specs=[pl.BlockSpec((None, n_new, length), lambda b: (b, 0, 0)),
                   pl.BlockSpec((None, n_new, LANES), lambda b: (b, 0, 0))],
        scratch_shapes=[pltpu.VMEM((n_new, length), jnp.int32), pltpu.VMEM((n_new, LANES), jnp.int32)],
        compiler_params=_params(("parallel",)),
        name="dsa_sample_select",
    )(scores, qi_rows, wi_rows, ki_new_pad)


def _head_rows(page_refs, kv, n_kv):
    parts = [r[pl.ds(kv, LANES, stride=n_kv), :] for r in page_refs]
    return parts[0] if len(parts) == 1 else jnp.concatenate(parts, axis=0)


def _dsa_sample_attn_kernel(pt_ref, *refs, per_step, n_steps, n_new, n_kv, groups):
    k_pages, v_pages = refs[:per_step], refs[per_step:2 * per_step]
    kn_ref, vn_ref, q_ref, sel_ref, seln_ref, o_ref, m_s, l_s, acc_s = refs[2 * per_step:]
    step = pl.program_id(1)
    rpk = groups * n_new
    scale = HEAD_DIM ** -0.5

    @pl.when(step == 0)
    def _():
        m_s[...] = jnp.full_like(m_s, NEG_INF)
        l_s[...] = jnp.zeros_like(l_s)
        acc_s[...] = jnp.zeros_like(acc_s)

    def update(get_k, get_v, sel):
        mask = jnp.concatenate([sel] * groups, axis=0) > 0.0
        rows = [slice(kv * rpk, (kv + 1) * rpk) for kv in range(n_kv)]
        logits = [lax.dot_general(q_ref[rows[kv], :], get_k(kv), _NT, preferred_element_type=F32)
                  for kv in range(n_kv)]
        probs, alphas = [], []
        for kv in range(n_kv):
            s = jnp.where(mask, logits[kv] * scale, NEG_INF)
            m_old = m_s[rows[kv], :]
            m_new = jnp.maximum(m_old, jnp.max(s, axis=1, keepdims=True))
            m_safe = jnp.where(m_new == NEG_INF, 0.0, m_new)
            alpha = jnp.exp(m_old - m_safe)
            pr = jnp.exp(s - m_safe)
            m_s[rows[kv], :] = m_new
            l_s[rows[kv], :] = alpha * l_s[rows[kv], :] + jnp.sum(pr, axis=1, keepdims=True)
            probs.append(pr)
            alphas.append(alpha)
        for kv in range(n_kv):
            acc_s[rows[kv], :] = alphas[kv] * acc_s[rows[kv], :] + jnp.dot(
                probs[kv], get_v(kv), preferred_element_type=F32)

    update(lambda kv: _head_rows(k_pages, kv, n_kv), lambda kv: _head_rows(v_pages, kv, n_kv), sel_ref[...])

    @pl.when(step == n_steps - 1)
    def _():
        update(lambda kv: kn_ref[kv], lambda kv: vn_ref[kv], seln_ref[...])
        o_ref[...] = acc_s[...] / l_s[...]


def _dsa_sample_attn(pt_flat, cache_k, cache_v, layer, k_new_pad, v_new_pad, q_rows, sel, sel_new, *, dec_batch,
                     n_pages, n_new, n_kv, groups):
    rows = n_kv * groups * n_new
    prow = cache_k.shape[2]
    per_step = min(PAGES_PER_STEP, n_pages)
    assert n_pages % per_step == 0
    n_steps = n_pages // per_step
    kern = functools.partial(_dsa_sample_attn_kernel, per_step=per_step, n_steps=n_steps, n_new=n_new, n_kv=n_kv,
                             groups=groups)
    page_specs = _page_specs((None, None, prow, HEAD_DIM), layer, n_pages, per_step)
    grid_spec = pltpu.PrefetchScalarGridSpec(
        num_scalar_prefetch=1,
        grid=(dec_batch, n_steps),
        in_specs=page_specs + page_specs
        + [pl.BlockSpec((None, n_kv, LANES, HEAD_DIM), lambda b, s, pt: (b, 0, 0, 0)),
           pl.BlockSpec((None, n_kv, LANES, HEAD_DIM), lambda b, s, pt: (b, 0, 0, 0)),
           pl.BlockSpec((None, rows, HEAD_DIM), lambda b, s, pt: (b, 0, 0)),
           pl.BlockSpec((None, n_new, per_step * LANES), lambda b, s, pt: (b, 0, s)),
           pl.BlockSpec((None, n_new, LANES), lambda b, s, pt: (b, 0, 0))],
        out_specs=pl.BlockSpec((None, rows, HEAD_DIM), lambda b, s, pt: (b, 0, 0)),
        scratch_shapes=[pltpu.VMEM((rows, 1), F32), pltpu.VMEM((rows, 1), F32),
                        pltpu.VMEM((rows, HEAD_DIM), F32)],
    )
    return pl.pallas_call(
        kern,
        out_shape=jax.ShapeDtypeStruct((dec_batch, rows, HEAD_DIM), F32),
        grid_spec=grid_spec,
        compiler_params=_params(("parallel", "arbitrary")),
        name="dsa_sample_attn",
    )(pt_flat, *([cache_k] * per_step), *([cache_v] * per_step), k_new_pad, v_new_pad, q_rows, sel, sel_new)


def _moba_sample_kernel(pt_ref, *refs, bps, n_steps, nb, n_sel, n_new, n_kv, groups, past_len):
    k_pages, v_pages = refs[:2 * bps], refs[2 * bps:4 * bps]
    kn_ref, vn_ref, q_ref, o_ref, gate_s, m_s, l_s, acc_s = refs[4 * bps:]
    step = pl.program_id(1)
    rpk = groups * n_new
    rows_all = n_kv * rpk
    scale = HEAD_DIM ** -0.5
    lane = lax.broadcasted_iota(jnp.int32, (rows_all, LANES), 1)

    @pl.when(step == 0)
    def _():
        gate_s[...] = jnp.zeros_like(gate_s)
        m_s[...] = jnp.zeros_like(m_s)
        l_s[...] = jnp.zeros_like(l_s)

    gate, m_blk, l_blk = gate_s[...], m_s[...], l_s[...]
    blk = MOBA_BLOCK
    kv_rows = [slice(kv * rpk, (kv + 1) * rpk) for kv in range(n_kv)]
    ks = [_head_rows(k_pages, kv, n_kv) for kv in range(n_kv)]
    logits = [lax.dot_general(q_ref[kv_rows[kv], :], ks[kv], _NT, preferred_element_type=F32)
              for kv in range(n_kv)]
    probs = {}
    for bb in range(bps):
        g_cols, m_cols, l_cols = [], [], []
        for kv in range(n_kv):
            kmean = jnp.mean(ks[kv][bb * blk:(bb + 1) * blk, :], axis=0, keepdims=True)
            g_cols.append(jnp.sum(q_ref[kv_rows[kv], :] * kmean, axis=1, keepdims=True))
            s = logits[kv][:, bb * blk:(bb + 1) * blk] * scale
            m = jnp.max(s, axis=1, keepdims=True)
            pr = jnp.exp(s - m)
            m_cols.append(m)
            l_cols.append(jnp.sum(pr, axis=1, keepdims=True))
            probs[bb, kv] = pr
        here = lane == step * bps + bb
        gate = jnp.where(here, jnp.concatenate(g_cols, axis=0), gate)
        m_blk = jnp.where(here, jnp.concatenate(m_cols, axis=0), m_blk)
        l_blk = jnp.where(here, jnp.concatenate(l_cols, axis=0), l_blk)
    for bb in range(bps):
        for kv in range(n_kv):
            v = _head_rows(v_pages[2 * bb:2 * bb + 2], kv, n_kv)
            acc_s[step * bps + bb, kv_rows[kv], :] = jnp.dot(probs[bb, kv], v, preferred_element_type=F32)
    gate_s[...] = gate
    m_s[...] = m_blk
    l_s[...] = l_blk

    @pl.when(step == n_steps - 1)
    def _():
        t_row = lax.broadcasted_iota(jnp.int32, (rpk, LANES), 0) & (n_new - 1)
        c_col = lax.broadcasted_iota(jnp.int32, (rpk, LANES), 1)
        own_ok = c_col <= t_row
        t_all = lax.broadcasted_iota(jnp.int32, (rows_all, LANES), 0) & (n_new - 1)
        own_blk = jnp.right_shift(past_len + t_all, MOBA_BLOCK.bit_length() - 1)
        valid = (lane < own_blk) & (lane < nb)
        sel = _topk_lanes(gate, valid, n_sel)
        for kv in range(n_kv):
            rows = slice(kv * rpk, (kv + 1) * rpk)
            q = q_ref[rows, :]
            s = lax.dot_general(q, kn_ref[kv], _NT, preferred_element_type=F32) * scale
            s = jnp.where(own_ok, s, NEG_INF)
            m_o = jnp.max(s, axis=1, keepdims=True)
            m_sel = jnp.where(sel[rows, :], m_blk[rows, :], NEG_INF)
            m_all = jnp.maximum(m_o, jnp.max(m_sel, axis=1, keepdims=True))
            pr = jnp.exp(s - m_all)
            w = jnp.where(sel[rows, :], jnp.exp(m_sel - m_all), 0.0)
            den = jnp.sum(pr, axis=1, keepdims=True) + jnp.sum(w * l_blk[rows, :], axis=1, keepdims=True)
            num = jnp.dot(pr, vn_ref[kv], preferred_element_type=F32)
            for jb in range(nb):
                num = num + w[:, jb:jb + 1] * acc_s[jb, rows, :]
            o_ref[rows, :] = num / den


def _moba_sample(pt_flat, cache_k, cache_v, layer, k_new_pad, v_new_pad, q_rows, *, dec_batch, n_pages, n_new,
                 n_kv, groups, past_len):
    assert past_len % MOBA_BLOCK == 0 and n_new < MOBA_BLOCK and MOBA_BLOCK == 2 * LANES
    assert n_new & (n_new - 1) == 0
    nb = (past_len + n_new) // MOBA_BLOCK
    assert 1 <= nb <= LANES and n_pages == 2 * nb
    bps = min(PAGES_PER_STEP // 2, nb)
    assert nb % bps == 0
    n_steps = nb // bps
    rows = n_kv * groups * n_new
    prow = cache_k.shape[2]
    kern = functools.partial(_moba_sample_kernel, bps=bps, n_steps=n_steps, nb=nb, n_sel=min(MOBA_TOPK, nb),
                             n_new=n_new, n_kv=n_kv, groups=groups, past_len=past_len)
    page_specs = _page_specs((None, None, prow, HEAD_DIM), layer, n_pages, 2 * bps)
    grid_spec = pltpu.PrefetchScalarGridSpec(
        num_scalar_prefetch=1,
        grid=(dec_batch, n_steps),
        in_specs=page_specs + page_specs
        + [pl.BlockSpec((None, n_kv, LANES, HEAD_DIM), lambda b, s, pt: (b, 0, 0, 0)),
           pl.BlockSpec((None, n_kv, LANES, HEAD_DIM), lambda b, s, pt: (b, 0, 0, 0)),
           pl.BlockSpec((None, rows, HEAD_DIM), lambda b, s, pt: (b, 0, 0))],
        out_specs=pl.BlockSpec((None, rows, HEAD_DIM), lambda b, s, pt: (b, 0, 0)),
        scratch_shapes=[pltpu.VMEM((rows, LANES), F32), pltpu.VMEM((rows, LANES), F32),
                        pltpu.VMEM((rows, LANES), F32), pltpu.VMEM((nb, rows, HEAD_DIM), F32)],
    )
    return pl.pallas_call(
        kern,
        out_shape=jax.ShapeDtypeStruct((dec_batch, rows, HEAD_DIM), F32),
        grid_spec=grid_spec,
        compiler_params=_params(("parallel", "arbitrary")),
        name="moba_sample",
    )(pt_flat, *([cache_k] * (2 * bps)), *([cache_v] * (2 * bps)), k_new_pad, v_new_pad, q_rows)


def _rope_tables(pos, rot, width, n_rope_lanes=LANES):
    half = rot // 2
    inv = jnp.power(ROPE_THETA, -jnp.arange(half, dtype=F32) / half)
    ang = pos.astype(F32)[:, None] * inv[None, :]
    cos, sin = jnp.cos(ang), jnp.sin(ang)
    t = pos.shape[0]
    zeros = jnp.zeros((t, half), F32)
    rest0 = jnp.zeros((t, width - rot), F32)
    c = jnp.concatenate([cos, cos, jnp.ones((t, width - rot), F32)], axis=1)
    sa = jnp.concatenate([-sin, zeros, rest0], axis=1)
    sb = jnp.concatenate([zeros, sin, rest0], axis=1)
    reps = n_rope_lanes // width
    pad = LANES - n_rope_lanes
    c = jnp.concatenate([jnp.tile(c, (1, reps)), jnp.ones((t, pad), F32)], axis=1)
    sa = jnp.concatenate([jnp.tile(sa, (1, reps)), jnp.zeros((t, pad), F32)], axis=1)
    sb = jnp.concatenate([jnp.tile(sb, (1, reps)), jnp.zeros((t, pad), F32)], axis=1)
    return c, sa, sb


def _kv_major_rows(a, db, n_new, n_kv, groups):
    a = a.reshape(db, n_new, n_kv, groups, HEAD_DIM)
    return jnp.transpose(a, (0, 2, 3, 1, 4)).reshape(db, n_kv * groups * n_new, HEAD_DIM)


def _from_kv_major_rows(o, db, n_new, n_kv, groups):
    o = o.reshape(db, n_kv, groups, n_new, HEAD_DIM)
    return jnp.transpose(o, (0, 3, 1, 2, 4)).reshape(db * n_new, n_kv * groups * HEAD_DIM)


def _new_kv_pad(k, db, n_new, n_kv):
    k = jnp.transpose(k.reshape(db, n_new, n_kv, HEAD_DIM), (0, 2, 1, 3))
    return jnp.pad(k, ((0, 0), (0, 0), (0, LANES - n_new), (0, 0)))


def kernel(x_prompt, x_sample, cache_a_k, cache_a_v, cache_a_idx, cache_b_k, cache_b_v, page_table, w_in, norm1_g,
           c_ln_g, c_ln_b, w_spatial, b_spatial, w_branch, w_out, norm2_g, w_ff1, w_ff2, norm_f_g):
    batch, seq, d_model = x_prompt.shape
    db, n_new, _ = x_sample.shape
    depth, n_phys, page, n_kv, _ = cache_a_k.shape
    n_pages = page_table.shape[1]
    past_len = n_pages * page
    n_heads = d_model // 256
    groups = n_heads // n_kv
    aw = n_heads * HEAD_DIM
    kvw = n_kv * HEAD_DIM
    n_idx_heads = d_model // 128
    iw = n_idx_heads * IDX_DIM
    n_cg = w_spatial.shape[1]
    cw = n_cg * C_GROUP_DIM
    rot, idx_rot = HEAD_DIM // 4, IDX_DIM // 4
    assert page == LANES and aw == 2 * kvw and cw % 512 == 0 and seq % CHUNK == 0 and n_new <= CHUNK

    sizes = (aw, kvw, kvw, iw, IDX_DIM, n_idx_heads, aw, kvw, kvw, cw, cw, N_BRANCH * d_model)
    offs = [0]
    for s in sizes:
        offs.append(offs[-1] + s)
    (o_qa, o_ka, o_va, o_qi, o_ki, o_wi, o_qb, o_kb, o_vb, o_cu, o_cv, o_gl, _) = offs

    pos_p = jnp.arange(seq, dtype=jnp.int32)
    pos_s = jnp.tile(past_len + jnp.arange(n_new, dtype=jnp.int32), db)
    tabs = {
        "p": (_rope_tables(pos_p, rot, HEAD_DIM), _rope_tables(pos_p, idx_rot, IDX_DIM),
              _rope_tables(pos_p, idx_rot, IDX_DIM, IDX_DIM)),
        "s": (_rope_tables(pos_s, rot, HEAD_DIM), _rope_tables(pos_s, idx_rot, IDX_DIM),
              _rope_tables(pos_s, idx_rot, IDX_DIM, IDX_DIM)),
    }

    cache_ak = cache_a_k.reshape(depth, n_phys, page * n_kv, HEAD_DIM)
    cache_av = cache_a_v.reshape(depth, n_phys, page * n_kv, HEAD_DIM)
    cache_bk = cache_b_k.reshape(depth, n_phys, page * n_kv, HEAD_DIM)
    cache_bv = cache_b_v.reshape(depth, n_phys, page * n_kv, HEAD_DIM)
    pt_flat = page_table.reshape(-1).astype(jnp.int32)

    xp = x_prompt.reshape(batch * seq, d_model)
    xs = x_sample.reshape(db * n_new, d_model)
    rs = db * n_new
    tri = jnp.tril(jnp.ones((CHUNK, CHUNK), dtype=bool))
    sel_s = min(IDX_TOPK, (past_len + n_new) // 4)

    outs = {k: [] for k in ("pa_k", "pa_v", "pa_i", "pb_k", "pb_v", "sa_k", "sa_v", "sa_i", "sb_k", "sb_v", "sc_v")}

    def project(x, l, which):
        t128, t64, tkiwi = tabs[which]
        w = w_in[l]
        xn = _rms_norm(x, norm1_g[l], BF16)
        w_q = jnp.concatenate([w[:, o_qa:o_ka], w[:, o_qb:o_kb]], axis=1).astype(BF16)
        w_k = jnp.concatenate([w[:, o_ka:o_va], w[:, o_kb:o_vb]], axis=1).astype(BF16)
        w_v = jnp.concatenate([w[:, o_va:o_qi], w[:, o_vb:o_cu]], axis=1).astype(BF16)
        w_qi = w[:, o_qi:o_ki].astype(BF16)
        w_kiwi = jnp.pad(w[:, o_ki:o_qb], ((0, 0), (0, LANES - IDX_DIM - n_idx_heads))).astype(BF16)
        w_c = w[:, o_cu:].astype(BF16)
        (hq,) = _matmul(xn, w_q, (BF16,), "rope", t128, rot // 2)
        hk, hk_bf = _matmul(xn, w_k, (F32, BF16), "rope", t128, rot // 2)
        hv, hv_bf = _matmul(xn, w_v, (F32, BF16))
        (hqi,) = _matmul(xn, w_qi, (BF16,), "rope", t64, idx_rot // 2)
        (hkiwi,) = _matmul(xn, w_kiwi, (F32,), "rope", tkiwi, idx_rot // 2)
        (hc,) = _matmul(xn, w_c, (F32,))
        return hq, hk, hk_bf, hv, hv_bf, hqi, hkiwi, hc

    def merge_ffn(x, o_a, o_b, o_c, hc, l):
        merged = _merge(o_a, o_b, o_c, w_branch[l].astype(BF16), hc, d_model=d_model, gate_col0=2 * cw)
        (x1,) = _matmul(merged, w_out[l].astype(BF16), (F32,), "residual", (x,))
        hn = _rms_norm(x1, norm2_g[l], BF16)
        (h1,) = _matmul(hn, w_ff1[l].astype(BF16), (BF16,), "relu2")
        (x2,) = _matmul(h1, w_ff2[l].astype(BF16), (F32,), "residual", (x1,))
        return x2

    for l in range(depth):
        hq, hk, hk_bf, hv, hv_bf, hqi, hkiwi, hc = project(xp, l, "p")
        o_a = _dsa_prompt(hq, hk_bf, hv_bf, hqi, hkiwi, batch=batch, seq=seq, n_heads=n_heads,
                          n_idx_heads=n_idx_heads, groups=groups)
        o_b = _moba_prompt(hq, hk_bf, hk, hv_bf, batch=batch, seq=seq, n_heads=n_heads, groups=groups)
        ws_p = jnp.where(tri[None], w_spatial[l], 0).astype(BF16)
        bs_p = b_spatial[l].T
        (o_c,) = _gmlp(hc, c_ln_g[l], c_ln_b[l], ws_p, bs_p, want_vn=False)
        xp = merge_ffn(xp, o_a, o_b, o_c, hc, l)
        outs["pa_k"].append(hk[:, :kvw].reshape(batch, seq, n_kv, HEAD_DIM))
        outs["pa_v"].append(hv[:, :kvw].reshape(batch, seq, n_kv, HEAD_DIM))
        outs["pa_i"].append(hkiwi[:, :IDX_DIM].reshape(batch, seq, IDX_DIM))
        outs["pb_k"].append(hk[:, kvw:].reshape(batch, seq, n_kv, HEAD_DIM))
        outs["pb_v"].append(hv[:, kvw:].reshape(batch, seq, n_kv, HEAD_DIM))

        hq, hk, hk_bf, hv, hv_bf, hqi, hkiwi, hc = project(xs, l, "s")
        hq_f = hq.astype(F32)
        qi_rows = jnp.transpose(hqi.astype(F32).reshape(db, n_new, n_idx_heads, IDX_DIM), (0, 2, 1, 3)).reshape(
            db, n_idx_heads * n_new, IDX_DIM)
        wi_rows = jnp.transpose(hkiwi[:, IDX_DIM:IDX_DIM + n_idx_heads].reshape(db, n_new, n_idx_heads),
                                (0, 2, 1)).reshape(db, n_idx_heads * n_new, 1)
        ki_new = jnp.pad(hkiwi[:, :IDX_DIM].reshape(db, n_new, IDX_DIM), ((0, 0), (0, LANES - n_new), (0, 0)))
        scores = _dsa_sample_scores(pt_flat, cache_a_idx, l, qi_rows, wi_rows, dec_batch=db, n_pages=n_pages,
                                    n_new=n_new, n_idx_heads=n_idx_heads)
        sel, sel_new = _dsa_sample_select(scores, qi_rows, wi_rows, ki_new, sel_s, n_idx_heads=n_idx_heads)
        qa_rows = _kv_major_rows(hq_f[:, :aw], db, n_new, n_kv, groups)
        o_a = _dsa_sample_attn(pt_flat, cache_ak, cache_av, l, _new_kv_pad(hk[:, :kvw], db, n_new, n_kv),
                               _new_kv_pad(hv[:, :kvw], db, n_new, n_kv), qa_rows, sel, sel_new, dec_batch=db,
                               n_pages=n_pages, n_new=n_new, n_kv=n_kv, groups=groups)
        o_a = _from_kv_major_rows(o_a, db, n_new, n_kv, groups).astype(BF16)
        qb_rows = _kv_major_rows(hq_f[:, aw:], db, n_new, n_kv, groups)
        o_b = _moba_sample(pt_flat, cache_bk, cache_bv, l, _new_kv_pad(hk[:, kvw:], db, n_new, n_kv),
                           _new_kv_pad(hv[:, kvw:], db, n_new, n_kv), qb_rows, dec_batch=db, n_pages=n_pages,
                           n_new=n_new, n_kv=n_kv, groups=groups, past_len=past_len)
        o_b = _from_kv_major_rows(o_b, db, n_new, n_kv, groups).astype(BF16)
        ws_c = jnp.where(tri[None, :n_new, :n_new], w_spatial[l][:, :n_new, :n_new], 0)
        ws_s = jnp.einsum("ab,gij->gaibj", jnp.eye(db, dtype=F32), ws_c).reshape(n_cg, rs, rs).astype(BF16)
        bs_s = jnp.tile(b_spatial[l][:, :n_new].T, (db, 1))
        o_c, vn = _gmlp(hc, c_ln_g[l], c_ln_b[l], ws_s, bs_s, want_vn=True)
        xs = merge_ffn(xs, o_a, o_b, o_c, hc, l)
        outs["sa_k"].append(hk[:, :kvw].reshape(db, n_new, n_kv, HEAD_DIM))
        outs["sa_v"].append(hv[:, :kvw].reshape(db, n_new, n_kv, HEAD_DIM))
        outs["sa_i"].append(hkiwi[:, :IDX_DIM].reshape(db, n_new, IDX_DIM))
        outs["sb_k"].append(hk[:, kvw:].reshape(db, n_new, n_kv, HEAD_DIM))
        outs["sb_v"].append(hv[:, kvw:].reshape(db, n_new, n_kv, HEAD_DIM))
        outs["sc_v"].append(vn.reshape(db, n_new, cw))

    y_prompt = _rms_norm(xp, norm_f_g, F32).reshape(batch, seq, d_model)
    y_sample = _rms_norm(xs, norm_f_g, F32).reshape(db, n_new, d_model)
    st = {k: jnp.stack(v) for k, v in outs.items()}
    return (y_prompt, y_sample, st["pa_k"], st["pa_v"], st["pa_i"], st["pb_k"], st["pb_v"],
            st["sa_k"], st["sa_v"], st["sa_i"], st["sb_k"], st["sb_v"], st["sc_v"])
```

```python
import functools

import jax
import jax.numpy as jnp
from jax import lax
from jax.experimental import pallas as pl
from jax.experimental.pallas import tpu as pltpu

HEAD_DIM = 128
ROPE_THETA = 500000.0
IDX_DIM = 64
IDX_TOPK = 256
MOBA_BLOCK = 256
MOBA_TOPK = 3
CHUNK = 128
C_GROUP_DIM = 128
N_BRANCH = 3
EPS = 1e-6
LANES = 128
INT32_MIN = -(2 ** 31)
PAGES_PER_STEP = 16

F32 = jnp.float32
BF16 = jnp.bfloat16
NEG_INF = float("-inf")
VMEM_LIMIT = 56 * 1024 * 1024

_NT = (((1,), (1,)), ((), ()))


def _params(sem):
    return pltpu.CompilerParams(dimension_semantics=sem, vmem_limit_bytes=VMEM_LIMIT)


def _rms_kernel(x_ref, g_ref, o_ref):
    x = x_ref[...]
    y = x * lax.rsqrt(jnp.mean(x * x, axis=-1, keepdims=True) + EPS)
    o_ref[...] = (y * g_ref[...]).astype(o_ref.dtype)


def _rms_norm(x, g, out_dtype):
    m, d = x.shape
    tm = min(256, m)
    return pl.pallas_call(
        _rms_kernel,
        out_shape=jax.ShapeDtypeStruct((m, d), out_dtype),
        grid=(m // tm,),
        in_specs=[pl.BlockSpec((tm, d), lambda i: (i, 0)),
                  pl.BlockSpec((1, d), lambda i: (0, 0))],
        out_specs=pl.BlockSpec((tm, d), lambda i: (i, 0)),
        compiler_params=_params(("parallel",)),
        name="rms_norm",
    )(x, g.reshape(1, d))


def _rope_epilogue(r, c, sa, sb, half):
    outs = []
    for s in range(r.shape[1] // LANES):
        a = r[:, s * LANES:(s + 1) * LANES]
        outs.append(a * c + pltpu.roll(a, LANES - half, 1) * sa + pltpu.roll(a, half, 1) * sb)
    return outs[0] if len(outs) == 1 else jnp.concatenate(outs, axis=1)


_N_EXTRA = {"none": 0, "relu2": 0, "rope": 3, "residual": 1}


def _epilogue(r, epi, extra, half, outs):
    if epi == "rope":
        r = _rope_epilogue(r, extra[0][...], extra[1][...], extra[2][...], half)
    elif epi == "relu2":
        r = jnp.square(jnp.maximum(r, 0.0))
    elif epi == "residual":
        r = extra[0][...] + r
    for o_ref in outs:
        o_ref[...] = r.astype(o_ref.dtype)


def _mm_ws_kernel(*refs, epi, n_out, half, w_rows_are_n):
    a_ref, w_ref = refs[0], refs[1]
    extra = refs[2:2 + _N_EXTRA[epi]]
    outs = refs[2 + _N_EXTRA[epi]:2 + _N_EXTRA[epi] + n_out]
    wbf_s = refs[-1]

    @pl.when(pl.program_id(1) == 0)
    def _():
        wbf_s[...] = (w_ref[0] if w_rows_are_n else w_ref[...]).astype(BF16)

    if w_rows_are_n:
        part = lax.dot_general(a_ref[...], wbf_s[...], _NT, preferred_element_type=F32)
    else:
        part = jnp.dot(a_ref[...], wbf_s[...], preferred_element_type=F32)
    _epilogue(part, epi, extra, half, outs)


def _matmul_ws(a, w_stack, layer, out_dtypes, epi="none", extra=(), half=0, n_off=0, n=None, w_rows_are_n=False,
               tm=1024, tn=1024):
    m, kdim = a.shape
    n_tot = w_stack.shape[1] if w_rows_are_n else w_stack.shape[2]
    n = n_tot - n_off if n is None else n
    tm, tn = min(tm, m), min(tn, n)
    if epi == "rope":
        tm = min(tm, extra[0].shape[0])
    assert m % tm == 0 and n % tn == 0 and n_off + n <= n_tot, (a.shape, w_stack.shape, tm, tn)
    in_specs = [pl.BlockSpec((tm, kdim), lambda j, i: (i, 0))]
    if w_rows_are_n:
        assert n_off % 8 == 0 and n_off + (n // tn) * tn <= n_tot
        in_specs.append(pl.BlockSpec((pl.Element(1), pl.Element(tn), pl.Element(kdim)),
                                     lambda j, i: (layer, pl.multiple_of(n_off + j * tn, 8), 0)))
        w_scratch = pltpu.VMEM((tn, kdim), BF16)
    else:
        assert n_off % tn == 0
        in_specs.append(pl.BlockSpec((None, kdim, tn), lambda j, i: (layer, 0, n_off // tn + j)))
        w_scratch = pltpu.VMEM((kdim, tn), BF16)
    if epi == "rope":
        nt = extra[0].shape[0] // tm
        in_specs += [pl.BlockSpec((tm, LANES), lambda j, i: (i % nt, 0))] * 3
    elif epi == "residual":
        in_specs += [pl.BlockSpec((tm, tn), lambda j, i: (i, j))]
    outs = pl.pallas_call(
        functools.partial(_mm_ws_kernel, epi=epi, n_out=len(out_dtypes), half=half, w_rows_are_n=w_rows_are_n),
        out_shape=[jax.ShapeDtypeStruct((m, n), dt) for dt in out_dtypes],
        grid=(n // tn, m // tm),
        in_specs=in_specs,
        out_specs=[pl.BlockSpec((tm, tn), lambda j, i: (i, j)) for _ in out_dtypes],
        scratch_shapes=[w_scratch],
        compiler_params=_params(("parallel", "arbitrary")),
        name="matmul_" + epi,
    )(a, w_stack, *extra)
    return tuple(outs)


def _mm_kt_kernel(*refs, epi, nk, n_out, half):
    a_ref, w_ref = refs[0], refs[1]
    extra = refs[2:2 + _N_EXTRA[epi]]
    outs = refs[2 + _N_EXTRA[epi]:2 + _N_EXTRA[epi] + n_out]
    acc_ref = refs[-1]
    k = pl.program_id(2)
    part = jnp.dot(a_ref[...], w_ref[...].astype(BF16), preferred_element_type=F32)

    @pl.when(k == 0)
    def _():
        acc_ref[...] = part

    @pl.when(k > 0)
    def _():
        acc_ref[...] += part

    @pl.when(k == nk - 1)
    def _():
        _epilogue(acc_ref[...], epi, extra, half, outs)


def _matmul_kt(a, w_stack, layer, out_dtypes, epi="none", extra=(), tm=1024, tn=1024, tk=2048):
    m, kdim = a.shape
    n = w_stack.shape[2]
    tm, tn, tk = min(tm, m), min(tn, n), min(tk, kdim)
    assert m % tm == 0 and n % tn == 0 and kdim % tk == 0 and kdim > tk, (a.shape, w_stack.shape, tm, tn, tk)
    nk = kdim // tk
    in_specs = [pl.BlockSpec((tm, tk), lambda i, j, k: (i, k)),
                pl.BlockSpec((None, tk, tn), lambda i, j, k: (layer, k, j))]
    if epi == "residual":
        in_specs += [pl.BlockSpec((tm, tn), lambda i, j, k: (i, j))]
    outs = pl.pallas_call(
        functools.partial(_mm_kt_kernel, epi=epi, nk=nk, n_out=len(out_dtypes), half=0),
        out_shape=[jax.ShapeDtypeStruct((m, n), dt) for dt in out_dtypes],
        grid=(m // tm, n // tn, nk),
        in_specs=in_specs,
        out_specs=[pl.BlockSpec((tm, tn), lambda i, j, k: (i, j)) for _ in out_dtypes],
        scratch_shapes=[pltpu.VMEM((tm, tn), F32)],
        compiler_params=_params(("parallel", "parallel", "arbitrary")),
        name="matmul_kt_" + epi,
    )(a, w_stack, *extra)
    return tuple(outs)


def _ordered_key(score):
    bits = pltpu.bitcast(score, jnp.int32)
    return jnp.where(bits < 0, bits ^ jnp.int32(0x7FFFFFFF), bits)


def _kth_largest_key(count_ge, rows, n_sel):
    def body(b, t):
        cand = t + lax.shift_left(jnp.int32(1), jnp.int32(31) - b)
        return jnp.where(count_ge(cand) >= float(n_sel), cand, t)

    return lax.fori_loop(0, 32, body, jnp.full((rows, 1), INT32_MIN, jnp.int32))


def _topk_lanes(g, valid, n_sel):
    lane = lax.broadcasted_iota(jnp.int32, g.shape, 1).astype(F32)
    gm = jnp.where(valid, g, NEG_INF)
    sel = jnp.zeros(g.shape, jnp.bool_)
    for _ in range(n_sel):
        mx = jnp.max(gm, axis=1, keepdims=True)
        first = jnp.min(jnp.where(gm == mx, lane, float(LANES)), axis=1, keepdims=True)
        pick = lane == first
        sel = sel | pick
        gm = jnp.where(pick, NEG_INF, gm)
    return sel & valid


def _softmax_pv(logits, v):
    p = jnp.exp(logits - jnp.max(logits, axis=1, keepdims=True))
    l = jnp.sum(p, axis=1, keepdims=True)
    return jnp.dot(p.astype(BF16), v, preferred_element_type=F32) / l


def _dsa_prompt_kernel(qa_ref, ka_ref, va_ref, qi_ref, ki_ref, wi_ref, o_ref, ki_s, key_s, bias_s,
                       *, tq, seq, n_sel, n_heads, n_idx_heads, groups):
    i = pl.program_id(1)
    n_chunks = seq // tq
    n_valid = i + 1

    @pl.when(i == 0)
    def _():
        ki_s[...] = ki_ref[:, :IDX_DIM].astype(BF16)

    row = lax.broadcasted_iota(jnp.int32, (tq, tq), 0)
    col = lax.broadcasted_iota(jnp.int32, (tq, tq), 1)

    def visible(c):
        return col <= row + jnp.where(c == i, 0, tq)

    w_all = wi_ref[:, IDX_DIM:IDX_DIM + n_idx_heads] * ((IDX_DIM ** -0.5) * (n_idx_heads ** -0.5))
    for c in range(n_chunks):
        @pl.when(c < n_valid)
        def _(c=c):
            kic = ki_s[c * tq:(c + 1) * tq, :]
            s = None
            for h in range(n_idx_heads):
                d = lax.dot_general(qi_ref[:, h * IDX_DIM:(h + 1) * IDX_DIM], kic, _NT,
                                    preferred_element_type=F32)
                term = jnp.maximum(d, 0.0) * w_all[:, h:h + 1]
                s = term if s is None else s + term
            key_s[c] = _ordered_key(jnp.where(visible(c), s, NEG_INF))

    def count_ge(cand):
        def body(c, acc):
            hit = (key_s[c] >= cand).astype(F32)
            part = hit[:, 0:LANES]
            for s in range(1, tq // LANES):
                part = part + hit[:, s * LANES:(s + 1) * LANES]
            return acc + part

        acc = lax.fori_loop(0, n_valid, body, jnp.zeros((tq, LANES), F32))
        return jnp.sum(acc, axis=1, keepdims=True)

    thr = _kth_largest_key(count_ge, tq, n_sel)
    for c in range(n_chunks):
        @pl.when(c < n_valid)
        def _(c=c):
            ok = (key_s[c] >= thr) & visible(c)
            bias_s[:, c * tq:(c + 1) * tq] = jnp.where(ok, 0.0, NEG_INF)

        @pl.when(c >= n_valid)
        def _(c=c):
            bias_s[:, c * tq:(c + 1) * tq] = jnp.full((tq, tq), NEG_INF, F32)

    def kv_cols(h):
        return slice((h // groups) * HEAD_DIM, (h // groups + 1) * HEAD_DIM)

    def qk(h):
        return lax.dot_general(qa_ref[:, h * HEAD_DIM:(h + 1) * HEAD_DIM], ka_ref[:, kv_cols(h)], _NT,
                               preferred_element_type=F32)

    raw = qk(0)
    for h in range(n_heads):
        nxt = qk(h + 1) if h + 1 < n_heads else None
        logits = raw * (HEAD_DIM ** -0.5) + bias_s[...]
        o_ref[:, h * HEAD_DIM:(h + 1) * HEAD_DIM] = _softmax_pv(logits, va_ref[:, kv_cols(h)]).astype(o_ref.dtype)
        raw = nxt


def _dsa_prompt(hq, hk_bf, hv_bf, hqi, hkiwi, *, batch, seq, n_heads, n_idx_heads, groups):
    tq = min(256, seq)
    assert seq % tq == 0 and tq % LANES == 0
    nq = seq // tq
    aw = n_heads * HEAD_DIM
    kvw = aw // groups
    n_sel = min(IDX_TOPK, seq // 4)
    kern = functools.partial(_dsa_prompt_kernel, tq=tq, seq=seq, n_sel=n_sel, n_heads=n_heads,
                             n_idx_heads=n_idx_heads, groups=groups)
    return pl.pallas_call(
        kern,
        out_shape=jax.ShapeDtypeStruct((batch * seq, aw), BF16),
        grid=(batch, nq),
        in_specs=[pl.BlockSpec((tq, aw), lambda b, i: (b * nq + i, 0)),
                  pl.BlockSpec((seq, kvw), lambda b, i: (b, 0)),
                  pl.BlockSpec((seq, kvw), lambda b, i: (b, 0)),
                  pl.BlockSpec((tq, n_idx_heads * IDX_DIM), lambda b, i: (b * nq + i, 0)),
                  pl.BlockSpec((seq, LANES), lambda b, i: (b, 0)),
                  pl.BlockSpec((tq, LANES), lambda b, i: (b * nq + i, 0))],
        out_specs=pl.BlockSpec((tq, aw), lambda b, i: (b * nq + i, 0)),
        scratch_shapes=[pltpu.VMEM((seq, IDX_DIM), BF16), pltpu.VMEM((nq, tq, tq), jnp.int32),
                        pltpu.VMEM((tq, seq), F32)],
        compiler_params=_params(("parallel", "arbitrary")),
        name="dsa_prompt",
    )(hq, hk_bf, hv_bf, hqi, hkiwi, hkiwi)


def _moba_prompt_kernel(qb_ref, kb_ref, kbf_ref, vb_ref, o_ref, km_s, *, nb, n_sel, n_heads, groups):
    i = pl.program_id(1)
    blk = MOBA_BLOCK

    @pl.when(i == 0)
    def _():
        km_s[...] = jnp.zeros_like(km_s)
        for j in range(nb):
            km_s[j:j + 1, :] = jnp.mean(kbf_ref[j * blk:(j + 1) * blk, :], axis=0, keepdims=True)

    past = lax.broadcasted_iota(jnp.int32, (blk, LANES), 1) < i
    tril = (lax.broadcasted_iota(jnp.int32, (blk, blk), 1)
            <= lax.broadcasted_iota(jnp.int32, (blk, blk), 0)).astype(F32)
    def kv_cols(h):
        return slice((h // groups) * HEAD_DIM, (h // groups + 1) * HEAD_DIM)

    def qk(h):
        q = qb_ref[:, h * HEAD_DIM:(h + 1) * HEAD_DIM]
        gate = lax.dot_general(q, km_s[:, kv_cols(h)].astype(BF16), _NT, preferred_element_type=F32)
        return gate, lax.dot_general(q, kb_ref[:, kv_cols(h)], _NT, preferred_element_type=F32)

    cur = qk(0)
    for h in range(n_heads):
        nxt = qk(h + 1) if h + 1 < n_heads else None
        gate, raw = cur
        selw = _topk_lanes(gate, past, n_sel).astype(F32)
        pieces = []
        for j in range(nb):
            allow = jnp.where(j == i, tril, selw[:, j:j + 1])
            pieces.append(jnp.where(allow > 0.0, raw[:, j * blk:(j + 1) * blk] * (HEAD_DIM ** -0.5), NEG_INF))
        o = _softmax_pv(jnp.concatenate(pieces, axis=1), vb_ref[:, kv_cols(h)])
        o_ref[:, h * HEAD_DIM:(h + 1) * HEAD_DIM] = o.astype(o_ref.dtype)
        cur = nxt


def _moba_prompt(hq, hk_bf, hk_f32, hv_bf, *, batch, seq, n_heads, groups):
    blk = MOBA_BLOCK
    assert seq % blk == 0
    nb = seq // blk
    assert 1 <= nb <= LANES
    aw = n_heads * HEAD_DIM
    kvw = aw // groups
    kern = functools.partial(_moba_prompt_kernel, nb=nb, n_sel=min(MOBA_TOPK, nb), n_heads=n_heads,
                             groups=groups)
    return pl.pallas_call(
        kern,
        out_shape=jax.ShapeDtypeStruct((batch * seq, aw), BF16),
        grid=(batch, nb),
        in_specs=[pl.BlockSpec((blk, aw), lambda b, i: (b * nb + i, 0)),
                  pl.BlockSpec((seq, kvw), lambda b, i: (b, 0)),
                  pl.BlockSpec((seq, kvw), lambda b, i: (b, 0)),
                  pl.BlockSpec((seq, kvw), lambda b, i: (b, 0))],
        out_specs=pl.BlockSpec((blk, aw), lambda b, i: (b * nb + i, 0)),
        scratch_shapes=[pltpu.VMEM((LANES, kvw), F32)],
        compiler_params=_params(("parallel", "arbitrary")),
        name="moba_prompt",
    )(hq, hk_bf, hk_f32, hv_bf)


def _gmlp_kernel(cu_ref, cv_ref, g_ref, b_ref, ws_ref, bs_ref, o_ref, *vn_out, n_groups):
    v = cv_ref[...]
    mu = jnp.mean(v, axis=-1, keepdims=True)
    vc = v - mu
    vn = vc * lax.rsqrt(jnp.mean(vc * vc, axis=-1, keepdims=True) + EPS) * g_ref[...] + b_ref[...]
    if vn_out:
        vn_out[0][...] = vn
    for g in range(n_groups):
        cols = slice(g * C_GROUP_DIM, (g + 1) * C_GROUP_DIM)
        z = jnp.dot(ws_ref[g], vn[:, cols].astype(BF16), preferred_element_type=F32) + bs_ref[:, g:g + 1]
        o_ref[:, cols] = (cu_ref[:, cols] * z).astype(o_ref.dtype)


def _gmlp(hc, ln_g, ln_b, ws, bs, *, want_vn):
    m = hc.shape[0]
    n_groups, r = ws.shape[0], ws.shape[1]
    cw = n_groups * C_GROUP_DIM
    out_shape = [jax.ShapeDtypeStruct((m, cw), BF16)]
    out_specs = [pl.BlockSpec((r, cw), lambda i: (i, 0))]
    if want_vn:
        out_shape.append(jax.ShapeDtypeStruct((m, cw), F32))
        out_specs.append(pl.BlockSpec((r, cw), lambda i: (i, 0)))
    return pl.pallas_call(
        functools.partial(_gmlp_kernel, n_groups=n_groups),
        out_shape=out_shape,
        grid=(m // r,),
        in_specs=[pl.BlockSpec((r, cw), lambda i: (i, 0)),
                  pl.BlockSpec((r, cw), lambda i: (i, 1)),
                  pl.BlockSpec((1, cw), lambda i: (0, 0)),
                  pl.BlockSpec((1, cw), lambda i: (0, 0)),
                  pl.BlockSpec((n_groups, r, r), lambda i: (0, 0, 0)),
                  pl.BlockSpec((r, n_groups), lambda i: (0, 0))],
        out_specs=out_specs,
        compiler_params=_params(("parallel",)),
        name="gmlp",
    )(hc, hc, ln_g.reshape(1, cw), ln_b.reshape(1, cw), ws, bs)


def _merge_kernel(oa_ref, ob_ref, oc_ref, wb_ref, g0_ref, g1_ref, g2_ref, o_ref, wbf_s):
    @pl.when(pl.program_id(1) == 0)
    def _():
        wbf_s[...] = wb_ref[...].astype(BF16)

    acc = None
    for o_n, g_n, n in ((oa_ref, g0_ref, 0), (ob_ref, g1_ref, 1), (oc_ref, g2_ref, 2)):
        br = jnp.dot(o_n[...], wbf_s[n], preferred_element_type=F32)
        term = (1.0 / (1.0 + jnp.exp(-g_n[...]))) * br
        acc = term if acc is None else acc + term
    o_ref[...] = acc.astype(o_ref.dtype)


def _merge(o_a, o_b, o_c, wb_stack, layer, hc, *, d_model, gate_col0):
    m, aw = o_a.shape
    tm, tn = min(1024, m), 512
    nj = d_model // tn
    g0 = gate_col0 // tn
    gate_specs = [pl.BlockSpec((tm, tn), functools.partial(lambda j, i, n: (i, g0 + n * nj + j), n=n))
                  for n in range(N_BRANCH)]
    return pl.pallas_call(
        _merge_kernel,
        out_shape=jax.ShapeDtypeStruct((m, d_model), BF16),
        grid=(nj, m // tm),
        in_specs=[pl.BlockSpec((tm, aw), lambda j, i: (i, 0))] * 3
        + [pl.BlockSpec((None, N_BRANCH, aw, tn), lambda j, i: (layer, 0, 0, j))] + gate_specs,
        out_specs=pl.BlockSpec((tm, tn), lambda j, i: (i, j)),
        scratch_shapes=[pltpu.VMEM((N_BRANCH, aw, tn), BF16)],
        compiler_params=_params(("parallel", "arbitrary")),
        name="merge",
    )(o_a, o_b, o_c, wb_stack, hc, hc, hc)


def _page_specs(block, layer, n_pages, per_step):
    def page_map(b, s, pt, slot):
        return (layer, pt[b * n_pages + s * per_step + slot], 0, 0)

    return [pl.BlockSpec(block, functools.partial(page_map, slot=slot)) for slot in range(per_step)]


def _idx_scores(qi, wi, k, n_new, n_idx_heads, k_is_t=False):
    if k_is_t:
        d = jnp.dot(qi, k, preferred_element_type=F32) * (IDX_DIM ** -0.5)
    else:
        d = lax.dot_general(qi, k, _NT, preferred_element_type=F32) * (IDX_DIM ** -0.5)
    dw = jnp.maximum(d, 0.0) * wi
    acc = dw[0:8]
    for r in range(8, n_idx_heads * n_new, 8):
        acc = acc + dw[r:r + 8]
    s = acc[0:n_new]
    for r in range(n_new, 8, n_new):
        s = s + acc[r:r + n_new]
    return s * (n_idx_heads ** -0.5)


def _dsa_sample_score_kernel(pt_ref, *refs, per_step, n_new, n_idx_heads):
    pages = refs[:per_step]
    qi_ref, wi_ref, o_ref = refs[per_step:]
    for g in range(per_step):
        o_ref[:, g * LANES:(g + 1) * LANES] = _idx_scores(qi_ref[...], wi_ref[...], pages[g][...], n_new,
                                                         n_idx_heads, k_is_t=True)


def _dsa_sample_scores(pt_flat, cache_idx_t, layer, qi_rows, wi_rows, *, dec_batch, n_pages, n_new, n_idx_heads):
    page = cache_idx_t.shape[3]
    assert page == LANES and 8 % n_new == 0 and (n_idx_heads * n_new) % 8 == 0
    per_step = min(PAGES_PER_STEP, n_pages)
    assert n_pages % per_step == 0
    rows = n_idx_heads * n_new
    kern = functools.partial(_dsa_sample_score_kernel, per_step=per_step, n_new=n_new, n_idx_heads=n_idx_heads)
    grid_spec = pltpu.PrefetchScalarGridSpec(
        num_scalar_prefetch=1,
        grid=(dec_batch, n_pages // per_step),
        in_specs=_page_specs((None, None, IDX_DIM, page), layer, n_pages, per_step)
        + [pl.BlockSpec((None, rows, IDX_DIM), lambda b, s, pt: (b, 0, 0)),
           pl.BlockSpec((None, rows, 1), lambda b, s, pt: (b, 0, 0))],
        out_specs=pl.BlockSpec((None, n_new, per_step * LANES), lambda b, s, pt: (b, 0, s)),
    )
    return pl.pallas_call(
        kern,
        out_shape=jax.ShapeDtypeStruct((dec_batch, n_new, n_pages * LANES), F32),
        grid_spec=grid_spec,
        compiler_params=_params(("parallel", "arbitrary")),
        name="dsa_sample_scores",
    )(pt_flat, *([cache_idx_t] * per_step), qi_rows, wi_rows)


def _select_kernel(s_ref, qi_ref, wi_ref, kn_ref, o_ref, on_ref, key_s, keyn_s, *, n_sel, n_new, n_idx_heads):
    t = lax.broadcasted_iota(jnp.int32, (n_new, LANES), 0)
    c = lax.broadcasted_iota(jnp.int32, (n_new, LANES), 1)
    new_ok = c <= t
    s_new = _idx_scores(qi_ref[...], wi_ref[...], kn_ref[...], n_new, n_idx_heads)
    key_s[...] = _ordered_key(s_ref[...])
    keyn_s[...] = _ordered_key(jnp.where(new_ok, s_new, NEG_INF))

    def count_ge(cand):
        return (jnp.sum((key_s[...] >= cand).astype(F32), axis=1, keepdims=True)
                + jnp.sum((keyn_s[...] >= cand).astype(F32), axis=1, keepdims=True))

    thr = _kth_largest_key(count_ge, n_new, n_sel)
    o_ref[...] = (key_s[...] >= thr).astype(F32)
    on_ref[...] = ((keyn_s[...] >= thr) & new_ok).astype(F32)


def _dsa_sample_select(scores, qi_rows, wi_rows, ki_new_pad, n_sel, *, n_idx_heads):
    db, n_new, length = scores.shape
    rows = n_idx_heads * n_new
    return pl.pallas_call(
        functools.partial(_select_kernel, n_sel=n_sel, n_new=n_new, n_idx_heads=n_idx_heads),
        out_shape=[jax.ShapeDtypeStruct(scores.shape, F32), jax.ShapeDtypeStruct((db, n_new, LANES), F32)],
        grid=(db,),
        in_specs=[pl.BlockSpec((None, n_new, length), lambda b: (b, 0, 0)),
                  pl.BlockSpec((None, rows, IDX_DIM), lambda b: (b, 0, 0)),
                  pl.BlockSpec((None, rows, 1), lambda b: (b, 0, 0)),
                  pl.BlockSpec((None, LANES, IDX_DIM), lambda b: (b, 0, 0))],
        out_specs=[pl.BlockSpec((None, n_new, length), lambda b: (b, 0, 0)),
                   pl.BlockSpec((None, n_new, LANES), lambda b: (b, 0, 0))],
        scratch_shapes=[pltpu.VMEM((n_new, length), jnp.int32), pltpu.VMEM((n_new, LANES), jnp.int32)],
        compiler_params=_params(("parallel",)),
        name="dsa_sample_select",
    )(scores, qi_rows, wi_rows, ki_new_pad)


def _head_rows(page_refs, kv, n_kv):
    parts = [r[pl.ds(kv, LANES, stride=n_kv), :] for r in page_refs]
    return parts[0] if len(parts) == 1 else jnp.concatenate(parts, axis=0)


def _dsa_sample_attn_kernel(pt_ref, *refs, per_step, n_steps, n_new, n_kv, groups):
    k_pages, v_pages = refs[:per_step], refs[per_step:2 * per_step]
    kn_ref, vn_ref, q_ref, sel_ref, seln_ref, o_ref, m_s, l_s, acc_s = refs[2 * per_step:]
    step = pl.program_id(1)
    rpk = groups * n_new
    scale = HEAD_DIM ** -0.5

    @pl.when(step == 0)
    def _():
        m_s[...] = jnp.full_like(m_s, NEG_INF)
        l_s[...] = jnp.zeros_like(l_s)
        acc_s[...] = jnp.zeros_like(acc_s)

    def update(get_k, get_v, sel):
        mask = jnp.concatenate([sel] * groups, axis=0) > 0.0
        rows = [slice(kv * rpk, (kv + 1) * rpk) for kv in range(n_kv)]
        logits = [lax.dot_general(q_ref[rows[kv], :], get_k(kv), _NT, preferred_element_type=F32)
                  for kv in range(n_kv)]
        probs, alphas = [], []
        for kv in range(n_kv):
            s = jnp.where(mask, logits[kv] * scale, NEG_INF)
            m_old = m_s[rows[kv], :]
            m_new = jnp.maximum(m_old, jnp.max(s, axis=1, keepdims=True))
            m_safe = jnp.where(m_new == NEG_INF, 0.0, m_new)
            alpha = jnp.exp(m_old - m_safe)
            pr = jnp.exp(s - m_safe)
            m_s[rows[kv], :] = m_new
            l_s[rows[kv], :] = alpha * l_s[rows[kv], :] + jnp.sum(pr, axis=1, keepdims=True)
            probs.append(pr)
            alphas.append(alpha)
        for kv in range(n_kv):
            acc_s[rows[kv], :] = alphas[kv] * acc_s[rows[kv], :] + jnp.dot(
                probs[kv], get_v(kv), preferred_element_type=F32)

    update(lambda kv: _head_rows(k_pages, kv, n_kv), lambda kv: _head_rows(v_pages, kv, n_kv), sel_ref[...])

    @pl.when(step == n_steps - 1)
    def _():
        update(lambda kv: kn_ref[kv], lambda kv: vn_ref[kv], seln_ref[...])
        o_ref[...] = acc_s[...] / l_s[...]


def _dsa_sample_attn(pt_flat, cache_k, cache_v, layer, k_new_pad, v_new_pad, q_rows, sel, sel_new, *, dec_batch,
                     n_pages, n_new, n_kv, groups):
    rows = n_kv * groups * n_new
    prow = cache_k.shape[2]
    per_step = min(PAGES_PER_STEP, n_pages)
    assert n_pages % per_step == 0
    n_steps = n_pages // per_step
    kern = functools.partial(_dsa_sample_attn_kernel, per_step=per_step, n_steps=n_steps, n_new=n_new, n_kv=n_kv,
                             groups=groups)
    page_specs = _page_specs((None, None, prow, HEAD_DIM), layer, n_pages, per_step)
    grid_spec = pltpu.PrefetchScalarGridSpec(
        num_scalar_prefetch=1,
        grid=(dec_batch, n_steps),
        in_specs=page_specs + page_specs
        + [pl.BlockSpec((None, n_kv, LANES, HEAD_DIM), lambda b, s, pt: (b, 0, 0, 0)),
           pl.BlockSpec((None, n_kv, LANES, HEAD_DIM), lambda b, s, pt: (b, 0, 0, 0)),
           pl.BlockSpec((None, rows, HEAD_DIM), lambda b, s, pt: (b, 0, 0)),
           pl.BlockSpec((None, n_new, per_step * LANES), lambda b, s, pt: (b, 0, s)),
           pl.BlockSpec((None, n_new, LANES), lambda b, s, pt: (b, 0, 0))],
        out_specs=pl.BlockSpec((None, rows, HEAD_DIM), lambda b, s, pt: (b, 0, 0)),
        scratch_shapes=[pltpu.VMEM((rows, 1), F32), pltpu.VMEM((rows, 1), F32),
                        pltpu.VMEM((rows, HEAD_DIM), F32)],
    )
    return pl.pallas_call(
        kern,
        out_shape=jax.ShapeDtypeStruct((dec_batch, rows, HEAD_DIM), F32),
        grid_spec=grid_spec,
        compiler_params=_params(("parallel", "arbitrary")),
        name="dsa_sample_attn",
    )(pt_flat, *([cache_k] * per_step), *([cache_v] * per_step), k_new_pad, v_new_pad, q_rows, sel, sel_new)


def _moba_sample_kernel(pt_ref, *refs, bps, n_steps, nb, n_sel, n_new, n_kv, groups, past_len):
    k_pages, v_pages = refs[:2 * bps], refs[2 * bps:4 * bps]
    kn_ref, vn_ref, q_ref, o_ref, gate_s, m_s, l_s, acc_s = refs[4 * bps:]
    step = pl.program_id(1)
    rpk = groups * n_new
    rows_all = n_kv * rpk
    scale = HEAD_DIM ** -0.5
    lane = lax.broadcasted_iota(jnp.int32, (rows_all, LANES), 1)

    @pl.when(step == 0)
    def _():
        gate_s[...] = jnp.zeros_like(gate_s)
        m_s[...] = jnp.zeros_like(m_s)
        l_s[...] = jnp.zeros_like(l_s)

    gate, m_blk, l_blk = gate_s[...], m_s[...], l_s[...]
    blk = MOBA_BLOCK
    kv_rows = [slice(kv * rpk, (kv + 1) * rpk) for kv in range(n_kv)]
    ks = [_head_rows(k_pages, kv, n_kv) for kv in range(n_kv)]
    logits = [lax.dot_general(q_ref[kv_rows[kv], :], ks[kv], _NT, preferred_element_type=F32)
              for kv in range(n_kv)]
    probs = {}
    for bb in range(bps):
        g_cols, m_cols, l_cols = [], [], []
        for kv in range(n_kv):
            kmean = jnp.mean(ks[kv][bb * blk:(bb + 1) * blk, :], axis=0, keepdims=True)
            g_cols.append(jnp.sum(q_ref[kv_rows[kv], :] * kmean, axis=1, keepdims=True))
            s = logits[kv][:, bb * blk:(bb + 1) * blk] * scale
            m = jnp.max(s, axis=1, keepdims=True)
            pr = jnp.exp(s - m)
            m_cols.append(m)
            l_cols.append(jnp.sum(pr, axis=1, keepdims=True))
            probs[bb, kv] = pr
        here = lane == step * bps + bb
        gate = jnp.where(here, jnp.concatenate(g_cols, axis=0), gate)
        m_blk = jnp.where(here, jnp.concatenate(m_cols, axis=0), m_blk)
        l_blk = jnp.where(here, jnp.concatenate(l_cols, axis=0), l_blk)
    for bb in range(bps):
        for kv in range(n_kv):
            v = _head_rows(v_pages[2 * bb:2 * bb + 2], kv, n_kv)
            acc_s[step * bps + bb, kv_rows[kv], :] = jnp.dot(probs[bb, kv], v, preferred_element_type=F32)
    gate_s[...] = gate
    m_s[...] = m_blk
    l_s[...] = l_blk

    @pl.when(step == n_steps - 1)
    def _():
        t_row = lax.broadcasted_iota(jnp.int32, (rpk, LANES), 0) & (n_new - 1)
        c_col = lax.broadcasted_iota(jnp.int32, (rpk, LANES), 1)
        own_ok = c_col <= t_row
        t_all = lax.broadcasted_iota(jnp.int32, (rows_all, LANES), 0) & (n_new - 1)
        own_blk = jnp.right_shift(past_len + t_all, MOBA_BLOCK.bit_length() - 1)
        valid = (lane < own_blk) & (lane < nb)
        sel = _topk_lanes(gate, valid, n_sel)
        for kv in range(n_kv):
            rows = slice(kv * rpk, (kv + 1) * rpk)
            q = q_ref[rows, :]
            s = lax.dot_general(q, kn_ref[kv], _NT, preferred_element_type=F32) * scale
            s = jnp.where(own_ok, s, NEG_INF)
            m_o = jnp.max(s, axis=1, keepdims=True)
            m_sel = jnp.where(sel[rows, :], m_blk[rows, :], NEG_INF)
            m_all = jnp.maximum(m_o, jnp.max(m_sel, axis=1, keepdims=True))
            pr = jnp.exp(s - m_all)
            w = jnp.where(sel[rows, :], jnp.exp(m_sel - m_all), 0.0)
            den = jnp.sum(pr, axis=1, keepdims=True) + jnp.sum(w * l_blk[rows, :], axis=1, keepdims=True)
            num = jnp.dot(pr, vn_ref[kv], preferred_element_type=F32)
            for jb in range(nb):
                num = num + w[:, jb:jb + 1] * acc_s[jb, rows, :]
            o_ref[rows, :] = num / den


def _moba_sample(pt_flat, cache_k, cache_v, layer, k_new_pad, v_new_pad, q_rows, *, dec_batch, n_pages, n_new,
                 n_kv, groups, past_len):
    assert past_len % MOBA_BLOCK == 0 and n_new < MOBA_BLOCK and MOBA_BLOCK == 2 * LANES
    assert n_new & (n_new - 1) == 0
    nb = (past_len + n_new) // MOBA_BLOCK
    assert 1 <= nb <= LANES and n_pages == 2 * nb
    bps = min(PAGES_PER_STEP // 2, nb)
    assert nb % bps == 0
    n_steps = nb // bps
    rows = n_kv * groups * n_new
    prow = cache_k.shape[2]
    kern = functools.partial(_moba_sample_kernel, bps=bps, n_steps=n_steps, nb=nb, n_sel=min(MOBA_TOPK, nb),
                             n_new=n_new, n_kv=n_kv, groups=groups, past_len=past_len)
    page_specs = _page_specs((None, None, prow, HEAD_DIM), layer, n_pages, 2 * bps)
    grid_spec = pltpu.PrefetchScalarGridSpec(
        num_scalar_prefetch=1,
        grid=(dec_batch, n_steps),
        in_specs=page_specs + page_specs
        + [pl.BlockSpec((None, n_kv, LANES, HEAD_DIM), lambda b, s, pt: (b, 0, 0, 0)),
           pl.BlockSpec((None, n_kv, LANES, HEAD_DIM), lambda b, s, pt: (b, 0, 0, 0)),
           pl.BlockSpec((None, rows, HEAD_DIM), lambda b, s, pt: (b, 0, 0))],
        out_specs=pl.BlockSpec((None, rows, HEAD_DIM), lambda b, s, pt: (b, 0, 0)),
        scratch_shapes=[pltpu.VMEM((rows, LANES), F32), pltpu.VMEM((rows, LANES), F32),
                        pltpu.VMEM((rows, LANES), F32), pltpu.VMEM((nb, rows, HEAD_DIM), F32)],
    )
    return pl.pallas_call(
        kern,
        out_shape=jax.ShapeDtypeStruct((dec_batch, rows, HEAD_DIM), F32),
        grid_spec=grid_spec,
        compiler_params=_params(("parallel", "arbitrary")),
        name="moba_sample",
    )(pt_flat, *([cache_k] * (2 * bps)), *([cache_v] * (2 * bps)), k_new_pad, v_new_pad, q_rows)


def _rope_tables(pos, rot, width, n_rope_lanes=LANES):
    half = rot // 2
    inv = jnp.power(ROPE_THETA, -jnp.arange(half, dtype=F32) / half)
    ang = pos.astype(F32)[:, None] * inv[None, :]
    cos, sin = jnp.cos(ang), jnp.sin(ang)
    t = pos.shape[0]
    zeros = jnp.zeros((t, half), F32)
    rest0 = jnp.zeros((t, width - rot), F32)
    c = jnp.concatenate([cos, cos, jnp.ones((t, width - rot), F32)], axis=1)
    sa = jnp.concatenate([-sin, zeros, rest0], axis=1)
    sb = jnp.concatenate([zeros, sin, rest0], axis=1)
    reps = n_rope_lanes // width
    pad = LANES - n_rope_lanes
    c = jnp.concatenate([jnp.tile(c, (1, reps)), jnp.ones((t, pad), F32)], axis=1)
    sa = jnp.concatenate([jnp.tile(sa, (1, reps)), jnp.zeros((t, pad), F32)], axis=1)
    sb = jnp.concatenate([jnp.tile(sb, (1, reps)), jnp.zeros((t, pad), F32)], axis=1)
    return c, sa, sb


def _kv_major_rows(a, db, n_new, n_kv, groups):
    a = a.reshape(db, n_new, n_kv, groups, HEAD_DIM)
    return jnp.transpose(a, (0, 2, 3, 1, 4)).reshape(db, n_kv * groups * n_new, HEAD_DIM)


def _from_kv_major_rows(o, db, n_new, n_kv, groups):
    o = o.reshape(db, n_kv, groups, n_new, HEAD_DIM)
    return jnp.transpose(o, (0, 3, 1, 2, 4)).reshape(db * n_new, n_kv * groups * HEAD_DIM)


def _new_kv_pad(k, db, n_new, n_kv):
    k = jnp.transpose(k.reshape(db, n_new, n_kv, HEAD_DIM), (0, 2, 1, 3))
    return jnp.pad(k, ((0, 0), (0, 0), (0, LANES - n_new), (0, 0)))


def kernel(x_prompt, x_sample, cache_a_k, cache_a_v, cache_a_idx, cache_b_k, cache_b_v, page_table, w_in, norm1_g,
           c_ln_g, c_ln_b, w_spatial, b_spatial, w_branch, w_out, norm2_g, w_ff1, w_ff2, norm_f_g):
    batch, seq, d_model = x_prompt.shape
    db, n_new, _ = x_sample.shape
    depth, n_phys, page, n_kv, _ = cache_a_k.shape
    n_pages = page_table.shape[1]
    past_len = n_pages * page
    n_heads = d_model // 256
    groups = n_heads // n_kv
    aw = n_heads * HEAD_DIM
    kvw = n_kv * HEAD_DIM
    n_idx_heads = d_model // 128
    iw = n_idx_heads * IDX_DIM
    n_cg = w_spatial.shape[1]
    cw = n_cg * C_GROUP_DIM
    rot, idx_rot = HEAD_DIM // 4, IDX_DIM // 4
    assert page == LANES and aw == 2 * kvw and cw % 512 == 0 and seq % CHUNK == 0 and n_new <= CHUNK

    sizes = (aw, kvw, kvw, iw, IDX_DIM, n_idx_heads, aw, kvw, kvw, cw, cw, N_BRANCH * d_model)
    offs = [0]
    for s in sizes:
        offs.append(offs[-1] + s)
    (o_qa, o_ka, o_va, o_qi, o_ki, o_wi, o_qb, o_kb, o_vb, o_cu, o_cv, o_gl, _) = offs

    pos_p = jnp.arange(seq, dtype=jnp.int32)
    pos_s = jnp.tile(past_len + jnp.arange(n_new, dtype=jnp.int32), db)
    tabs = {
        "p": (_rope_tables(pos_p, rot, HEAD_DIM), _rope_tables(pos_p, idx_rot, IDX_DIM),
              _rope_tables(pos_p, idx_rot, IDX_DIM, IDX_DIM)),
        "s": (_rope_tables(pos_s, rot, HEAD_DIM), _rope_tables(pos_s, idx_rot, IDX_DIM),
              _rope_tables(pos_s, idx_rot, IDX_DIM, IDX_DIM)),
    }

    cache_ak = cache_a_k.reshape(depth, n_phys, page * n_kv, HEAD_DIM)
    cache_av = cache_a_v.reshape(depth, n_phys, page * n_kv, HEAD_DIM)
    cache_bk = cache_b_k.reshape(depth, n_phys, page * n_kv, HEAD_DIM)
    cache_bv = cache_b_v.reshape(depth, n_phys, page * n_kv, HEAD_DIM)
    pt_flat = page_table.reshape(-1).astype(jnp.int32)

    xp = x_prompt.reshape(batch * seq, d_model)
    xs = x_sample.reshape(db * n_new, d_model)
    rs = db * n_new
    tri = jnp.tril(jnp.ones((CHUNK, CHUNK), dtype=bool))
    sel_s = min(IDX_TOPK, (past_len + n_new) // 4)

    outs = {k: [] for k in ("pa_k", "pa_v", "pa_i", "pb_k", "pb_v", "sa_k", "sa_v", "sa_i", "sb_k", "sb_v", "sc_v")}

    w_in_t = jnp.swapaxes(w_in, 1, 2)
    cache_idx_t = jnp.swapaxes(cache_a_idx, 2, 3)

    def project(x, l, which):
        t128, t64, tkiwi = tabs[which]
        xn = _rms_norm(x, norm1_g[l], BF16)

        def proj(n_off, n, out_dtypes, epi="none", extra=(), half=0):
            return _matmul_ws(xn, w_in_t, l, out_dtypes, epi, extra, half, n_off=n_off, n=n, w_rows_are_n=True)

        (hqa,) = proj(o_qa, aw, (BF16,), "rope", t128, rot // 2)
        (hqb,) = proj(o_qb, aw, (BF16,), "rope", t128, rot // 2)
        hka, hka_bf = proj(o_ka, kvw, (F32, BF16), "rope", t128, rot // 2)
        hkb, hkb_bf = proj(o_kb, kvw, (F32, BF16), "rope", t128, rot // 2)
        hva, hva_bf = proj(o_va, kvw, (F32, BF16))
        hvb, hvb_bf = proj(o_vb, kvw, (F32, BF16))
        (hqi,) = proj(o_qi, iw, (BF16,), "rope", t64, idx_rot // 2)
        (hkiwi,) = proj(o_ki, LANES, (F32,), "rope", tkiwi, idx_rot // 2)
        (hc,) = proj(o_cu, 2 * cw + N_BRANCH * d_model, (F32,))
        return hqa, hqb, hka, hka_bf, hkb, hkb_bf, hva, hva_bf, hvb, hvb_bf, hqi, hkiwi, hc

    def merge_ffn(x, o_a, o_b, o_c, hc, l):
        merged = _merge(o_a, o_b, o_c, w_branch, l, hc, d_model=d_model, gate_col0=2 * cw)
        (x1,) = _matmul_ws(merged, w_out, l, (F32,), "residual", (x,))
        hn = _rms_norm(x1, norm2_g[l], BF16)
        (h1,) = _matmul_ws(hn, w_ff1, l, (BF16,), "relu2")
        (x2,) = _matmul_kt(h1, w_ff2, l, (F32,), "residual", (x1,))
        return x2

    for l in range(depth):
        hqa, hqb, hka, hka_bf, hkb, hkb_bf, hva, hva_bf, hvb, hvb_bf, hqi, hkiwi, hc = project(xp, l, "p")
        o_a = _dsa_prompt(hqa, hka_bf, hva_bf, hqi, hkiwi, batch=batch, seq=seq, n_heads=n_heads,
                          n_idx_heads=n_idx_heads, groups=groups)
        o_b = _moba_prompt(hqb, hkb_bf, hkb, hvb_bf, batch=batch, seq=seq, n_heads=n_heads, groups=groups)
        ws_p = jnp.where(tri[None], w_spatial[l], 0).astype(BF16)
        bs_p = b_spatial[l].T
        (o_c,) = _gmlp(hc, c_ln_g[l], c_ln_b[l], ws_p, bs_p, want_vn=False)
        xp = merge_ffn(xp, o_a, o_b, o_c, hc, l)
        outs["pa_k"].append(hka.reshape(batch, seq, n_kv, HEAD_DIM))
        outs["pa_v"].append(hva.reshape(batch, seq, n_kv, HEAD_DIM))
        outs["pa_i"].append(hkiwi[:, :IDX_DIM].reshape(batch, seq, IDX_DIM))
        outs["pb_k"].append(hkb.reshape(batch, seq, n_kv, HEAD_DIM))
        outs["pb_v"].append(hvb.reshape(batch, seq, n_kv, HEAD_DIM))

        hqa, hqb, hka, hka_bf, hkb, hkb_bf, hva, hva_bf, hvb, hvb_bf, hqi, hkiwi, hc = project(xs, l, "s")
        qi_rows = jnp.transpose(hqi.astype(F32).reshape(db, n_new, n_idx_heads, IDX_DIM), (0, 2, 1, 3)).reshape(
            db, n_idx_heads * n_new, IDX_DIM)
        wi_rows = jnp.transpose(hkiwi[:, IDX_DIM:IDX_DIM + n_idx_heads].reshape(db, n_new, n_idx_heads),
                                (0, 2, 1)).reshape(db, n_idx_heads * n_new, 1)
        ki_new = jnp.pad(hkiwi[:, :IDX_DIM].reshape(db, n_new, IDX_DIM), ((0, 0), (0, LANES - n_new), (0, 0)))
        scores = _dsa_sample_scores(pt_flat, cache_idx_t, l, qi_rows, wi_rows, dec_batch=db, n_pages=n_pages,
                                    n_new=n_new, n_idx_heads=n_idx_heads)
        sel, sel_new = _dsa_sample_select(scores, qi_rows, wi_rows, ki_new, sel_s, n_idx_heads=n_idx_heads)
        qa_rows = _kv_major_rows(hqa.astype(F32), db, n_new, n_kv, groups)
        o_a = _dsa_sample_attn(pt_flat, cache_ak, cache_av, l, _new_kv_pad(hka, db, n_new, n_kv),
                               _new_kv_pad(hva, db, n_new, n_kv), qa_rows, sel, sel_new, dec_batch=db,
                               n_pages=n_pages, n_new=n_new, n_kv=n_kv, groups=groups)
        o_a = _from_kv_major_rows(o_a, db, n_new, n_kv, groups).astype(BF16)
        qb_rows = _kv_major_rows(hqb.astype(F32), db, n_new, n_kv, groups)
        o_b = _moba_sample(pt_flat, cache_bk, cache_bv, l, _new_kv_pad(hkb, db, n_new, n_kv),
                           _new_kv_pad(hvb, db, n_new, n_kv), qb_rows, dec_batch=db, n_pages=n_pages,
                           n_new=n_new, n_kv=n_kv, groups=groups, past_len=past_len)
        o_b = _from_kv_major_rows(o_b, db, n_new, n_kv, groups).astype(BF16)
        ws_c = jnp.where(tri[None, :n_new, :n_new], w_spatial[l][:, :n_new, :n_new], 0)
        ws_s = jnp.einsum("ab,gij->gaibj", jnp.eye(db, dtype=F32), ws_c).reshape(n_cg, rs, rs).astype(BF16)
        bs_s = jnp.tile(b_spatial[l][:, :n_new].T, (db, 1))
        o_c, vn = _gmlp(hc, c_ln_g[l], c_ln_b[l], ws_s, bs_s, want_vn=True)
        xs = merge_ffn(xs, o_a, o_b, o_c, hc, l)
        outs["sa_k"].append(hka.reshape(db, n_new, n_kv, HEAD_DIM))
        outs["sa_v"].append(hva.reshape(db, n_new, n_kv, HEAD_DIM))
        outs["sa_i"].append(hkiwi[:, :IDX_DIM].reshape(db, n_new, IDX_DIM))
        outs["sb_k"].append(hkb.reshape(db, n_new, n_kv, HEAD_DIM))
        outs["sb_v"].append(hvb.reshape(db, n_new, n_kv, HEAD_DIM))
        outs["sc_v"].append(vn.reshape(db, n_new, cw))

    y_prompt = _rms_norm(xp, norm_f_g, F32).reshape(batch, seq, d_model)
    y_sample = _rms_norm(xs, norm_f_g, F32).reshape(db, n_new, d_model)
    st = {k: jnp.stack(v) for k, v in outs.items()}
    return (y_prompt, y_sample, st["pa_k"], st["pa_v"], st["pa_i"], st["pb_k"], st["pb_v"],
            st["sa_k"], st["sa_v"], st["sa_i"], st["sb_k"], st["sb_v"], st["sc_v"])
```

```python
import functools
import math

import jax
import jax.numpy as jnp
from jax import lax
from jax.experimental import pallas as pl
from jax.experimental.pallas import tpu as pltpu

HEAD_DIM = 128
ROPE_THETA = 500000.0
IDX_DIM = 64
IDX_TOPK = 256
MOBA_BLOCK = 256
MOBA_TOPK = 3
CHUNK = 128
C_GROUP_DIM = 128
N_BRANCH = 3
EPS = 1e-6
LANES = 128
INT32_MIN = -(2 ** 31)
PAGES_PER_STEP = 16
Q_SCALE = HEAD_DIM ** -0.5 * math.log2(math.e)

F32 = jnp.float32
BF16 = jnp.bfloat16
NEG_INF = float("-inf")
VMEM_LIMIT = 56 * 1024 * 1024

_NT = (((1,), (1,)), ((), ()))


def _params(sem):
    return pltpu.CompilerParams(dimension_semantics=sem, vmem_limit_bytes=VMEM_LIMIT)


def _rms_kernel(x_ref, g_ref, o_ref):
    x = x_ref[...]
    y = x * lax.rsqrt(jnp.mean(x * x, axis=-1, keepdims=True) + EPS)
    o_ref[...] = (y * g_ref[...]).astype(o_ref.dtype)


def _rms_norm(x, g, out_dtype):
    m, d = x.shape
    tm = min(1024, m)
    return pl.pallas_call(
        _rms_kernel,
        out_shape=jax.ShapeDtypeStruct((m, d), out_dtype),
        grid=(m // tm,),
        in_specs=[pl.BlockSpec((tm, d), lambda i: (i, 0)),
                  pl.BlockSpec((1, d), lambda i: (0, 0))],
        out_specs=pl.BlockSpec((tm, d), lambda i: (i, 0)),
        compiler_params=_params(("parallel",)),
        name="rms_norm",
    )(x, g.reshape(1, d))


def _rope_epilogue(r, c, sa, sb, half):
    outs = []
    for s in range(r.shape[1] // LANES):
        a = r[:, s * LANES:(s + 1) * LANES]
        outs.append(a * c + pltpu.roll(a, LANES - half, 1) * sa + pltpu.roll(a, half, 1) * sb)
    return outs[0] if len(outs) == 1 else jnp.concatenate(outs, axis=1)


_N_EXTRA = {"none": 0, "relu2": 0, "rope": 3, "residual": 1}


def _epilogue(r, epi, extra, half, outs, out_scale=1.0):
    if epi == "rope":
        r = _rope_epilogue(r, extra[0][...], extra[1][...], extra[2][...], half)
        if out_scale != 1.0:
            r = r * out_scale
    elif epi == "relu2":
        r = jnp.square(jnp.maximum(r, 0.0))
    elif epi == "residual":
        r = extra[0][...] + r
    for o_ref in outs:
        o_ref[...] = r.astype(o_ref.dtype)
    return r


def _mm_ws_kernel(*refs, epi, n_out, half, w_rows_are_n, out_scale, n_kv, has_buf, want_means):
    a_ref, w_ref = refs[0], refs[1]
    extra = refs[2:2 + _N_EXTRA[epi]]
    first_out = 2 + _N_EXTRA[epi] + (1 if has_buf else 0)
    outs = refs[first_out:first_out + n_out]
    wbf_s = refs[-1]

    @pl.when(pl.program_id(1) == 0)
    def _():
        wbf_s[...] = (w_ref[0] if w_rows_are_n else w_ref[...]).astype(BF16)

    if w_rows_are_n:
        part = lax.dot_general(a_ref[...], wbf_s[...], _NT, preferred_element_type=F32)
    else:
        part = jnp.dot(a_ref[...], wbf_s[...], preferred_element_type=F32)
    r = _epilogue(part, epi, extra, half, outs, out_scale)
    if n_kv:
        kv_ref = refs[first_out + n_out]
        tm = r.shape[0]
        for kv in range(n_kv):
            kv_ref[pl.ds(kv, tm, stride=n_kv), :] = r[:, kv * HEAD_DIM:(kv + 1) * HEAD_DIM]
    if want_means:
        mean_ref = refs[first_out + n_out + 1]
        for g in range(r.shape[0] // MOBA_BLOCK):
            mean_ref[g:g + 1, :] = jnp.mean(r[g * MOBA_BLOCK:(g + 1) * MOBA_BLOCK, :], axis=0, keepdims=True)


def _matmul_ws(a, w_stack, layer, out_dtypes, epi="none", extra=(), half=0, n_off=0, n=None, w_rows_are_n=False,
               out_scale=1.0, kv_out=None, want_means=False, tm=1024, tn=1024):
    m, kdim = a.shape
    n_tot = w_stack.shape[1] if w_rows_are_n else w_stack.shape[2]
    n = n_tot - n_off if n is None else n
    tm, tn = min(tm, m), min(tn, n)
    if epi == "rope":
        tm = min(tm, extra[0].shape[0])
    assert m % tm == 0 and n % tn == 0 and n_off + n <= n_tot, (a.shape, w_stack.shape, tm, tn)
    in_specs = [pl.BlockSpec((tm, kdim), lambda j, i: (i, 0))]
    if w_rows_are_n:
        assert n_off % 8 == 0 and n_off + (n // tn) * tn <= n_tot
        in_specs.append(pl.BlockSpec((pl.Element(1), pl.Element(tn), pl.Element(kdim)),
                                     lambda j, i: (layer, pl.multiple_of(n_off + j * tn, 8), 0)))
        w_scratch = pltpu.VMEM((tn, kdim), BF16)
    else:
        assert n_off % tn == 0
        in_specs.append(pl.BlockSpec((None, kdim, tn), lambda j, i: (layer, 0, n_off // tn + j)))
        w_scratch = pltpu.VMEM((kdim, tn), BF16)
    if epi == "rope":
        nt = extra[0].shape[0] // tm
        in_specs += [pl.BlockSpec((tm, LANES), lambda j, i: (i % nt, 0))] * 3
    elif epi == "residual":
        in_specs += [pl.BlockSpec((tm, tn), lambda j, i: (i, j))]
    out_shape = [jax.ShapeDtypeStruct((m, n), dt) for dt in out_dtypes]
    out_specs = [pl.BlockSpec((tm, tn), lambda j, i: (i, j)) for _ in out_dtypes]
    operands, aliases, n_kv = [a, w_stack, *extra], {}, 0
    if kv_out is not None:
        buf, n_layers, n_kv = kv_out
        assert n == tn == n_kv * HEAD_DIM
        if buf is not None:
            aliases = {len(operands): len(out_shape)}
            operands.append(buf)
            in_specs.append(pl.BlockSpec(memory_space=pl.ANY))
        out_shape.append(jax.ShapeDtypeStruct((n_layers * m * n_kv, HEAD_DIM), F32))
        out_specs.append(pl.BlockSpec((tm * n_kv, HEAD_DIM), lambda j, i: (layer * (m // tm) + i, 0)))
    if want_means:
        assert kv_out is not None and tm % MOBA_BLOCK == 0
        out_shape.append(jax.ShapeDtypeStruct((m // tm, tm // MOBA_BLOCK, n), F32))
        out_specs.append(pl.BlockSpec((None, tm // MOBA_BLOCK, n), lambda j, i: (i, 0, 0)))
    outs = pl.pallas_call(
        functools.partial(_mm_ws_kernel, epi=epi, n_out=len(out_dtypes), half=half, w_rows_are_n=w_rows_are_n,
                          out_scale=out_scale, n_kv=n_kv, has_buf=bool(aliases), want_means=want_means),
        out_shape=out_shape,
        grid=(n // tn, m // tm),
        in_specs=in_specs,
        out_specs=out_specs,
        scratch_shapes=[w_scratch],
        input_output_aliases=aliases,
        compiler_params=_params(("parallel", "arbitrary")),
        name="matmul_" + epi,
    )(*operands)
    return tuple(outs)


def _mm_kt_kernel(*refs, epi, nk, n_out, half):
    a_ref, w_ref = refs[0], refs[1]
    extra = refs[2:2 + _N_EXTRA[epi]]
    outs = refs[2 + _N_EXTRA[epi]:2 + _N_EXTRA[epi] + n_out]
    acc_ref = refs[-1]
    k = pl.program_id(2)
    part = jnp.dot(a_ref[...], w_ref[...].astype(BF16), preferred_element_type=F32)

    @pl.when(k == 0)
    def _():
        acc_ref[...] = part

    @pl.when(k > 0)
    def _():
        acc_ref[...] += part

    @pl.when(k == nk - 1)
    def _():
        _epilogue(acc_ref[...], epi, extra, half, outs)


def _matmul_kt(a, w_stack, layer, out_dtypes, epi="none", extra=(), tm=1024, tn=1024, tk=2048):
    m, kdim = a.shape
    n = w_stack.shape[2]
    tm, tn, tk = min(tm, m), min(tn, n), min(tk, kdim)
    assert m % tm == 0 and n % tn == 0 and kdim % tk == 0 and kdim > tk, (a.shape, w_stack.shape, tm, tn, tk)
    nk = kdim // tk
    in_specs = [pl.BlockSpec((tm, tk), lambda i, j, k: (i, k)),
                pl.BlockSpec((None, tk, tn), lambda i, j, k: (layer, k, j))]
    if epi == "residual":
        in_specs += [pl.BlockSpec((tm, tn), lambda i, j, k: (i, j))]
    outs = pl.pallas_call(
        functools.partial(_mm_kt_kernel, epi=epi, nk=nk, n_out=len(out_dtypes), half=0),
        out_shape=[jax.ShapeDtypeStruct((m, n), dt) for dt in out_dtypes],
        grid=(m // tm, n // tn, nk),
        in_specs=in_specs,
        out_specs=[pl.BlockSpec((tm, tn), lambda i, j, k: (i, j)) for _ in out_dtypes],
        scratch_shapes=[pltpu.VMEM((tm, tn), F32)],
        compiler_params=_params(("parallel", "parallel", "arbitrary")),
        name="matmul_kt_" + epi,
    )(a, w_stack, *extra)
    return tuple(outs)


def _ordered_key(score):
    bits = pltpu.bitcast(score, jnp.int32)
    return jnp.where(bits < 0, bits ^ jnp.int32(0x7FFFFFFF), bits)


def _kth_largest_key(count_ge, rows, n_sel):
    def body(b, t):
        cand = t + lax.shift_left(jnp.int32(1), jnp.int32(31) - b)
        return jnp.where(count_ge(cand) >= float(n_sel), cand, t)

    return lax.fori_loop(0, 32, body, jnp.full((rows, 1), INT32_MIN, jnp.int32))


def _topk_lanes(g, valid, n_sel):
    lane = lax.broadcasted_iota(jnp.int32, g.shape, 1).astype(F32)
    gm = jnp.where(valid, g, NEG_INF)
    sel = jnp.zeros(g.shape, jnp.bool_)
    for _ in range(n_sel):
        mx = jnp.max(gm, axis=1, keepdims=True)
        first = jnp.min(jnp.where(gm == mx, lane, float(LANES)), axis=1, keepdims=True)
        pick = lane == first
        sel = sel | pick
        gm = jnp.where(pick, NEG_INF, gm)
    return sel & valid


def _softmax_pv(logits2, v):
    p = jnp.exp2(logits2 - jnp.max(logits2, axis=1, keepdims=True))
    l = jnp.sum(p, axis=1, keepdims=True)
    return jnp.dot(p.astype(BF16), v, preferred_element_type=F32) / l


def _width_classes(seq, tile):
    widths = sorted({w for w in (seq // 4, seq // 2, seq) if w % tile == 0 and w >= tile})
    classes, lo = [], 0
    for w in widths:
        classes.append((lo, w // tile, w))
        lo = w // tile
    return classes


def _dsa_prompt_kernel(qa_ref, ka_ref, va_ref, qi_ref, ki_ref, wi_ref, o_ref, ki_s, key_s, bias_s,
                       *, tq, seq, n_sel, n_heads, n_idx_heads, groups):
    i = pl.program_id(1)
    n_chunks = seq // tq
    n_valid = i + 1

    @pl.when(i == 0)
    def _():
        ki_s[...] = ki_ref[:, :IDX_DIM].astype(BF16)

    row = lax.broadcasted_iota(jnp.int32, (tq, tq), 0)
    col = lax.broadcasted_iota(jnp.int32, (tq, tq), 1)

    def visible(c):
        return col <= row + jnp.where(c == i, 0, tq)

    w_all = wi_ref[:, IDX_DIM:IDX_DIM + n_idx_heads] * ((IDX_DIM ** -0.5) * (n_idx_heads ** -0.5))
    for c in range(n_chunks):
        @pl.when(c < n_valid)
        def _(c=c):
            kic = ki_s[c * tq:(c + 1) * tq, :]
            s = None
            for h in range(n_idx_heads):
                d = lax.dot_general(qi_ref[:, h * IDX_DIM:(h + 1) * IDX_DIM], kic, _NT,
                                    preferred_element_type=F32)
                term = jnp.maximum(d, 0.0) * w_all[:, h:h + 1]
                s = term if s is None else s + term
            key_s[c] = _ordered_key(jnp.where(visible(c), s, NEG_INF))

    def count_ge(cand):
        def body(c, acc):
            hit = (key_s[c] >= cand).astype(F32)
            part = hit[:, 0:LANES]
            for s in range(1, tq // LANES):
                part = part + hit[:, s * LANES:(s + 1) * LANES]
            return acc + part

        acc = lax.fori_loop(0, n_valid, body, jnp.zeros((tq, LANES), F32))
        return jnp.sum(acc, axis=1, keepdims=True)

    thr = _kth_largest_key(count_ge, tq, n_sel)

    def kv_cols(h):
        return slice((h // groups) * HEAD_DIM, (h // groups + 1) * HEAD_DIM)

    def attend(width):
        for c in range(width // tq):
            @pl.when(c < n_valid)
            def _(c=c):
                ok = (key_s[c] >= thr) & visible(c)
                bias_s[:, c * tq:(c + 1) * tq] = jnp.where(ok, 0.0, NEG_INF)

            @pl.when(c >= n_valid)
            def _(c=c):
                bias_s[:, c * tq:(c + 1) * tq] = jnp.full((tq, tq), NEG_INF, F32)

        def qk(h):
            return lax.dot_general(qa_ref[:, h * HEAD_DIM:(h + 1) * HEAD_DIM], ka_ref[0:width, kv_cols(h)], _NT,
                                   preferred_element_type=F32)

        raw = qk(0)
        for h in range(n_heads):
            nxt = qk(h + 1) if h + 1 < n_heads else None
            o = _softmax_pv(raw + bias_s[:, 0:width], va_ref[0:width, kv_cols(h)])
            o_ref[:, h * HEAD_DIM:(h + 1) * HEAD_DIM] = o.astype(o_ref.dtype)
            raw = nxt

    for lo, hi, width in _width_classes(seq, tq):
        @pl.when((i >= lo) & (i < hi))
        def _(width=width):
            attend(width)


def _dsa_prompt(hq, hk_bf, hv_bf, hqi, hkiwi, *, batch, seq, n_heads, n_idx_heads, groups):
    tq = min(256, seq)
    assert seq % tq == 0 and tq % LANES == 0
    nq = seq // tq
    aw = n_heads * HEAD_DIM
    kvw = aw // groups
    n_sel = min(IDX_TOPK, seq // 4)
    kern = functools.partial(_dsa_prompt_kernel, tq=tq, seq=seq, n_sel=n_sel, n_heads=n_heads,
                             n_idx_heads=n_idx_heads, groups=groups)
    return pl.pallas_call(
        kern,
        out_shape=jax.ShapeDtypeStruct((batch * seq, aw), BF16),
        grid=(batch, nq),
        in_specs=[pl.BlockSpec((tq, aw), lambda b, i: (b * nq + i, 0)),
                  pl.BlockSpec((seq, kvw), lambda b, i: (b, 0)),
                  pl.BlockSpec((seq, kvw), lambda b, i: (b, 0)),
                  pl.BlockSpec((tq, n_idx_heads * IDX_DIM), lambda b, i: (b * nq + i, 0)),
                  pl.BlockSpec((seq, LANES), lambda b, i: (b, 0)),
                  pl.BlockSpec((tq, LANES), lambda b, i: (b * nq + i, 0))],
        out_specs=pl.BlockSpec((tq, aw), lambda b, i: (b * nq + i, 0)),
        scratch_shapes=[pltpu.VMEM((seq, IDX_DIM), BF16), pltpu.VMEM((nq, tq, tq), jnp.int32),
                        pltpu.VMEM((tq, seq), F32)],
        compiler_params=_params(("parallel", "arbitrary")),
        name="dsa_prompt",
    )(hq, hk_bf, hv_bf, hqi, hkiwi, hkiwi)


def _moba_prompt_kernel(qb_ref, kb_ref, kmean_ref, vb_ref, o_ref, km_s, *, nb, n_sel, n_heads, groups):
    i = pl.program_id(1)
    blk = MOBA_BLOCK

    @pl.when(i == 0)
    def _():
        km_s[...] = jnp.zeros_like(km_s)
        per = kmean_ref.shape[1]
        for j in range(nb):
            km_s[j:j + 1, :] = kmean_ref[j // per, j % per:j % per + 1, :]

    past = lax.broadcasted_iota(jnp.int32, (blk, LANES), 1) < i
    tril_bias = jnp.where(lax.broadcasted_iota(jnp.int32, (blk, blk), 1)
                          <= lax.broadcasted_iota(jnp.int32, (blk, blk), 0), 0.0, NEG_INF)

    def kv_cols(h):
        return slice((h // groups) * HEAD_DIM, (h // groups + 1) * HEAD_DIM)

    def attend(width):
        def qk(h):
            q = qb_ref[:, h * HEAD_DIM:(h + 1) * HEAD_DIM]
            gate = lax.dot_general(q, km_s[:, kv_cols(h)].astype(BF16), _NT, preferred_element_type=F32)
            return gate, lax.dot_general(q, kb_ref[0:width, kv_cols(h)], _NT, preferred_element_type=F32)

        cur = qk(0)
        for h in range(n_heads):
            nxt = qk(h + 1) if h + 1 < n_heads else None
            gate, raw = cur
            sel_bias = jnp.where(_topk_lanes(gate, past, n_sel), 0.0, NEG_INF)
            pieces = []
            for j in range(width // blk):
                bias = jnp.where(j == i, tril_bias, sel_bias[:, j:j + 1])
                pieces.append(raw[:, j * blk:(j + 1) * blk] + bias)
            o = _softmax_pv(jnp.concatenate(pieces, axis=1), vb_ref[0:width, kv_cols(h)])
            o_ref[:, h * HEAD_DIM:(h + 1) * HEAD_DIM] = o.astype(o_ref.dtype)
            cur = nxt

    for lo, hi, width in _width_classes(nb * blk, blk):
        @pl.when((i >= lo) & (i < hi))
        def _(width=width):
            attend(width)


def _moba_prompt(hq, hk_bf, k_means, hv_bf, *, batch, seq, n_heads, groups):
    blk = MOBA_BLOCK
    assert seq % blk == 0
    nb = seq // blk
    g_per_seq = k_means.shape[0] // batch
    assert 1 <= nb <= LANES and g_per_seq * k_means.shape[1] == nb
    aw = n_heads * HEAD_DIM
    kvw = aw // groups
    kern = functools.partial(_moba_prompt_kernel, nb=nb, n_sel=min(MOBA_TOPK, nb), n_heads=n_heads,
                             groups=groups)
    return pl.pallas_call(
        kern,
        out_shape=jax.ShapeDtypeStruct((batch * seq, aw), BF16),
        grid=(batch, nb),
        in_specs=[pl.BlockSpec((blk, aw), lambda b, i: (b * nb + i, 0)),
                  pl.BlockSpec((seq, kvw), lambda b, i: (b, 0)),
                  pl.BlockSpec((g_per_seq, k_means.shape[1], kvw), lambda b, i: (b, 0, 0)),
                  pl.BlockSpec((seq, kvw), lambda b, i: (b, 0))],
        out_specs=pl.BlockSpec((blk, aw), lambda b, i: (b * nb + i, 0)),
        scratch_shapes=[pltpu.VMEM((LANES, kvw), F32)],
        compiler_params=_params(("parallel", "arbitrary")),
        name="moba_prompt",
    )(hq, hk_bf, k_means, hv_bf)


def _gmlp_kernel(cu_ref, cv_ref, g_ref, b_ref, ws_ref, bs_ref, o_ref, *vn_out, n_groups):
    v = cv_ref[...]
    mu = jnp.mean(v, axis=-1, keepdims=True)
    vc = v - mu
    vn = vc * lax.rsqrt(jnp.mean(vc * vc, axis=-1, keepdims=True) + EPS) * g_ref[...] + b_ref[...]
    if vn_out:
        vn_out[0][...] = vn
    for g in range(n_groups):
        cols = slice(g * C_GROUP_DIM, (g + 1) * C_GROUP_DIM)
        z = jnp.dot(ws_ref[g], vn[:, cols].astype(BF16), preferred_element_type=F32) + bs_ref[:, g:g + 1]
        o_ref[:, cols] = (cu_ref[:, cols] * z).astype(o_ref.dtype)


def _gmlp(hc, ln_g, ln_b, ws, bs, *, want_vn):
    m = hc.shape[0]
    n_groups, r = ws.shape[0], ws.shape[1]
    cw = n_groups * C_GROUP_DIM
    out_shape = [jax.ShapeDtypeStruct((m, cw), BF16)]
    out_specs = [pl.BlockSpec((r, cw), lambda i: (i, 0))]
    if want_vn:
        out_shape.append(jax.ShapeDtypeStruct((m, cw), F32))
        out_specs.append(pl.BlockSpec((r, cw), lambda i: (i, 0)))
    return pl.pallas_call(
        functools.partial(_gmlp_kernel, n_groups=n_groups),
        out_shape=out_shape,
        grid=(m // r,),
        in_specs=[pl.BlockSpec((r, cw), lambda i: (i, 0)),
                  pl.BlockSpec((r, cw), lambda i: (i, 1)),
                  pl.BlockSpec((1, cw), lambda i: (0, 0)),
                  pl.BlockSpec((1, cw), lambda i: (0, 0)),
                  pl.BlockSpec((n_groups, r, r), lambda i: (0, 0, 0)),
                  pl.BlockSpec((r, n_groups), lambda i: (0, 0))],
        out_specs=out_specs,
        compiler_params=_params(("parallel",)),
        name="gmlp",
    )(hc, hc, ln_g.reshape(1, cw), ln_b.reshape(1, cw), ws, bs)


def _merge_kernel(oa_ref, ob_ref, oc_ref, wb_ref, g0_ref, g1_ref, g2_ref, o_ref, wbf_s):
    @pl.when(pl.program_id(1) == 0)
    def _():
        wbf_s[...] = wb_ref[...].astype(BF16)

    acc = None
    for o_n, g_n, n in ((oa_ref, g0_ref, 0), (ob_ref, g1_ref, 1), (oc_ref, g2_ref, 2)):
        br = jnp.dot(o_n[...], wbf_s[n], preferred_element_type=F32)
        term = (1.0 / (1.0 + jnp.exp(-g_n[...]))) * br
        acc = term if acc is None else acc + term
    o_ref[...] = acc.astype(o_ref.dtype)


def _merge(o_a, o_b, o_c, wb_stack, layer, hc, *, d_model, gate_col0):
    m, aw = o_a.shape
    tm, tn = min(1024, m), 512
    nj = d_model // tn
    g0 = gate_col0 // tn
    gate_specs = [pl.BlockSpec((tm, tn), functools.partial(lambda j, i, n: (i, g0 + n * nj + j), n=n))
                  for n in range(N_BRANCH)]
    return pl.pallas_call(
        _merge_kernel,
        out_shape=jax.ShapeDtypeStruct((m, d_model), BF16),
        grid=(nj, m // tm),
        in_specs=[pl.BlockSpec((tm, aw), lambda j, i: (i, 0))] * 3
        + [pl.BlockSpec((None, N_BRANCH, aw, tn), lambda j, i: (layer, 0, 0, j))] + gate_specs,
        out_specs=pl.BlockSpec((tm, tn), lambda j, i: (i, j)),
        scratch_shapes=[pltpu.VMEM((N_BRANCH, aw, tn), BF16)],
        compiler_params=_params(("parallel", "arbitrary")),
        name="merge",
    )(o_a, o_b, o_c, wb_stack, hc, hc, hc)


def _page_specs(block, layer, n_pages, per_step):
    def page_map(b, s, pt, slot):
        return (layer, pt[b * n_pages + s * per_step + slot], 0, 0)

    return [pl.BlockSpec(block, functools.partial(page_map, slot=slot)) for slot in range(per_step)]


def _idx_scores(qi, wi, k, n_new, n_idx_heads, k_is_t=False):
    if k_is_t:
        d = jnp.dot(qi, k, preferred_element_type=F32) * (IDX_DIM ** -0.5)
    else:
        d = lax.dot_general(qi, k, _NT, preferred_element_type=F32) * (IDX_DIM ** -0.5)
    dw = jnp.maximum(d, 0.0) * wi
    acc = dw[0:8]
    for r in range(8, n_idx_heads * n_new, 8):
        acc = acc + dw[r:r + 8]
    s = acc[0:n_new]
    for r in range(n_new, 8, n_new):
        s = s + acc[r:r + n_new]
    return s * (n_idx_heads ** -0.5)


def _dsa_sample_score_kernel(pt_ref, *refs, per_step, n_new, n_idx_heads):
    pages = refs[:per_step]
    qi_ref, wi_ref, o_ref = refs[per_step:]
    for g in range(per_step):
        o_ref[:, g * LANES:(g + 1) * LANES] = _idx_scores(qi_ref[...], wi_ref[...], pages[g][...], n_new,
                                                         n_idx_heads, k_is_t=True)


def _dsa_sample_scores(pt_flat, cache_idx_t, layer, qi_rows, wi_rows, *, dec_batch, n_pages, n_new, n_idx_heads):
    page = cache_idx_t.shape[3]
    assert page == LANES and 8 % n_new == 0 and (n_idx_heads * n_new) % 8 == 0
    per_step = min(PAGES_PER_STEP, n_pages)
    assert n_pages % per_step == 0
    rows = n_idx_heads * n_new
    kern = functools.partial(_dsa_sample_score_kernel, per_step=per_step, n_new=n_new, n_idx_heads=n_idx_heads)
    grid_spec = pltpu.PrefetchScalarGridSpec(
        num_scalar_prefetch=1,
        grid=(dec_batch, n_pages // per_step),
        in_specs=_page_specs((None, None, IDX_DIM, page), layer, n_pages, per_step)
        + [pl.BlockSpec((None, rows, IDX_DIM), lambda b, s, pt: (b, 0, 0)),
           pl.BlockSpec((None, rows, 1), lambda b, s, pt: (b, 0, 0))],
        out_specs=pl.BlockSpec((None, n_new, per_step * LANES), lambda b, s, pt: (b, 0, s)),
    )
    return pl.pallas_call(
        kern,
        out_shape=jax.ShapeDtypeStruct((dec_batch, n_new, n_pages * LANES), F32),
        grid_spec=grid_spec,
        compiler_params=_params(("parallel", "arbitrary")),
        name="dsa_sample_scores",
    )(pt_flat, *([cache_idx_t] * per_step), qi_rows, wi_rows)


def _select_kernel(s_ref, qi_ref, wi_ref, kn_ref, o_ref, on_ref, key_s, keyn_s, *, n_sel, n_new, n_idx_heads):
    t = lax.broadcasted_iota(jnp.int32, (n_new, LANES), 0)
    c = lax.broadcasted_iota(jnp.int32, (n_new, LANES), 1)
    new_ok = c <= t
    s_new = _idx_scores(qi_ref[...], wi_ref[...], kn_ref[...], n_new, n_idx_heads)
    key_s[...] = _ordered_key(s_ref[...])
    keyn_s[...] = _ordered_key(jnp.where(new_ok, s_new, NEG_INF))

    def count_ge(cand):
        return (jnp.sum((key_s[...] >= cand).astype(F32), axis=1, keepdims=True)
                + jnp.sum((keyn_s[...] >= cand).astype(F32), axis=1, keepdims=True))

    thr = _kth_largest_key(count_ge, n_new, n_sel)
    o_ref[...] = (key_s[...] >= thr).astype(F32)
    on_ref[...] = ((keyn_s[...] >= thr) & new_ok).astype(F32)


def _dsa_sample_select(scores, qi_rows, wi_rows, ki_new_pad, n_sel, *, n_idx_heads):
    db, n_new, length = scores.shape
    rows = n_idx_heads * n_new
    return pl.pallas_call(
        functools.partial(_select_kernel, n_sel=n_sel, n_new=n_new, n_idx_heads=n_idx_heads),
        out_shape=[jax.ShapeDtypeStruct(scores.shape, F32), jax.ShapeDtypeStruct((db, n_new, LANES), F32)],
        grid=(db,),
        in_specs=[pl.BlockSpec((None, n_new, length), lambda b: (b, 0, 0)),
                  pl.BlockSpec((None, rows, IDX_DIM), lambda b: (b, 0, 0)),
                  pl.BlockSpec((None, rows, 1), lambda b: (b, 0, 0)),
                  pl.BlockSpec((None, LANES, IDX_DIM), lambda b: (b, 0, 0))],
        out_specs=[pl.BlockSpec((None, n_new, length), lambda b: (b, 0, 0)),
                   pl.BlockSpec((None, n_new, LANES), lambda b: (b, 0, 0))],
        scratch_shapes=[pltpu.VMEM((n_new, length), jnp.int32), pltpu.VMEM((n_new, LANES), jnp.int32)],
        compiler_params=_params(("parallel",)),
        name="dsa_sample_select",
    )(scores, qi_rows, wi_rows, ki_new_pad)


def _head_rows(page_refs, kv, n_kv):
    parts = [r[pl.ds(kv, LANES, stride=n_kv), :] for r in page_refs]
    return parts[0] if len(parts) == 1 else jnp.concatenate(parts, axis=0)


def _dsa_sample_attn_kernel(pt_ref, *refs, per_step, n_steps, n_new, n_kv, groups):
    k_pages, v_pages = refs[:per_step], refs[per_step:2 * per_step]
    kn_ref, vn_ref, q_ref, sel_ref, seln_ref, o_ref, m_s, l_s, acc_s = refs[2 * per_step:]
    step = pl.program_id(1)
    rpk = groups * n_new

    @pl.when(step == 0)
    def _():
        m_s[...] = jnp.full_like(m_s, NEG_INF)
        l_s[...] = jnp.zeros_like(l_s)
        acc_s[...] = jnp.zeros_like(acc_s)

    def update(get_k, get_v, sel):
        mask = jnp.concatenate([sel] * groups, axis=0) > 0.0
        rows = [slice(kv * rpk, (kv + 1) * rpk) for kv in range(n_kv)]
        logits = [lax.dot_general(q_ref[rows[kv], :], get_k(kv), _NT, preferred_element_type=F32)
                  for kv in range(n_kv)]
        probs, alphas = [], []
        for kv in range(n_kv):
            s = jnp.where(mask, logits[kv], NEG_INF)
            m_old = m_s[rows[kv], :]
            m_new = jnp.maximum(m_old, jnp.max(s, axis=1, keepdims=True))
            m_safe = jnp.where(m_new == NEG_INF, 0.0, m_new)
            alpha = jnp.exp2(m_old - m_safe)
            pr = jnp.exp2(s - m_safe)
            m_s[rows[kv], :] = m_new
            l_s[rows[kv], :] = alpha * l_s[rows[kv], :] + jnp.sum(pr, axis=1, keepdims=True)
            probs.append(pr)
            alphas.append(alpha)
        for kv in range(n_kv):
            acc_s[rows[kv], :] = alphas[kv] * acc_s[rows[kv], :] + jnp.dot(
                probs[kv], get_v(kv), preferred_element_type=F32)

    update(lambda kv: _head_rows(k_pages, kv, n_kv), lambda kv: _head_rows(v_pages, kv, n_kv), sel_ref[...])

    @pl.when(step == n_steps - 1)
    def _():
        update(lambda kv: kn_ref[kv], lambda kv: vn_ref[kv], seln_ref[...])
        o_ref[...] = acc_s[...] / l_s[...]


def _dsa_sample_attn(pt_flat, cache_k, cache_v, layer, k_new_pad, v_new_pad, q_rows, sel, sel_new, *, dec_batch,
                     n_pages, n_new, n_kv, groups):
    rows = n_kv * groups * n_new
    prow = cache_k.shape[2]
    per_step = min(PAGES_PER_STEP, n_pages)
    assert n_pages % per_step == 0
    n_steps = n_pages // per_step
    kern = functools.partial(_dsa_sample_attn_kernel, per_step=per_step, n_steps=n_steps, n_new=n_new, n_kv=n_kv,
                             groups=groups)
    page_specs = _page_specs((None, None, prow, HEAD_DIM), layer, n_pages, per_step)
    grid_spec = pltpu.PrefetchScalarGridSpec(
        num_scalar_prefetch=1,
        grid=(dec_batch, n_steps),
        in_specs=page_specs + page_specs
        + [pl.BlockSpec((None, n_kv, LANES, HEAD_DIM), lambda b, s, pt: (b, 0, 0, 0)),
           pl.BlockSpec((None, n_kv, LANES, HEAD_DIM), lambda b, s, pt: (b, 0, 0, 0)),
           pl.BlockSpec((None, rows, HEAD_DIM), lambda b, s, pt: (b, 0, 0)),
           pl.BlockSpec((None, n_new, per_step * LANES), lambda b, s, pt: (b, 0, s)),
           pl.BlockSpec((None, n_new, LANES), lambda b, s, pt: (b, 0, 0))],
        out_specs=pl.BlockSpec((None, rows, HEAD_DIM), lambda b, s, pt: (b, 0, 0)),
        scratch_shapes=[pltpu.VMEM((rows, 1), F32), pltpu.VMEM((rows, 1), F32),
                        pltpu.VMEM((rows, HEAD_DIM), F32)],
    )
    return pl.pallas_call(
        kern,
        out_shape=jax.ShapeDtypeStruct((dec_batch, rows, HEAD_DIM), F32),
        grid_spec=grid_spec,
        compiler_params=_params(("parallel", "arbitrary")),
        name="dsa_sample_attn",
    )(pt_flat, *([cache_k] * per_step), *([cache_v] * per_step), k_new_pad, v_new_pad, q_rows, sel, sel_new)


def _moba_sample_kernel(pt_ref, *refs, bps, n_steps, nb, n_sel, n_new, n_kv, groups, past_len):
    k_pages, v_pages = refs[:2 * bps], refs[2 * bps:4 * bps]
    kn_ref, vn_ref, q_ref, o_ref, gate_s, m_s, l_s, acc_s = refs[4 * bps:]
    step = pl.program_id(1)
    rpk = groups * n_new
    rows_all = n_kv * rpk
    lane = lax.broadcasted_iota(jnp.int32, (rows_all, LANES), 1)

    @pl.when(step == 0)
    def _():
        gate_s[...] = jnp.zeros_like(gate_s)
        m_s[...] = jnp.zeros_like(m_s)
        l_s[...] = jnp.zeros_like(l_s)

    gate, m_blk, l_blk = gate_s[...], m_s[...], l_s[...]
    blk = MOBA_BLOCK
    kv_rows = [slice(kv * rpk, (kv + 1) * rpk) for kv in range(n_kv)]
    ks = [_head_rows(k_pages, kv, n_kv) for kv in range(n_kv)]
    logits = [lax.dot_general(q_ref[kv_rows[kv], :], ks[kv], _NT, preferred_element_type=F32)
              for kv in range(n_kv)]
    probs = {}
    for bb in range(bps):
        g_cols, m_cols, l_cols = [], [], []
        for kv in range(n_kv):
            kmean = jnp.mean(ks[kv][bb * blk:(bb + 1) * blk, :], axis=0, keepdims=True)
            g_cols.append(jnp.sum(q_ref[kv_rows[kv], :] * kmean, axis=1, keepdims=True))
            s = logits[kv][:, bb * blk:(bb + 1) * blk]
            m = jnp.max(s, axis=1, keepdims=True)
            pr = jnp.exp2(s - m)
            m_cols.append(m)
            l_cols.append(jnp.sum(pr, axis=1, keepdims=True))
            probs[bb, kv] = pr
        here = lane == step * bps + bb
        gate = jnp.where(here, jnp.concatenate(g_cols, axis=0), gate)
        m_blk = jnp.where(here, jnp.concatenate(m_cols, axis=0), m_blk)
        l_blk = jnp.where(here, jnp.concatenate(l_cols, axis=0), l_blk)
    for bb in range(bps):
        for kv in range(n_kv):
            v = _head_rows(v_pages[2 * bb:2 * bb + 2], kv, n_kv)
            acc_s[step * bps + bb, kv_rows[kv], :] = jnp.dot(probs[bb, kv], v, preferred_element_type=F32)
    gate_s[...] = gate
    m_s[...] = m_blk
    l_s[...] = l_blk

    @pl.when(step == n_steps - 1)
    def _():
        t_row = lax.broadcasted_iota(jnp.int32, (rpk, LANES), 0) & (n_new - 1)
        c_col = lax.broadcasted_iota(jnp.int32, (rpk, LANES), 1)
        own_ok = c_col <= t_row
        t_all = lax.broadcasted_iota(jnp.int32, (rows_all, LANES), 0) & (n_new - 1)
        own_blk = jnp.right_shift(past_len + t_all, MOBA_BLOCK.bit_length() - 1)
        valid = (lane < own_blk) & (lane < nb)
        sel = _topk_lanes(gate, valid, n_sel)
        for kv in range(n_kv):
            rows = slice(kv * rpk, (kv + 1) * rpk)
            q = q_ref[rows, :]
            s = lax.dot_general(q, kn_ref[kv], _NT, preferred_element_type=F32)
            s = jnp.where(own_ok, s, NEG_INF)
            m_o = jnp.max(s, axis=1, keepdims=True)
            m_sel = jnp.where(sel[rows, :], m_blk[rows, :], NEG_INF)
            m_all = jnp.maximum(m_o, jnp.max(m_sel, axis=1, keepdims=True))
            pr = jnp.exp2(s - m_all)
            w = jnp.where(sel[rows, :], jnp.exp2(m_sel - m_all), 0.0)
            den = jnp.sum(pr, axis=1, keepdims=True) + jnp.sum(w * l_blk[rows, :], axis=1, keepdims=True)
            num = jnp.dot(pr, vn_ref[kv], preferred_element_type=F32)
            for jb in range(nb):
                num = num + w[:, jb:jb + 1] * acc_s[jb, rows, :]
            o_ref[rows, :] = num / den


def _moba_sample(pt_flat, cache_k, cache_v, layer, k_new_pad, v_new_pad, q_rows, *, dec_batch, n_pages, n_new,
                 n_kv, groups, past_len):
    assert past_len % MOBA_BLOCK == 0 and n_new < MOBA_BLOCK and MOBA_BLOCK == 2 * LANES
    assert n_new & (n_new - 1) == 0
    nb = (past_len + n_new) // MOBA_BLOCK
    assert 1 <= nb <= LANES and n_pages == 2 * nb
    bps = min(PAGES_PER_STEP // 2, nb)
    assert nb % bps == 0
    n_steps = nb // bps
    rows = n_kv * groups * n_new
    prow = cache_k.shape[2]
    kern = functools.partial(_moba_sample_kernel, bps=bps, n_steps=n_steps, nb=nb, n_sel=min(MOBA_TOPK, nb),
                             n_new=n_new, n_kv=n_kv, groups=groups, past_len=past_len)
    page_specs = _page_specs((None, None, prow, HEAD_DIM), layer, n_pages, 2 * bps)
    grid_spec = pltpu.PrefetchScalarGridSpec(
        num_scalar_prefetch=1,
        grid=(dec_batch, n_steps),
        in_specs=page_specs + page_specs
        + [pl.BlockSpec((None, n_kv, LANES, HEAD_DIM), lambda b, s, pt: (b, 0, 0, 0)),
           pl.BlockSpec((None, n_kv, LANES, HEAD_DIM), lambda b, s, pt: (b, 0, 0, 0)),
           pl.BlockSpec((None, rows, HEAD_DIM), lambda b, s, pt: (b, 0, 0))],
        out_specs=pl.BlockSpec((None, rows, HEAD_DIM), lambda b, s, pt: (b, 0, 0)),
        scratch_shapes=[pltpu.VMEM((rows, LANES), F32), pltpu.VMEM((rows, LANES), F32),
                        pltpu.VMEM((rows, LANES), F32), pltpu.VMEM((nb, rows, HEAD_DIM), F32)],
    )
    return pl.pallas_call(
        kern,
        out_shape=jax.ShapeDtypeStruct((dec_batch, rows, HEAD_DIM), F32),
        grid_spec=grid_spec,
        compiler_params=_params(("parallel", "arbitrary")),
        name="moba_sample",
    )(pt_flat, *([cache_k] * (2 * bps)), *([cache_v] * (2 * bps)), k_new_pad, v_new_pad, q_rows)


def _rope_tables(pos, rot, width, n_rope_lanes=LANES):
    half = rot // 2
    inv = jnp.power(ROPE_THETA, -jnp.arange(half, dtype=F32) / half)
    ang = pos.astype(F32)[:, None] * inv[None, :]
    cos, sin = jnp.cos(ang), jnp.sin(ang)
    t = pos.shape[0]
    zeros = jnp.zeros((t, half), F32)
    rest0 = jnp.zeros((t, width - rot), F32)
    c = jnp.concatenate([cos, cos, jnp.ones((t, width - rot), F32)], axis=1)
    sa = jnp.concatenate([-sin, zeros, rest0], axis=1)
    sb = jnp.concatenate([zeros, sin, rest0], axis=1)
    reps = n_rope_lanes // width
    pad = LANES - n_rope_lanes
    c = jnp.concatenate([jnp.tile(c, (1, reps)), jnp.ones((t, pad), F32)], axis=1)
    sa = jnp.concatenate([jnp.tile(sa, (1, reps)), jnp.zeros((t, pad), F32)], axis=1)
    sb = jnp.concatenate([jnp.tile(sb, (1, reps)), jnp.zeros((t, pad), F32)], axis=1)
    return c, sa, sb


def _kv_major_rows(a, db, n_new, n_kv, groups):
    a = a.reshape(db, n_new, n_kv, groups, HEAD_DIM)
    return jnp.transpose(a, (0, 2, 3, 1, 4)).reshape(db, n_kv * groups * n_new, HEAD_DIM)


def _from_kv_major_rows(o, db, n_new, n_kv, groups):
    o = o.reshape(db, n_kv, groups, n_new, HEAD_DIM)
    return jnp.transpose(o, (0, 3, 1, 2, 4)).reshape(db * n_new, n_kv * groups * HEAD_DIM)


def _new_kv_pad(k, db, n_new, n_kv):
    k = jnp.transpose(k.reshape(db, n_new, n_kv, HEAD_DIM), (0, 2, 1, 3))
    return jnp.pad(k, ((0, 0), (0, 0), (0, LANES - n_new), (0, 0)))


def kernel(x_prompt, x_sample, cache_a_k, cache_a_v, cache_a_idx, cache_b_k, cache_b_v, page_table, w_in, norm1_g,
           c_ln_g, c_ln_b, w_spatial, b_spatial, w_branch, w_out, norm2_g, w_ff1, w_ff2, norm_f_g):
    batch, seq, d_model = x_prompt.shape
    db, n_new, _ = x_sample.shape
    depth, n_phys, page, n_kv, _ = cache_a_k.shape
    n_pages = page_table.shape[1]
    past_len = n_pages * page
    n_heads = d_model // 256
    groups = n_heads // n_kv
    aw = n_heads * HEAD_DIM
    kvw = n_kv * HEAD_DIM
    n_idx_heads = d_model // 128
    iw = n_idx_heads * IDX_DIM
    n_cg = w_spatial.shape[1]
    cw = n_cg * C_GROUP_DIM
    rot, idx_rot = HEAD_DIM // 4, IDX_DIM // 4
    assert page == LANES and aw == 2 * kvw and cw % 512 == 0 and seq % CHUNK == 0 and n_new <= CHUNK

    sizes = (aw, kvw, kvw, iw, IDX_DIM, n_idx_heads, aw, kvw, kvw, cw, cw, N_BRANCH * d_model)
    offs = [0]
    for s in sizes:
        offs.append(offs[-1] + s)
    (o_qa, o_ka, o_va, o_qi, o_ki, o_wi, o_qb, o_kb, o_vb, o_cu, o_cv, o_gl, _) = offs

    pos_p = jnp.arange(seq, dtype=jnp.int32)
    pos_s = jnp.tile(past_len + jnp.arange(n_new, dtype=jnp.int32), db)
    tabs = {
        "p": (_rope_tables(pos_p, rot, HEAD_DIM), _rope_tables(pos_p, idx_rot, IDX_DIM),
              _rope_tables(pos_p, idx_rot, IDX_DIM, IDX_DIM)),
        "s": (_rope_tables(pos_s, rot, HEAD_DIM), _rope_tables(pos_s, idx_rot, IDX_DIM),
              _rope_tables(pos_s, idx_rot, IDX_DIM, IDX_DIM)),
    }

    cache_ak = cache_a_k.reshape(depth, n_phys, page * n_kv, HEAD_DIM)
    cache_av = cache_a_v.reshape(depth, n_phys, page * n_kv, HEAD_DIM)
    cache_bk = cache_b_k.reshape(depth, n_phys, page * n_kv, HEAD_DIM)
    cache_bv = cache_b_v.reshape(depth, n_phys, page * n_kv, HEAD_DIM)
    pt_flat = page_table.reshape(-1).astype(jnp.int32)

    xp = x_prompt.reshape(batch * seq, d_model)
    xs = x_sample.reshape(db * n_new, d_model)
    rs = db * n_new
    tri = jnp.tril(jnp.ones((CHUNK, CHUNK), dtype=bool))
    sel_s = min(IDX_TOPK, (past_len + n_new) // 4)

    outs = {k: [] for k in ("pa_k", "pa_v", "pa_i", "pb_k", "pb_v", "sa_k", "sa_v", "sa_i", "sb_k", "sb_v", "sc_v")}

    w_in_t = jnp.swapaxes(w_in, 1, 2)
    cache_idx_t = jnp.swapaxes(cache_a_idx, 2, 3)

    kv_bufs = {(which, name): None for which in "ps" for name in ("ka", "kb", "va", "vb")}

    def project(x, l, which):
        t128, t64, tkiwi = tabs[which]
        xn = _rms_norm(x, norm1_g[l], BF16)

        def proj(n_off, n, out_dtypes, epi="none", extra=(), half=0, out_scale=1.0, **kw):
            return _matmul_ws(xn, w_in_t, l, out_dtypes, epi, extra, half, n_off=n_off, n=n, w_rows_are_n=True,
                              out_scale=out_scale, **kw)

        def kv_proj(name, n_off, epi="none", extra=(), half=0, want_means=False):
            res = proj(n_off, kvw, (BF16,), epi, extra, half, kv_out=(kv_bufs[which, name], depth, n_kv),
                       want_means=want_means)
            kv_bufs[which, name] = res[1]
            return (res[0], res[2]) if want_means else res[0]

        (hqa,) = proj(o_qa, aw, (BF16,), "rope", t128, rot // 2, Q_SCALE)
        (hqb,) = proj(o_qb, aw, (BF16,), "rope", t128, rot // 2, Q_SCALE)
        hka_bf = kv_proj("ka", o_ka, "rope", t128, rot // 2)
        hkb_bf = kv_proj("kb", o_kb, "rope", t128, rot // 2, want_means=(which == "p"))
        hva_bf = kv_proj("va", o_va)
        hvb_bf = kv_proj("vb", o_vb)
        (hqi,) = proj(o_qi, iw, (BF16,), "rope", t64, idx_rot // 2)
        (hkiwi,) = proj(o_ki, LANES, (F32,), "rope", tkiwi, idx_rot // 2)
        (hc,) = proj(o_cu, 2 * cw + N_BRANCH * d_model, (F32,))
        return hqa, hqb, hka_bf, hkb_bf, hva_bf, hvb_bf, hqi, hkiwi, hc

    def new_rows(name, l):
        return kv_bufs["s", name].reshape(depth, rs, kvw)[l]

    def merge_ffn(x, o_a, o_b, o_c, hc, l):
        merged = _merge(o_a, o_b, o_c, w_branch, l, hc, d_model=d_model, gate_col0=2 * cw)
        (x1,) = _matmul_ws(merged, w_out, l, (F32,), "residual", (x,))
        hn = _rms_norm(x1, norm2_g[l], BF16)
        (h1,) = _matmul_ws(hn, w_ff1, l, (BF16,), "relu2")
        (x2,) = _matmul_kt(h1, w_ff2, l, (F32,), "residual", (x1,))
        return x2

    for l in range(depth):
        hqa, hqb, hka_bf, (hkb_bf, kb_means), hva_bf, hvb_bf, hqi, hkiwi, hc = project(xp, l, "p")
        o_a = _dsa_prompt(hqa, hka_bf, hva_bf, hqi, hkiwi, batch=batch, seq=seq, n_heads=n_heads,
                          n_idx_heads=n_idx_heads, groups=groups)
        o_b = _moba_prompt(hqb, hkb_bf, kb_means, hvb_bf, batch=batch, seq=seq, n_heads=n_heads, groups=groups)
        ws_p = jnp.where(tri[None], w_spatial[l], 0).astype(BF16)
        bs_p = b_spatial[l].T
        (o_c,) = _gmlp(hc, c_ln_g[l], c_ln_b[l], ws_p, bs_p, want_vn=False)
        xp = merge_ffn(xp, o_a, o_b, o_c, hc, l)
        outs["pa_i"].append(hkiwi[:, :IDX_DIM].reshape(batch, seq, IDX_DIM))

        hqa, hqb, hka_bf, hkb_bf, hva_bf, hvb_bf, hqi, hkiwi, hc = project(xs, l, "s")
        qi_rows = jnp.transpose(hqi.astype(F32).reshape(db, n_new, n_idx_heads, IDX_DIM), (0, 2, 1, 3)).reshape(
            db, n_idx_heads * n_new, IDX_DIM)
        wi_rows = jnp.transpose(hkiwi[:, IDX_DIM:IDX_DIM + n_idx_heads].reshape(db, n_new, n_idx_heads),
                                (0, 2, 1)).reshape(db, n_idx_heads * n_new, 1)
        ki_new = jnp.pad(hkiwi[:, :IDX_DIM].reshape(db, n_new, IDX_DIM), ((0, 0), (0, LANES - n_new), (0, 0)))
        scores = _dsa_sample_scores(pt_flat, cache_idx_t, l, qi_rows, wi_rows, dec_batch=db, n_pages=n_pages,
                                    n_new=n_new, n_idx_heads=n_idx_heads)
        sel, sel_new = _dsa_sample_select(scores, qi_rows, wi_rows, ki_new, sel_s, n_idx_heads=n_idx_heads)
        qa_rows = _kv_major_rows(hqa.astype(F32), db, n_new, n_kv, groups)
        o_a = _dsa_sample_attn(pt_flat, cache_ak, cache_av, l, _new_kv_pad(new_rows("ka", l), db, n_new, n_kv),
                               _new_kv_pad(new_rows("va", l), db, n_new, n_kv), qa_rows, sel, sel_new, dec_batch=db,
                               n_pages=n_pages, n_new=n_new, n_kv=n_kv, groups=groups)
        o_a = _from_kv_major_rows(o_a, db, n_new, n_kv, groups).astype(BF16)
        qb_rows = _kv_major_rows(hqb.astype(F32), db, n_new, n_kv, groups)
        o_b = _moba_sample(pt_flat, cache_bk, cache_bv, l, _new_kv_pad(new_rows("kb", l), db, n_new, n_kv),
                           _new_kv_pad(new_rows("vb", l), db, n_new, n_kv), qb_rows, dec_batch=db, n_pages=n_pages,
                           n_new=n_new, n_kv=n_kv, groups=groups, past_len=past_len)
        o_b = _from_kv_major_rows(o_b, db, n_new, n_kv, groups).astype(BF16)
        ws_c = jnp.where(tri[None, :n_new, :n_new], w_spatial[l][:, :n_new, :n_new], 0)
        ws_s = jnp.einsum("ab,gij->gaibj", jnp.eye(db, dtype=F32), ws_c).reshape(n_cg, rs, rs).astype(BF16)
        bs_s = jnp.tile(b_spatial[l][:, :n_new].T, (db, 1))
        o_c, vn = _gmlp(hc, c_ln_g[l], c_ln_b[l], ws_s, bs_s, want_vn=True)
        xs = merge_ffn(xs, o_a, o_b, o_c, hc, l)
        outs["sa_i"].append(hkiwi[:, :IDX_DIM].reshape(db, n_new, IDX_DIM))
        outs["sc_v"].append(vn.reshape(db, n_new, cw))

    y_prompt = _rms_norm(xp, norm_f_g, F32).reshape(batch, seq, d_model)
    y_sample = _rms_norm(xs, norm_f_g, F32).reshape(db, n_new, d_model)

    def kv_result(which, name):
        lead = (depth, batch, seq) if which == "p" else (depth, db, n_new)
        return kv_bufs[which, name].reshape(lead + (n_kv, HEAD_DIM))

    return (y_prompt, y_sample, kv_result("p", "ka"), kv_result("p", "va"), jnp.stack(outs["pa_i"]),
            kv_result("p", "kb"), kv_result("p", "vb"), kv_result("s", "ka"), kv_result("s", "va"),
            jnp.stack(outs["sa_i"]), kv_result("s", "kb"), kv_result("s", "vb"), jnp.stack(outs["sc_v"]))
```

```python
import functools
import math

import jax
import jax.numpy as jnp
from jax import lax
from jax.experimental import pallas as pl
from jax.experimental.pallas import tpu as pltpu

HEAD_DIM = 128
ROPE_THETA = 500000.0
IDX_DIM = 64
IDX_TOPK = 256
MOBA_BLOCK = 256
MOBA_TOPK = 3
CHUNK = 128
C_GROUP_DIM = 128
N_BRANCH = 3
EPS = 1e-6
LANES = 128
INT32_MIN = -(2 ** 31)
PAGES_PER_STEP = 16
Q_SCALE = HEAD_DIM ** -0.5 * math.log2(math.e)

F32 = jnp.float32
BF16 = jnp.bfloat16
NEG_INF = float("-inf")
VMEM_LIMIT = 56 * 1024 * 1024

_NT = (((1,), (1,)), ((), ()))


def _params(sem):
    return pltpu.CompilerParams(dimension_semantics=sem, vmem_limit_bytes=VMEM_LIMIT)


def _rms_kernel(x_ref, g_ref, o_ref):
    x = x_ref[...]
    y = x * lax.rsqrt(jnp.mean(x * x, axis=-1, keepdims=True) + EPS)
    o_ref[...] = (y * g_ref[...]).astype(o_ref.dtype)


def _rms_norm(x, g, out_dtype):
    m, d = x.shape
    tm = min(1024, m)
    return pl.pallas_call(
        _rms_kernel,
        out_shape=jax.ShapeDtypeStruct((m, d), out_dtype),
        grid=(m // tm,),
        in_specs=[pl.BlockSpec((tm, d), lambda i: (i, 0)),
                  pl.BlockSpec((1, d), lambda i: (0, 0))],
        out_specs=pl.BlockSpec((tm, d), lambda i: (i, 0)),
        compiler_params=_params(("parallel",)),
        name="rms_norm",
    )(x, g.reshape(1, d))


def _rope_epilogue(r, c, sa, sb, half):
    outs = []
    for s in range(r.shape[1] // LANES):
        a = r[:, s * LANES:(s + 1) * LANES]
        outs.append(a * c + pltpu.roll(a, LANES - half, 1) * sa + pltpu.roll(a, half, 1) * sb)
    return outs[0] if len(outs) == 1 else jnp.concatenate(outs, axis=1)


_N_EXTRA = {"none": 0, "relu2": 0, "rope": 3, "residual": 1}


def _epilogue(r, epi, extra, half, outs, out_scale=1.0):
    if epi == "rope":
        r = _rope_epilogue(r, extra[0][...], extra[1][...], extra[2][...], half)
        if out_scale != 1.0:
            r = r * out_scale
    elif epi == "relu2":
        r = jnp.square(jnp.maximum(r, 0.0))
    elif epi == "residual":
        r = extra[0][...] + r
    for o_ref in outs:
        o_ref[...] = r.astype(o_ref.dtype)
    return r


def _mm_ws_kernel(*refs, epi, n_out, half, w_rows_are_n, out_scale, n_kv, has_buf, want_means):
    a_ref, w_ref = refs[0], refs[1]
    extra = refs[2:2 + _N_EXTRA[epi]]
    first_out = 2 + _N_EXTRA[epi] + (1 if has_buf else 0)
    outs = refs[first_out:first_out + n_out]
    wbf_s = refs[-1]

    @pl.when(pl.program_id(1) == 0)
    def _():
        wbf_s[...] = (w_ref[0] if w_rows_are_n else w_ref[...]).astype(BF16)

    if w_rows_are_n:
        part = lax.dot_general(a_ref[...], wbf_s[...], _NT, preferred_element_type=F32)
    else:
        part = jnp.dot(a_ref[...], wbf_s[...], preferred_element_type=F32)
    r = _epilogue(part, epi, extra, half, outs, out_scale)
    if n_kv:
        kv_ref = refs[first_out + n_out]
        tm = r.shape[0]
        for kv in range(n_kv):
            kv_ref[pl.ds(kv, tm, stride=n_kv), :] = r[:, kv * HEAD_DIM:(kv + 1) * HEAD_DIM]
    if want_means:
        mean_ref = refs[first_out + n_out + 1]
        for g in range(r.shape[0] // MOBA_BLOCK):
            mean_ref[g:g + 1, :] = jnp.mean(r[g * MOBA_BLOCK:(g + 1) * MOBA_BLOCK, :], axis=0, keepdims=True)


def _matmul_ws(a, w_stack, layer, out_dtypes, epi="none", extra=(), half=0, n_off=0, n=None, w_rows_are_n=False,
               out_scale=1.0, kv_out=None, want_means=False, tm=1024, tn=1024):
    m, kdim = a.shape
    n_tot = w_stack.shape[1] if w_rows_are_n else w_stack.shape[2]
    n = n_tot - n_off if n is None else n
    tm, tn = min(tm, m), min(tn, n)
    if epi == "rope":
        tm = min(tm, extra[0].shape[0])
    assert m % tm == 0 and n % tn == 0 and n_off + n <= n_tot, (a.shape, w_stack.shape, tm, tn)
    in_specs = [pl.BlockSpec((tm, kdim), lambda j, i: (i, 0))]
    if w_rows_are_n:
        assert n_off % 8 == 0 and n_off + (n // tn) * tn <= n_tot
        in_specs.append(pl.BlockSpec((pl.Element(1), pl.Element(tn), pl.Element(kdim)),
                                     lambda j, i: (layer, pl.multiple_of(n_off + j * tn, 8), 0)))
        w_scratch = pltpu.VMEM((tn, kdim), BF16)
    else:
        assert n_off % tn == 0
        in_specs.append(pl.BlockSpec((None, kdim, tn), lambda j, i: (layer, 0, n_off // tn + j)))
        w_scratch = pltpu.VMEM((kdim, tn), BF16)
    if epi == "rope":
        nt = extra[0].shape[0] // tm
        in_specs += [pl.BlockSpec((tm, LANES), lambda j, i: (i % nt, 0))] * 3
    elif epi == "residual":
        in_specs += [pl.BlockSpec((tm, tn), lambda j, i: (i, j))]
    out_shape = [jax.ShapeDtypeStruct((m, n), dt) for dt in out_dtypes]
    out_specs = [pl.BlockSpec((tm, tn), lambda j, i: (i, j)) for _ in out_dtypes]
    operands, aliases, n_kv = [a, w_stack, *extra], {}, 0
    if kv_out is not None:
        buf, n_layers, n_kv = kv_out
        assert n == tn == n_kv * HEAD_DIM
        if buf is not None:
            aliases = {len(operands): len(out_shape)}
            operands.append(buf)
            in_specs.append(pl.BlockSpec(memory_space=pl.ANY))
        out_shape.append(jax.ShapeDtypeStruct((n_layers * m * n_kv, HEAD_DIM), F32))
        out_specs.append(pl.BlockSpec((tm * n_kv, HEAD_DIM), lambda j, i: (layer * (m // tm) + i, 0)))
    if want_means:
        assert kv_out is not None and tm % MOBA_BLOCK == 0
        out_shape.append(jax.ShapeDtypeStruct((m // tm, tm // MOBA_BLOCK, n), F32))
        out_specs.append(pl.BlockSpec((None, tm // MOBA_BLOCK, n), lambda j, i: (i, 0, 0)))
    outs = pl.pallas_call(
        functools.partial(_mm_ws_kernel, epi=epi, n_out=len(out_dtypes), half=half, w_rows_are_n=w_rows_are_n,
                          out_scale=out_scale, n_kv=n_kv, has_buf=bool(aliases), want_means=want_means),
        out_shape=out_shape,
        grid=(n // tn, m // tm),
        in_specs=in_specs,
        out_specs=out_specs,
        scratch_shapes=[w_scratch],
        input_output_aliases=aliases,
        compiler_params=_params(("parallel", "arbitrary")),
        name="matmul_" + epi,
    )(*operands)
    return tuple(outs)


def _mm_kt_kernel(*refs, epi, nk, n_out, half):
    a_ref, w_ref = refs[0], refs[1]
    extra = refs[2:2 + _N_EXTRA[epi]]
    outs = refs[2 + _N_EXTRA[epi]:2 + _N_EXTRA[epi] + n_out]
    acc_ref = refs[-1]
    k = pl.program_id(2)
    part = jnp.dot(a_ref[...], w_ref[...].astype(BF16), preferred_element_type=F32)

    @pl.when(k == 0)
    def _():
        acc_ref[...] = part

    @pl.when(k > 0)
    def _():
        acc_ref[...] += part

    @pl.when(k == nk - 1)
    def _():
        _epilogue(acc_ref[...], epi, extra, half, outs)


def _matmul_kt(a, w_stack, layer, out_dtypes, epi="none", extra=(), tm=1024, tn=1024, tk=2048):
    m, kdim = a.shape
    n = w_stack.shape[2]
    tm, tn, tk = min(tm, m), min(tn, n), min(tk, kdim)
    assert m % tm == 0 and n % tn == 0 and kdim % tk == 0 and kdim > tk, (a.shape, w_stack.shape, tm, tn, tk)
    nk = kdim // tk
    in_specs = [pl.BlockSpec((tm, tk), lambda i, j, k: (i, k)),
                pl.BlockSpec((None, tk, tn), lambda i, j, k: (layer, k, j))]
    if epi == "residual":
        in_specs += [pl.BlockSpec((tm, tn), lambda i, j, k: (i, j))]
    outs = pl.pallas_call(
        functools.partial(_mm_kt_kernel, epi=epi, nk=nk, n_out=len(out_dtypes), half=0),
        out_shape=[jax.ShapeDtypeStruct((m, n), dt) for dt in out_dtypes],
        grid=(m // tm, n // tn, nk),
        in_specs=in_specs,
        out_specs=[pl.BlockSpec((tm, tn), lambda i, j, k: (i, j)) for _ in out_dtypes],
        scratch_shapes=[pltpu.VMEM((tm, tn), F32)],
        compiler_params=_params(("parallel", "parallel", "arbitrary")),
        name="matmul_kt_" + epi,
    )(a, w_stack, *extra)
    return tuple(outs)


def _ordered_key(score):
    bits = pltpu.bitcast(score, jnp.int32)
    return jnp.where(bits < 0, bits ^ jnp.int32(0x7FFFFFFF), bits)


def _kth_largest_key(count_ge, shape, n_sel):
    def body(b, t):
        cand = t + lax.shift_left(jnp.int32(1), jnp.int32(31) - b)
        return jnp.where(count_ge(cand) >= float(n_sel), cand, t)

    return lax.fori_loop(0, 32, body, jnp.full(shape, INT32_MIN, jnp.int32))


def _topk_lanes(g, valid, n_sel):
    lane = lax.broadcasted_iota(jnp.int32, g.shape, 1).astype(F32)
    gm = jnp.where(valid, g, NEG_INF)
    sel = jnp.zeros(g.shape, jnp.bool_)
    for _ in range(n_sel):
        mx = jnp.max(gm, axis=1, keepdims=True)
        first = jnp.min(jnp.where(gm == mx, lane, float(LANES)), axis=1, keepdims=True)
        pick = lane == first
        sel = sel | pick
        gm = jnp.where(pick, NEG_INF, gm)
    return sel & valid


def _softmax_pv(logits2, v):
    p = jnp.exp2(logits2 - jnp.max(logits2, axis=1, keepdims=True))
    l = jnp.sum(p, axis=1, keepdims=True)
    return jnp.dot(p.astype(BF16), v, preferred_element_type=F32) / l


def _width_classes(seq, tile):
    widths = sorted({w for w in (seq // 4, seq // 2, seq) if w % tile == 0 and w >= tile})
    classes, lo = [], 0
    for w in widths:
        classes.append((lo, w // tile, w))
        lo = w // tile
    return classes


def _dsa_prompt_kernel(qa_ref, ka_ref, va_ref, qi_ref, ki_ref, wi_ref, o_ref, ki_s, qcat_s, wt_s, key_s, bias_s,
                       *, tq, seq, n_sel, n_heads, n_idx_heads, groups):
    i = pl.program_id(1)
    n_valid = i + 1
    kblk = 32

    @pl.when(i == 0)
    def _():
        ki_s[...] = ki_ref[:, :IDX_DIM].astype(BF16)

    for h in range(n_idx_heads):
        qcat_s[h * tq:(h + 1) * tq, :] = qi_ref[:, h * IDX_DIM:(h + 1) * IDX_DIM]
    wt_s[...] = jnp.transpose(wi_ref[...]) * ((IDX_DIM ** -0.5) * (n_idx_heads ** -0.5))

    krow = lax.broadcasted_iota(jnp.int32, (tq, tq), 0)
    qcol = lax.broadcasted_iota(jnp.int32, (tq, tq), 1)

    def visible(c):
        return krow <= qcol + jnp.where(c == i, 0, tq)

    for c in range(seq // tq):
        @pl.when(c < n_valid)
        def _(c=c):
            d_all = lax.dot_general(ki_s[c * tq:(c + 1) * tq, :], qcat_s[...], _NT,
                                    preferred_element_type=F32)
            for kb in range(tq // kblk):
                rows = slice(kb * kblk, (kb + 1) * kblk)
                s = None
                for h in range(n_idx_heads):
                    term = jnp.maximum(d_all[rows, h * tq:(h + 1) * tq], 0.0) * wt_s[IDX_DIM + h:IDX_DIM + h + 1, :]
                    s = term if s is None else s + term
                key_s[c, rows, :] = _ordered_key(jnp.where(visible(c)[rows, :], s, NEG_INF))

    def count_ge(cand):
        def body(c, acc):
            hit = (key_s[c] >= cand).astype(F32)
            return acc + jnp.sum(hit.reshape(tq // 8, 8, tq), axis=0)

        acc = lax.fori_loop(0, n_valid, body, jnp.zeros((8, tq), F32))
        return jnp.sum(acc, axis=0, keepdims=True)

    thr = _kth_largest_key(count_ge, (1, tq), n_sel)

    def kv_cols(h):
        return slice((h // groups) * HEAD_DIM, (h // groups + 1) * HEAD_DIM)

    def attend(width):
        for c in range(width // tq):
            @pl.when(c < n_valid)
            def _(c=c):
                ok = (key_s[c] >= thr) & visible(c)
                bias_s[:, c * tq:(c + 1) * tq] = jnp.transpose(jnp.where(ok, 0.0, NEG_INF))

            @pl.when(c >= n_valid)
            def _(c=c):
                bias_s[:, c * tq:(c + 1) * tq] = jnp.full((tq, tq), NEG_INF, F32)

        def qk(h):
            return lax.dot_general(qa_ref[:, h * HEAD_DIM:(h + 1) * HEAD_DIM], ka_ref[0:width, kv_cols(h)], _NT,
                                   preferred_element_type=F32)

        raw = qk(0)
        for h in range(n_heads):
            nxt = qk(h + 1) if h + 1 < n_heads else None
            o = _softmax_pv(raw + bias_s[:, 0:width], va_ref[0:width, kv_cols(h)])
            o_ref[:, h * HEAD_DIM:(h + 1) * HEAD_DIM] = o.astype(o_ref.dtype)
            raw = nxt

    for lo, hi, width in _width_classes(seq, tq):
        @pl.when((i >= lo) & (i < hi))
        def _(width=width):
            attend(width)


def _dsa_prompt(hq, hk_bf, hv_bf, hqi, hkiwi, *, batch, seq, n_heads, n_idx_heads, groups):
    tq = min(256, seq)
    assert seq % tq == 0 and tq % LANES == 0
    nq = seq // tq
    aw = n_heads * HEAD_DIM
    kvw = aw // groups
    n_sel = min(IDX_TOPK, seq // 4)
    kern = functools.partial(_dsa_prompt_kernel, tq=tq, seq=seq, n_sel=n_sel, n_heads=n_heads,
                             n_idx_heads=n_idx_heads, groups=groups)
    return pl.pallas_call(
        kern,
        out_shape=jax.ShapeDtypeStruct((batch * seq, aw), BF16),
        grid=(batch, nq),
        in_specs=[pl.BlockSpec((tq, aw), lambda b, i: (b * nq + i, 0)),
                  pl.BlockSpec((seq, kvw), lambda b, i: (b, 0)),
                  pl.BlockSpec((seq, kvw), lambda b, i: (b, 0)),
                  pl.BlockSpec((tq, n_idx_heads * IDX_DIM), lambda b, i: (b * nq + i, 0)),
                  pl.BlockSpec((seq, LANES), lambda b, i: (b, 0)),
                  pl.BlockSpec((tq, LANES), lambda b, i: (b * nq + i, 0))],
        out_specs=pl.BlockSpec((tq, aw), lambda b, i: (b * nq + i, 0)),
        scratch_shapes=[pltpu.VMEM((seq, IDX_DIM), BF16), pltpu.VMEM((n_idx_heads * tq, IDX_DIM), BF16),
                        pltpu.VMEM((LANES, tq), F32), pltpu.VMEM((nq, tq, tq), jnp.int32),
                        pltpu.VMEM((tq, seq), F32)],
        compiler_params=_params(("parallel", "arbitrary")),
        name="dsa_prompt",
    )(hq, hk_bf, hv_bf, hqi, hkiwi, hkiwi)


def _moba_prompt_kernel(qb_ref, kb_ref, kmean_ref, vb_ref, o_ref, km_s, *, nb, n_sel, n_heads, groups):
    i = pl.program_id(1)
    blk = MOBA_BLOCK

    @pl.when(i == 0)
    def _():
        km_s[...] = jnp.zeros_like(km_s)
        per = kmean_ref.shape[1]
        for j in range(nb):
            km_s[j:j + 1, :] = kmean_ref[j // per, j % per:j % per + 1, :]

    past = lax.broadcasted_iota(jnp.int32, (blk, LANES), 1) < i
    tril_bias = jnp.where(lax.broadcasted_iota(jnp.int32, (blk, blk), 1)
                          <= lax.broadcasted_iota(jnp.int32, (blk, blk), 0), 0.0, NEG_INF)

    def kv_cols(h):
        return slice((h // groups) * HEAD_DIM, (h // groups + 1) * HEAD_DIM)

    def attend(width):
        def qk(h):
            q = qb_ref[:, h * HEAD_DIM:(h + 1) * HEAD_DIM]
            gate = lax.dot_general(q, km_s[:, kv_cols(h)].astype(BF16), _NT, preferred_element_type=F32)
            return gate, lax.dot_general(q, kb_ref[0:width, kv_cols(h)], _NT, preferred_element_type=F32)

        cur = qk(0)
        for h in range(n_heads):
            nxt = qk(h + 1) if h + 1 < n_heads else None
            gate, raw = cur
            sel_bias = jnp.where(_topk_lanes(gate, past, n_sel), 0.0, NEG_INF)
            pieces = []
            for j in range(width // blk):
                bias = jnp.where(j == i, tril_bias, sel_bias[:, j:j + 1])
                pieces.append(raw[:, j * blk:(j + 1) * blk] + bias)
            o = _softmax_pv(jnp.concatenate(pieces, axis=1), vb_ref[0:width, kv_cols(h)])
            o_ref[:, h * HEAD_DIM:(h + 1) * HEAD_DIM] = o.astype(o_ref.dtype)
            cur = nxt

    for lo, hi, width in _width_classes(nb * blk, blk):
        @pl.when((i >= lo) & (i < hi))
        def _(width=width):
            attend(width)


def _moba_prompt(hq, hk_bf, k_means, hv_bf, *, batch, seq, n_heads, groups):
    blk = MOBA_BLOCK
    assert seq % blk == 0
    nb = seq // blk
    g_per_seq = k_means.shape[0] // batch
    assert 1 <= nb <= LANES and g_per_seq * k_means.shape[1] == nb
    aw = n_heads * HEAD_DIM
    kvw = aw // groups
    kern = functools.partial(_moba_prompt_kernel, nb=nb, n_sel=min(MOBA_TOPK, nb), n_heads=n_heads,
                             groups=groups)
    return pl.pallas_call(
        kern,
        out_shape=jax.ShapeDtypeStruct((batch * seq, aw), BF16),
        grid=(batch, nb),
        in_specs=[pl.BlockSpec((blk, aw), lambda b, i: (b * nb + i, 0)),
                  pl.BlockSpec((seq, kvw), lambda b, i: (b, 0)),
                  pl.BlockSpec((g_per_seq, k_means.shape[1], kvw), lambda b, i: (b, 0, 0)),
                  pl.BlockSpec((seq, kvw), lambda b, i: (b, 0))],
        out_specs=pl.BlockSpec((blk, aw), lambda b, i: (b * nb + i, 0)),
        scratch_shapes=[pltpu.VMEM((LANES, kvw), F32)],
        compiler_params=_params(("parallel", "arbitrary")),
        name="moba_prompt",
    )(hq, hk_bf, k_means, hv_bf)


def _gmlp_kernel(cu_ref, cv_ref, g_ref, b_ref, ws_ref, bs_ref, o_ref, *vn_out, n_groups):
    r = ws_ref.shape[1]
    for ch in range(cv_ref.shape[0] // r):
        rows = slice(ch * r, (ch + 1) * r)
        v = cv_ref[rows, :]
        mu = jnp.mean(v, axis=-1, keepdims=True)
        vc = v - mu
        vn = vc * lax.rsqrt(jnp.mean(vc * vc, axis=-1, keepdims=True) + EPS) * g_ref[...] + b_ref[...]
        if vn_out:
            vn_out[0][rows, :] = vn
        for g in range(n_groups):
            cols = slice(g * C_GROUP_DIM, (g + 1) * C_GROUP_DIM)
            z = jnp.dot(ws_ref[g], vn[:, cols].astype(BF16), preferred_element_type=F32) + bs_ref[:, g:g + 1]
            o_ref[rows, cols] = (cu_ref[rows, cols] * z).astype(o_ref.dtype)


def _gmlp(hc, ln_g, ln_b, ws, bs, *, want_vn):
    m = hc.shape[0]
    n_groups, chunk = ws.shape[0], ws.shape[1]
    cw = n_groups * C_GROUP_DIM
    r = min(m, 8 * chunk)
    assert m % r == 0 and r % chunk == 0
    out_shape = [jax.ShapeDtypeStruct((m, cw), BF16)]
    out_specs = [pl.BlockSpec((r, cw), lambda i: (i, 0))]
    if want_vn:
        out_shape.append(jax.ShapeDtypeStruct((m, cw), F32))
        out_specs.append(pl.BlockSpec((r, cw), lambda i: (i, 0)))
    return pl.pallas_call(
        functools.partial(_gmlp_kernel, n_groups=n_groups),
        out_shape=out_shape,
        grid=(m // r,),
        in_specs=[pl.BlockSpec((r, cw), lambda i: (i, 0)),
                  pl.BlockSpec((r, cw), lambda i: (i, 1)),
                  pl.BlockSpec((1, cw), lambda i: (0, 0)),
                  pl.BlockSpec((1, cw), lambda i: (0, 0)),
                  pl.BlockSpec((n_groups, chunk, chunk), lambda i: (0, 0, 0)),
                  pl.BlockSpec((chunk, n_groups), lambda i: (0, 0))],
        out_specs=out_specs,
        compiler_params=_params(("parallel",)),
        name="gmlp",
    )(hc, hc, ln_g.reshape(1, cw), ln_b.reshape(1, cw), ws, bs)


def _merge_kernel(oa_ref, ob_ref, oc_ref, wb_ref, g0_ref, g1_ref, g2_ref, o_ref, wbf_s):
    @pl.when(pl.program_id(1) == 0)
    def _():
        wbf_s[...] = wb_ref[...].astype(BF16)

    acc = None
    for o_n, g_n, n in ((oa_ref, g0_ref, 0), (ob_ref, g1_ref, 1), (oc_ref, g2_ref, 2)):
        br = jnp.dot(o_n[...], wbf_s[n], preferred_element_type=F32)
        term = (1.0 / (1.0 + jnp.exp(-g_n[...]))) * br
        acc = term if acc is None else acc + term
    o_ref[...] = acc.astype(o_ref.dtype)


def _merge(o_a, o_b, o_c, wb_stack, layer, hc, *, d_model, gate_col0):
    m, aw = o_a.shape
    tm, tn = min(1024, m), 512
    nj = d_model // tn
    g0 = gate_col0 // tn
    gate_specs = [pl.BlockSpec((tm, tn), functools.partial(lambda j, i, n: (i, g0 + n * nj + j), n=n))
                  for n in range(N_BRANCH)]
    return pl.pallas_call(
        _merge_kernel,
        out_shape=jax.ShapeDtypeStruct((m, d_model), BF16),
        grid=(nj, m // tm),
        in_specs=[pl.BlockSpec((tm, aw), lambda j, i: (i, 0))] * 3
        + [pl.BlockSpec((None, N_BRANCH, aw, tn), lambda j, i: (layer, 0, 0, j))] + gate_specs,
        out_specs=pl.BlockSpec((tm, tn), lambda j, i: (i, j)),
        scratch_shapes=[pltpu.VMEM((N_BRANCH, aw, tn), BF16)],
        compiler_params=_params(("parallel", "arbitrary")),
        name="merge",
    )(o_a, o_b, o_c, wb_stack, hc, hc, hc)


def _page_specs(block, layer, n_pages, per_step):
    def page_map(b, s, pt, slot):
        return (layer, pt[b * n_pages + s * per_step + slot], 0, 0)

    return [pl.BlockSpec(block, functools.partial(page_map, slot=slot)) for slot in range(per_step)]


def _idx_scores(qi, wi, k, n_new, n_idx_heads, k_is_t=False):
    if k_is_t:
        d = jnp.dot(qi, k, preferred_element_type=F32) * (IDX_DIM ** -0.5)
    else:
        d = lax.dot_general(qi, k, _NT, preferred_element_type=F32) * (IDX_DIM ** -0.5)
    dw = jnp.maximum(d, 0.0) * wi
    acc = dw[0:8]
    for r in range(8, n_idx_heads * n_new, 8):
        acc = acc + dw[r:r + 8]
    s = acc[0:n_new]
    for r in range(n_new, 8, n_new):
        s = s + acc[r:r + n_new]
    return s * (n_idx_heads ** -0.5)


def _dsa_sample_score_kernel(pt_ref, *refs, per_step, n_new, n_idx_heads):
    pages = refs[:per_step]
    qi_ref, wi_ref, o_ref = refs[per_step:]
    for g in range(per_step):
        o_ref[:, g * LANES:(g + 1) * LANES] = _idx_scores(qi_ref[...], wi_ref[...], pages[g][...], n_new,
                                                         n_idx_heads, k_is_t=True)


def _dsa_sample_scores(pt_flat, cache_idx_t, layer, qi_rows, wi_rows, *, dec_batch, n_pages, n_new, n_idx_heads):
    page = cache_idx_t.shape[3]
    assert page == LANES and 8 % n_new == 0 and (n_idx_heads * n_new) % 8 == 0
    per_step = min(PAGES_PER_STEP, n_pages)
    assert n_pages % per_step == 0
    rows = n_idx_heads * n_new
    kern = functools.partial(_dsa_sample_score_kernel, per_step=per_step, n_new=n_new, n_idx_heads=n_idx_heads)
    grid_spec = pltpu.PrefetchScalarGridSpec(
        num_scalar_prefetch=1,
        grid=(dec_batch, n_pages // per_step),
        in_specs=_page_specs((None, None, IDX_DIM, page), layer, n_pages, per_step)
        + [pl.BlockSpec((None, rows, IDX_DIM), lambda b, s, pt: (b, 0, 0)),
           pl.BlockSpec((None, rows, 1), lambda b, s, pt: (b, 0, 0))],
        out_specs=pl.BlockSpec((None, n_new, per_step * LANES), lambda b, s, pt: (b, 0, s)),
    )
    return pl.pallas_call(
        kern,
        out_shape=jax.ShapeDtypeStruct((dec_batch, n_new, n_pages * LANES), F32),
        grid_spec=grid_spec,
        compiler_params=_params(("parallel", "arbitrary")),
        name="dsa_sample_scores",
    )(pt_flat, *([cache_idx_t] * per_step), qi_rows, wi_rows)


def _select_kernel(s_ref, qi_ref, wi_ref, kn_ref, o_ref, on_ref, key_s, keyn_s, *, n_sel, n_new, n_idx_heads, db):
    rows = db * n_new
    t = lax.broadcasted_iota(jnp.int32, (rows, LANES), 0) & (n_new - 1)
    c = lax.broadcasted_iota(jnp.int32, (rows, LANES), 1)
    new_ok = c <= t
    key_s[...] = _ordered_key(s_ref[...])
    for b in range(db):
        s_new = _idx_scores(qi_ref[b], wi_ref[b], kn_ref[b], n_new, n_idx_heads)
        keyn_s[b * n_new:(b + 1) * n_new, :] = _ordered_key(jnp.where(new_ok[0:n_new], s_new, NEG_INF))

    def count_ge(cand):
        return (jnp.sum((key_s[...] >= cand).astype(F32), axis=1, keepdims=True)
                + jnp.sum((keyn_s[...] >= cand).astype(F32), axis=1, keepdims=True))

    thr = _kth_largest_key(count_ge, (rows, 1), n_sel)
    o_ref[...] = (key_s[...] >= thr).astype(F32)
    on_ref[...] = ((keyn_s[...] >= thr) & new_ok).astype(F32)


def _dsa_sample_select(scores, qi_rows, wi_rows, ki_new_pad, n_sel, *, n_idx_heads):
    db, n_new, length = scores.shape
    rows = db * n_new
    assert n_new & (n_new - 1) == 0
    whole = pl.BlockSpec(memory_space=pltpu.VMEM)
    sel, sel_new = pl.pallas_call(
        functools.partial(_select_kernel, n_sel=n_sel, n_new=n_new, n_idx_heads=n_idx_heads, db=db),
        out_shape=[jax.ShapeDtypeStruct((rows, length), F32), jax.ShapeDtypeStruct((rows, LANES), F32)],
        in_specs=[whole] * 4,
        out_specs=[whole] * 2,
        scratch_shapes=[pltpu.VMEM((rows, length), jnp.int32), pltpu.VMEM((rows, LANES), jnp.int32)],
        compiler_params=pltpu.CompilerParams(vmem_limit_bytes=VMEM_LIMIT),
        name="dsa_sample_select",
    )(scores.reshape(rows, length), qi_rows, wi_rows, ki_new_pad)
    return sel.reshape(db, n_new, length), sel_new.reshape(db, n_new, LANES)


def _head_rows(page_refs, kv, n_kv):
    parts = [r[pl.ds(kv, LANES, stride=n_kv), :] for r in page_refs]
    return parts[0] if len(parts) == 1 else jnp.concatenate(parts, axis=0)


def _dsa_sample_attn_kernel(pt_ref, *refs, per_step, n_steps, n_new, n_kv, groups):
    k_pages, v_pages = refs[:per_step], refs[per_step:2 * per_step]
    kn_ref, vn_ref, q_ref, sel_ref, seln_ref, o_ref, m_s, l_s, acc_s = refs[2 * per_step:]
    step = pl.program_id(1)
    rpk = groups * n_new

    @pl.when(step == 0)
    def _():
        m_s[...] = jnp.full_like(m_s, NEG_INF)
        l_s[...] = jnp.zeros_like(l_s)
        acc_s[...] = jnp.zeros_like(acc_s)

    def update(get_k, get_v, sel):
        mask = jnp.concatenate([sel] * groups, axis=0) > 0.0
        rows = [slice(kv * rpk, (kv + 1) * rpk) for kv in range(n_kv)]
        logits = [lax.dot_general(q_ref[rows[kv], :], get_k(kv), _NT, preferred_element_type=F32)
                  for kv in range(n_kv)]
        probs, alphas = [], []
        for kv in range(n_kv):
            s = jnp.where(mask, logits[kv], NEG_INF)
            m_old = m_s[rows[kv], :]
            m_new = jnp.maximum(m_old, jnp.max(s, axis=1, keepdims=True))
            m_safe = jnp.where(m_new == NEG_INF, 0.0, m_new)
            alpha = jnp.exp2(m_old - m_safe)
            pr = jnp.exp2(s - m_safe)
            m_s[rows[kv], :] = m_new
            l_s[rows[kv], :] = alpha * l_s[rows[kv], :] + jnp.sum(pr, axis=1, keepdims=True)
            probs.append(pr)
            alphas.append(alpha)
        for kv in range(n_kv):
            acc_s[rows[kv], :] = alphas[kv] * acc_s[rows[kv], :] + jnp.dot(
                probs[kv], get_v(kv), preferred_element_type=F32)

    update(lambda kv: _head_rows(k_pages, kv, n_kv), lambda kv: _head_rows(v_pages, kv, n_kv), sel_ref[...])

    @pl.when(step == n_steps - 1)
    def _():
        update(lambda kv: kn_ref[kv], lambda kv: vn_ref[kv], seln_ref[...])
        o_ref[...] = acc_s[...] / l_s[...]


def _dsa_sample_attn(pt_flat, cache_k, cache_v, layer, k_new_pad, v_new_pad, q_rows, sel, sel_new, *, dec_batch,
                     n_pages, n_new, n_kv, groups):
    rows = n_kv * groups * n_new
    prow = cache_k.shape[2]
    per_step = min(PAGES_PER_STEP, n_pages)
    assert n_pages % per_step == 0
    n_steps = n_pages // per_step
    kern = functools.partial(_dsa_sample_attn_kernel, per_step=per_step, n_steps=n_steps, n_new=n_new, n_kv=n_kv,
                             groups=groups)
    page_specs = _page_specs((None, None, prow, HEAD_DIM), layer, n_pages, per_step)
    grid_spec = pltpu.PrefetchScalarGridSpec(
        num_scalar_prefetch=1,
        grid=(dec_batch, n_steps),
        in_specs=page_specs + page_specs
        + [pl.BlockSpec((None, n_kv, LANES, HEAD_DIM), lambda b, s, pt: (b, 0, 0, 0)),
           pl.BlockSpec((None, n_kv, LANES, HEAD_DIM), lambda b, s, pt: (b, 0, 0, 0)),
           pl.BlockSpec((None, rows, HEAD_DIM), lambda b, s, pt: (b, 0, 0)),
           pl.BlockSpec((None, n_new, per_step * LANES), lambda b, s, pt: (b, 0, s)),
           pl.BlockSpec((None, n_new, LANES), lambda b, s, pt: (b, 0, 0))],
        out_specs=pl.BlockSpec((None, rows, HEAD_DIM), lambda b, s, pt: (b, 0, 0)),
        scratch_shapes=[pltpu.VMEM((rows, 1), F32), pltpu.VMEM((rows, 1), F32),
                        pltpu.VMEM((rows, HEAD_DIM), F32)],
    )
    return pl.pallas_call(
        kern,
        out_shape=jax.ShapeDtypeStruct((dec_batch, rows, HEAD_DIM), F32),
        grid_spec=grid_spec,
        compiler_params=_params(("parallel", "arbitrary")),
        name="dsa_sample_attn",
    )(pt_flat, *([cache_k] * per_step), *([cache_v] * per_step), k_new_pad, v_new_pad, q_rows, sel, sel_new)


def _moba_sample_kernel(pt_ref, *refs, bps, n_steps, nb, n_sel, n_new, n_kv, groups, past_len):
    k_pages, v_pages = refs[:2 * bps], refs[2 * bps:4 * bps]
    kn_ref, vn_ref, q_ref, o_ref, gate_s, m_s, l_s, acc_s = refs[4 * bps:]
    step = pl.program_id(1)
    rpk = groups * n_new
    rows_all = n_kv * rpk
    lane = lax.broadcasted_iota(jnp.int32, (rows_all, LANES), 1)

    @pl.when(step == 0)
    def _():
        gate_s[...] = jnp.zeros_like(gate_s)
        m_s[...] = jnp.zeros_like(m_s)
        l_s[...] = jnp.zeros_like(l_s)

    gate, m_blk, l_blk = gate_s[...], m_s[...], l_s[...]
    blk = MOBA_BLOCK
    kv_rows = [slice(kv * rpk, (kv + 1) * rpk) for kv in range(n_kv)]
    ks = [_head_rows(k_pages, kv, n_kv) for kv in range(n_kv)]
    logits = [lax.dot_general(q_ref[kv_rows[kv], :], ks[kv], _NT, preferred_element_type=F32)
              for kv in range(n_kv)]
    probs = {}
    for bb in range(bps):
        g_cols, m_cols, l_cols = [], [], []
        for kv in range(n_kv):
            kmean = jnp.mean(ks[kv][bb * blk:(bb + 1) * blk, :], axis=0, keepdims=True)
            g_cols.append(jnp.sum(q_ref[kv_rows[kv], :] * kmean, axis=1, keepdims=True))
            s = logits[kv][:, bb * blk:(bb + 1) * blk]
            m = jnp.max(s, axis=1, keepdims=True)
            pr = jnp.exp2(s - m)
            m_cols.append(m)
            l_cols.append(jnp.sum(pr, axis=1, keepdims=True))
            probs[bb, kv] = pr
        here = lane == step * bps + bb
        gate = jnp.where(here, jnp.concatenate(g_cols, axis=0), gate)
        m_blk = jnp.where(here, jnp.concatenate(m_cols, axis=0), m_blk)
        l_blk = jnp.where(here, jnp.concatenate(l_cols, axis=0), l_blk)
    for bb in range(bps):
        for kv in range(n_kv):
            v = _head_rows(v_pages[2 * bb:2 * bb + 2], kv, n_kv)
            acc_s[step * bps + bb, kv_rows[kv], :] = jnp.dot(probs[bb, kv], v, preferred_element_type=F32)
    gate_s[...] = gate
    m_s[...] = m_blk
    l_s[...] = l_blk

    @pl.when(step == n_steps - 1)
    def _():
        t_row = lax.broadcasted_iota(jnp.int32, (rpk, LANES), 0) & (n_new - 1)
        c_col = lax.broadcasted_iota(jnp.int32, (rpk, LANES), 1)
        own_ok = c_col <= t_row
        t_all = lax.broadcasted_iota(jnp.int32, (rows_all, LANES), 0) & (n_new - 1)
        own_blk = jnp.right_shift(past_len + t_all, MOBA_BLOCK.bit_length() - 1)
        valid = (lane < own_blk) & (lane < nb)
        sel = _topk_lanes(gate, valid, n_sel)
        for kv in range(n_kv):
            rows = slice(kv * rpk, (kv + 1) * rpk)
            q = q_ref[rows, :]
            s = lax.dot_general(q, kn_ref[kv], _NT, preferred_element_type=F32)
            s = jnp.where(own_ok, s, NEG_INF)
            m_o = jnp.max(s, axis=1, keepdims=True)
            m_sel = jnp.where(sel[rows, :], m_blk[rows, :], NEG_INF)
            m_all = jnp.maximum(m_o, jnp.max(m_sel, axis=1, keepdims=True))
            pr = jnp.exp2(s - m_all)
            w = jnp.where(sel[rows, :], jnp.exp2(m_sel - m_all), 0.0)
            den = jnp.sum(pr, axis=1, keepdims=True) + jnp.sum(w * l_blk[rows, :], axis=1, keepdims=True)
            num = jnp.dot(pr, vn_ref[kv], preferred_element_type=F32)
            for jb in range(nb):
                num = num + w[:, jb:jb + 1] * acc_s[jb, rows, :]
            o_ref[rows, :] = num / den


def _moba_sample(pt_flat, cache_k, cache_v, layer, k_new_pad, v_new_pad, q_rows, *, dec_batch, n_pages, n_new,
                 n_kv, groups, past_len):
    assert past_len % MOBA_BLOCK == 0 and n_new < MOBA_BLOCK and MOBA_BLOCK == 2 * LANES
    assert n_new & (n_new - 1) == 0
    nb = (past_len + n_new) // MOBA_BLOCK
    assert 1 <= nb <= LANES and n_pages == 2 * nb
    bps = min(PAGES_PER_STEP // 2, nb)
    assert nb % bps == 0
    n_steps = nb // bps
    rows = n_kv * groups * n_new
    prow = cache_k.shape[2]
    kern = functools.partial(_moba_sample_kernel, bps=bps, n_steps=n_steps, nb=nb, n_sel=min(MOBA_TOPK, nb),
                             n_new=n_new, n_kv=n_kv, groups=groups, past_len=past_len)
    page_specs = _page_specs((None, None, prow, HEAD_DIM), layer, n_pages, 2 * bps)
    grid_spec = pltpu.PrefetchScalarGridSpec(
        num_scalar_prefetch=1,
        grid=(dec_batch, n_steps),
        in_specs=page_specs + page_specs
        + [pl.BlockSpec((None, n_kv, LANES, HEAD_DIM), lambda b, s, pt: (b, 0, 0, 0)),
           pl.BlockSpec((None, n_kv, LANES, HEAD_DIM), lambda b, s, pt: (b, 0, 0, 0)),
           pl.BlockSpec((None, rows, HEAD_DIM), lambda b, s, pt: (b, 0, 0))],
        out_specs=pl.BlockSpec((None, rows, HEAD_DIM), lambda b, s, pt: (b, 0, 0)),
        scratch_shapes=[pltpu.VMEM((rows, LANES), F32), pltpu.VMEM((rows, LANES), F32),
                        pltpu.VMEM((rows, LANES), F32), pltpu.VMEM((nb, rows, HEAD_DIM), F32)],
    )
    return pl.pallas_call(
        kern,
        out_shape=jax.ShapeDtypeStruct((dec_batch, rows, HEAD_DIM), F32),
        grid_spec=grid_spec,
        compiler_params=_params(("parallel", "arbitrary")),
        name="moba_sample",
    )(pt_flat, *([cache_k] * (2 * bps)), *([cache_v] * (2 * bps)), k_new_pad, v_new_pad, q_rows)


def _rope_tables(pos, rot, width, n_rope_lanes=LANES):
    half = rot // 2
    inv = jnp.power(ROPE_THETA, -jnp.arange(half, dtype=F32) / half)
    ang = pos.astype(F32)[:, None] * inv[None, :]
    cos, sin = jnp.cos(ang), jnp.sin(ang)
    t = pos.shape[0]
    zeros = jnp.zeros((t, half), F32)
    rest0 = jnp.zeros((t, width - rot), F32)
    c = jnp.concatenate([cos, cos, jnp.ones((t, width - rot), F32)], axis=1)
    sa = jnp.concatenate([-sin, zeros, rest0], axis=1)
    sb = jnp.concatenate([zeros, sin, rest0], axis=1)
    reps = n_rope_lanes // width
    pad = LANES - n_rope_lanes
    c = jnp.concatenate([jnp.tile(c, (1, reps)), jnp.ones((t, pad), F32)], axis=1)
    sa = jnp.concatenate([jnp.tile(sa, (1, reps)), jnp.zeros((t, pad), F32)], axis=1)
    sb = jnp.concatenate([jnp.tile(sb, (1, reps)), jnp.zeros((t, pad), F32)], axis=1)
    return c, sa, sb


def _kv_major_rows(a, db, n_new, n_kv, groups):
    a = a.reshape(db, n_new, n_kv, groups, HEAD_DIM)
    return jnp.transpose(a, (0, 2, 3, 1, 4)).reshape(db, n_kv * groups * n_new, HEAD_DIM)


def _from_kv_major_rows(o, db, n_new, n_kv, groups):
    o = o.reshape(db, n_kv, groups, n_new, HEAD_DIM)
    return jnp.transpose(o, (0, 3, 1, 2, 4)).reshape(db * n_new, n_kv * groups * HEAD_DIM)


def _new_kv_pad(k, db, n_new, n_kv):
    k = jnp.transpose(k.reshape(db, n_new, n_kv, HEAD_DIM), (0, 2, 1, 3))
    return jnp.pad(k, ((0, 0), (0, 0), (0, LANES - n_new), (0, 0)))


def kernel(x_prompt, x_sample, cache_a_k, cache_a_v, cache_a_idx, cache_b_k, cache_b_v, page_table, w_in, norm1_g,
           c_ln_g, c_ln_b, w_spatial, b_spatial, w_branch, w_out, norm2_g, w_ff1, w_ff2, norm_f_g):
    batch, seq, d_model = x_prompt.shape
    db, n_new, _ = x_sample.shape
    depth, n_phys, page, n_kv, _ = cache_a_k.shape
    n_pages = page_table.shape[1]
    past_len = n_pages * page
    n_heads = d_model // 256
    groups = n_heads // n_kv
    aw = n_heads * HEAD_DIM
    kvw = n_kv * HEAD_DIM
    n_idx_heads = d_model // 128
    iw = n_idx_heads * IDX_DIM
    n_cg = w_spatial.shape[1]
    cw = n_cg * C_GROUP_DIM
    rot, idx_rot = HEAD_DIM // 4, IDX_DIM // 4
    assert page == LANES and aw == 2 * kvw and cw % 512 == 0 and seq % CHUNK == 0 and n_new <= CHUNK

    sizes = (aw, kvw, kvw, iw, IDX_DIM, n_idx_heads, aw, kvw, kvw, cw, cw, N_BRANCH * d_model)
    offs = [0]
    for s in sizes:
        offs.append(offs[-1] + s)
    (o_qa, o_ka, o_va, o_qi, o_ki, o_wi, o_qb, o_kb, o_vb, o_cu, o_cv, o_gl, _) = offs

    pos_p = jnp.arange(seq, dtype=jnp.int32)
    pos_s = jnp.tile(past_len + jnp.arange(n_new, dtype=jnp.int32), db)
    tabs = {
        "p": (_rope_tables(pos_p, rot, HEAD_DIM), _rope_tables(pos_p, idx_rot, IDX_DIM),
              _rope_tables(pos_p, idx_rot, IDX_DIM, IDX_DIM)),
        "s": (_rope_tables(pos_s, rot, HEAD_DIM), _rope_tables(pos_s, idx_rot, IDX_DIM),
              _rope_tables(pos_s, idx_rot, IDX_DIM, IDX_DIM)),
    }

    cache_ak = cache_a_k.reshape(depth, n_phys, page * n_kv, HEAD_DIM)
    cache_av = cache_a_v.reshape(depth, n_phys, page * n_kv, HEAD_DIM)
    cache_bk = cache_b_k.reshape(depth, n_phys, page * n_kv, HEAD_DIM)
    cache_bv = cache_b_v.reshape(depth, n_phys, page * n_kv, HEAD_DIM)
    pt_flat = page_table.reshape(-1).astype(jnp.int32)

    xp = x_prompt.reshape(batch * seq, d_model)
    xs = x_sample.reshape(db * n_new, d_model)
    rs = db * n_new
    tri = jnp.tril(jnp.ones((CHUNK, CHUNK), dtype=bool))
    sel_s = min(IDX_TOPK, (past_len + n_new) // 4)

    outs = {k: [] for k in ("pa_k", "pa_v", "pa_i", "pb_k", "pb_v", "sa_k", "sa_v", "sa_i", "sb_k", "sb_v", "sc_v")}

    w_in_t = jnp.swapaxes(w_in, 1, 2)
    cache_idx_t = jnp.swapaxes(cache_a_idx, 2, 3)

    kv_bufs = {(which, name): None for which in "ps" for name in ("ka", "kb", "va", "vb")}

    def project(x, l, which):
        t128, t64, tkiwi = tabs[which]
        xn = _rms_norm(x, norm1_g[l], BF16)

        def proj(n_off, n, out_dtypes, epi="none", extra=(), half=0, out_scale=1.0, **kw):
            return _matmul_ws(xn, w_in_t, l, out_dtypes, epi, extra, half, n_off=n_off, n=n, w_rows_are_n=True,
                              out_scale=out_scale, **kw)

        def kv_proj(name, n_off, epi="none", extra=(), half=0, want_means=False):
            res = proj(n_off, kvw, (BF16,), epi, extra, half, kv_out=(kv_bufs[which, name], depth, n_kv),
                       want_means=want_means)
            kv_bufs[which, name] = res[1]
            return (res[0], res[2]) if want_means else res[0]

        (hqa,) = proj(o_qa, aw, (BF16,), "rope", t128, rot // 2, Q_SCALE)
        (hqb,) = proj(o_qb, aw, (BF16,), "rope", t128, rot // 2, Q_SCALE)
        hka_bf = kv_proj("ka", o_ka, "rope", t128, rot // 2)
        hkb_bf = kv_proj("kb", o_kb, "rope", t128, rot // 2, want_means=(which == "p"))
        hva_bf = kv_proj("va", o_va)
        hvb_bf = kv_proj("vb", o_vb)
        (hqi,) = proj(o_qi, iw, (BF16,), "rope", t64, idx_rot // 2)
        (hkiwi,) = proj(o_ki, LANES, (F32,), "rope", tkiwi, idx_rot // 2)
        (hc,) = proj(o_cu, 2 * cw + N_BRANCH * d_model, (F32,))
        return hqa, hqb, hka_bf, hkb_bf, hva_bf, hvb_bf, hqi, hkiwi, hc

    def new_rows(name, l):
        return kv_bufs["s", name].reshape(depth, rs, kvw)[l]

    def merge_ffn(x, o_a, o_b, o_c, hc, l):
        merged = _merge(o_a, o_b, o_c, w_branch, l, hc, d_model=d_model, gate_col0=2 * cw)
        (x1,) = _matmul_ws(merged, w_out, l, (F32,), "residual", (x,))
        hn = _rms_norm(x1, norm2_g[l], BF16)
        (h1,) = _matmul_ws(hn, w_ff1, l, (BF16,), "relu2")
        (x2,) = _matmul_kt(h1, w_ff2, l, (F32,), "residual", (x1,))
        return x2

    for l in range(depth):
        hqa, hqb, hka_bf, (hkb_bf, kb_means), hva_bf, hvb_bf, hqi, hkiwi, hc = project(xp, l, "p")
        o_a = _dsa_prompt(hqa, hka_bf, hva_bf, hqi, hkiwi, batch=batch, seq=seq, n_heads=n_heads,
                          n_idx_heads=n_idx_heads, groups=groups)
        o_b = _moba_prompt(hqb, hkb_bf, kb_means, hvb_bf, batch=batch, seq=seq, n_heads=n_heads, groups=groups)
        ws_p = jnp.where(tri[None], w_spatial[l], 0).astype(BF16)
        bs_p = b_spatial[l].T
        (o_c,) = _gmlp(hc, c_ln_g[l], c_ln_b[l], ws_p, bs_p, want_vn=False)
        xp = merge_ffn(xp, o_a, o_b, o_c, hc, l)
        outs["pa_i"].append(hkiwi[:, :IDX_DIM].reshape(batch, seq, IDX_DIM))

        hqa, hqb, hka_bf, hkb_bf, hva_bf, hvb_bf, hqi, hkiwi, hc = project(xs, l, "s")
        qi_rows = jnp.transpose(hqi.astype(F32).reshape(db, n_new, n_idx_heads, IDX_DIM), (0, 2, 1, 3)).reshape(
            db, n_idx_heads * n_new, IDX_DIM)
        wi_rows = jnp.transpose(hkiwi[:, IDX_DIM:IDX_DIM + n_idx_heads].reshape(db, n_new, n_idx_heads),
                                (0, 2, 1)).reshape(db, n_idx_heads * n_new, 1)
        ki_new = jnp.pad(hkiwi[:, :IDX_DIM].reshape(db, n_new, IDX_DIM), ((0, 0), (0, LANES - n_new), (0, 0)))
        scores = _dsa_sample_scores(pt_flat, cache_idx_t, l, qi_rows, wi_rows, dec_batch=db, n_pages=n_pages,
                                    n_new=n_new, n_idx_heads=n_idx_heads)
        sel, sel_new = _dsa_sample_select(scores, qi_rows, wi_rows, ki_new, sel_s, n_idx_heads=n_idx_heads)
        qa_rows = _kv_major_rows(hqa.astype(F32), db, n_new, n_kv, groups)
        o_a = _dsa_sample_attn(pt_flat, cache_ak, cache_av, l, _new_kv_pad(new_rows("ka", l), db, n_new, n_kv),
                               _new_kv_pad(new_rows("va", l), db, n_new, n_kv), qa_rows, sel, sel_new, dec_batch=db,
                               n_pages=n_pages, n_new=n_new, n_kv=n_kv, groups=groups)
        o_a = _from_kv_major_rows(o_a, db, n_new, n_kv, groups).astype(BF16)
        qb_rows = _kv_major_rows(hqb.astype(F32), db, n_new, n_kv, groups)
        o_b = _moba_sample(pt_flat, cache_bk, cache_bv, l, _new_kv_pad(new_rows("kb", l), db, n_new, n_kv),
                           _new_kv_pad(new_rows("vb", l), db, n_new, n_kv), qb_rows, dec_batch=db, n_pages=n_pages,
                           n_new=n_new, n_kv=n_kv, groups=groups, past_len=past_len)
        o_b = _from_kv_major_rows(o_b, db, n_new, n_kv, groups).astype(BF16)
        ws_c = jnp.where(tri[None, :n_new, :n_new], w_spatial[l][:, :n_new, :n_new], 0)
        ws_s = jnp.einsum("ab,gij->gaibj", jnp.eye(db, dtype=F32), ws_c).reshape(n_cg, rs, rs).astype(BF16)
        bs_s = jnp.tile(b_spatial[l][:, :n_new].T, (db, 1))
        o_c, vn = _gmlp(hc, c_ln_g[l], c_ln_b[l], ws_s, bs_s, want_vn=True)
        xs = merge_ffn(xs, o_a, o_b, o_c, hc, l)
        outs["sa_i"].append(hkiwi[:, :IDX_DIM].reshape(db, n_new, IDX_DIM))
        outs["sc_v"].append(vn.reshape(db, n_new, cw))

    y_prompt = _rms_norm(xp, norm_f_g, F32).reshape(batch, seq, d_model)
    y_sample = _rms_norm(xs, norm_f_g, F32).reshape(db, n_new, d_model)

    def kv_result(which, name):
        lead = (depth, batch, seq) if which == "p" else (depth, db, n_new)
        return kv_bufs[which, name].reshape(lead + (n_kv, HEAD_DIM))

    return (y_prompt, y_sample, kv_result("p", "ka"), kv_result("p", "va"), jnp.stack(outs["pa_i"]),
            kv_result("p", "kb"), kv_result("p", "vb"), kv_result("s", "ka"), kv_result("s", "va"),
            jnp.stack(outs["sa_i"]), kv_result("s", "kb"), kv_result("s", "vb"), jnp.stack(outs["sc_v"]))
```

```python
import functools
import math

import jax
import jax.numpy as jnp
from jax import lax
from jax.experimental import pallas as pl
from jax.experimental.pallas import tpu as pltpu

HEAD_DIM = 128
ROPE_THETA = 500000.0
IDX_DIM = 64
IDX_TOPK = 256
MOBA_BLOCK = 256
MOBA_TOPK = 3
CHUNK = 128
C_GROUP_DIM = 128
N_BRANCH = 3
EPS = 1e-6
LANES = 128
INT32_MIN = -(2 ** 31)
PAGES_PER_STEP = 16
Q_SCALE = HEAD_DIM ** -0.5 * math.log2(math.e)

F32 = jnp.float32
BF16 = jnp.bfloat16
NEG_INF = float("-inf")
VMEM_LIMIT = 56 * 1024 * 1024

_NT = (((1,), (1,)), ((), ()))


def _params(sem):
    return pltpu.CompilerParams(dimension_semantics=sem, vmem_limit_bytes=VMEM_LIMIT)


def _rms_kernel(x_ref, g_ref, o_ref):
    x = x_ref[...]
    y = x * lax.rsqrt(jnp.mean(x * x, axis=-1, keepdims=True) + EPS)
    o_ref[...] = (y * g_ref[...]).astype(o_ref.dtype)


def _rms_norm(x, g, out_dtype):
    m, d = x.shape
    tm = min(1024, m)
    return pl.pallas_call(
        _rms_kernel,
        out_shape=jax.ShapeDtypeStruct((m, d), out_dtype),
        grid=(m // tm,),
        in_specs=[pl.BlockSpec((tm, d), lambda i: (i, 0)),
                  pl.BlockSpec((1, d), lambda i: (0, 0))],
        out_specs=pl.BlockSpec((tm, d), lambda i: (i, 0)),
        compiler_params=_params(("parallel",)),
        name="rms_norm",
    )(x, g.reshape(1, d))


def _rope_epilogue(r, c, sa, sb, half):
    outs = []
    for s in range(r.shape[1] // LANES):
        a = r[:, s * LANES:(s + 1) * LANES]
        outs.append(a * c + pltpu.roll(a, LANES - half, 1) * sa + pltpu.roll(a, half, 1) * sb)
    return outs[0] if len(outs) == 1 else jnp.concatenate(outs, axis=1)


_N_EXTRA = {"none": 0, "relu2": 0, "rope": 3, "residual": 1}


def _epilogue(r, epi, extra, half, outs, out_scale=1.0):
    if epi == "rope":
        r = _rope_epilogue(r, extra[0][...], extra[1][...], extra[2][...], half)
        if out_scale != 1.0:
            r = r * out_scale
    elif epi == "relu2":
        r = jnp.square(jnp.maximum(r, 0.0))
    elif epi == "residual":
        r = extra[0][...] + r
    for o_ref in outs:
        o_ref[...] = r.astype(o_ref.dtype)
    return r


def _mm_ws_kernel(*refs, epi, n_out, half, w_rows_are_n, out_scale, n_kv, has_buf, want_means):
    a_ref, w_ref = refs[0], refs[1]
    extra = refs[2:2 + _N_EXTRA[epi]]
    first_out = 2 + _N_EXTRA[epi] + (1 if has_buf else 0)
    outs = refs[first_out:first_out + n_out]
    wbf_s = refs[-1]

    @pl.when(pl.program_id(1) == 0)
    def _():
        wbf_s[...] = (w_ref[0] if w_rows_are_n else w_ref[...]).astype(BF16)

    if w_rows_are_n:
        part = lax.dot_general(a_ref[...], wbf_s[...], _NT, preferred_element_type=F32)
    else:
        part = jnp.dot(a_ref[...], wbf_s[...], preferred_element_type=F32)
    r = _epilogue(part, epi, extra, half, outs, out_scale)
    if n_kv:
        kv_ref = refs[first_out + n_out]
        tm = r.shape[0]
        for kv in range(n_kv):
            kv_ref[pl.ds(kv, tm, stride=n_kv), :] = r[:, kv * HEAD_DIM:(kv + 1) * HEAD_DIM]
    if want_means:
        mean_ref = refs[first_out + n_out + 1]
        for g in range(r.shape[0] // MOBA_BLOCK):
            mean_ref[g:g + 1, :] = jnp.mean(r[g * MOBA_BLOCK:(g + 1) * MOBA_BLOCK, :], axis=0, keepdims=True)


def _matmul_ws(a, w_stack, layer, out_dtypes, epi="none", extra=(), half=0, n_off=0, n=None, w_rows_are_n=False,
               out_scale=1.0, kv_out=None, want_means=False, tm=1024, tn=1024):
    m, kdim = a.shape
    n_tot = w_stack.shape[1] if w_rows_are_n else w_stack.shape[2]
    n = n_tot - n_off if n is None else n
    tm, tn = min(tm, m), min(tn, n)
    if epi == "rope":
        tm = min(tm, extra[0].shape[0])
    assert m % tm == 0 and n % tn == 0 and n_off + n <= n_tot, (a.shape, w_stack.shape, tm, tn)
    in_specs = [pl.BlockSpec((tm, kdim), lambda j, i: (i, 0))]
    if w_rows_are_n:
        assert n_off % 8 == 0 and n_off + (n // tn) * tn <= n_tot
        in_specs.append(pl.BlockSpec((pl.Element(1), pl.Element(tn), pl.Element(kdim)),
                                     lambda j, i: (layer, pl.multiple_of(n_off + j * tn, 8), 0)))
        w_scratch = pltpu.VMEM((tn, kdim), BF16)
    else:
        assert n_off % tn == 0
        in_specs.append(pl.BlockSpec((None, kdim, tn), lambda j, i: (layer, 0, n_off // tn + j)))
        w_scratch = pltpu.VMEM((kdim, tn), BF16)
    if epi == "rope":
        nt = extra[0].shape[0] // tm
        in_specs += [pl.BlockSpec((tm, LANES), lambda j, i: (i % nt, 0))] * 3
    elif epi == "residual":
        in_specs += [pl.BlockSpec((tm, tn), lambda j, i: (i, j))]
    out_shape = [jax.ShapeDtypeStruct((m, n), dt) for dt in out_dtypes]
    out_specs = [pl.BlockSpec((tm, tn), lambda j, i: (i, j)) for _ in out_dtypes]
    operands, aliases, n_kv = [a, w_stack, *extra], {}, 0
    if kv_out is not None:
        buf, n_layers, n_kv = kv_out
        assert n == tn == n_kv * HEAD_DIM
        if buf is not None:
            aliases = {len(operands): len(out_shape)}
            operands.append(buf)
            in_specs.append(pl.BlockSpec(memory_space=pl.ANY))
        out_shape.append(jax.ShapeDtypeStruct((n_layers * m * n_kv, HEAD_DIM), F32))
        out_specs.append(pl.BlockSpec((tm * n_kv, HEAD_DIM), lambda j, i: (layer * (m // tm) + i, 0)))
    if want_means:
        assert kv_out is not None and tm % MOBA_BLOCK == 0
        out_shape.append(jax.ShapeDtypeStruct((m // tm, tm // MOBA_BLOCK, n), F32))
        out_specs.append(pl.BlockSpec((None, tm // MOBA_BLOCK, n), lambda j, i: (i, 0, 0)))
    outs = pl.pallas_call(
        functools.partial(_mm_ws_kernel, epi=epi, n_out=len(out_dtypes), half=half, w_rows_are_n=w_rows_are_n,
                          out_scale=out_scale, n_kv=n_kv, has_buf=bool(aliases), want_means=want_means),
        out_shape=out_shape,
        grid=(n // tn, m // tm),
        in_specs=in_specs,
        out_specs=out_specs,
        scratch_shapes=[w_scratch],
        input_output_aliases=aliases,
        compiler_params=_params(("parallel", "arbitrary")),
        name="matmul_" + epi,
    )(*operands)
    return tuple(outs)


def _mm_kt_kernel(*refs, epi, nk, n_out, half):
    a_ref, w_ref = refs[0], refs[1]
    extra = refs[2:2 + _N_EXTRA[epi]]
    outs = refs[2 + _N_EXTRA[epi]:2 + _N_EXTRA[epi] + n_out]
    acc_ref = refs[-1]
    k = pl.program_id(2)
    part = jnp.dot(a_ref[...], w_ref[...].astype(BF16), preferred_element_type=F32)

    @pl.when(k == 0)
    def _():
        acc_ref[...] = part

    @pl.when(k > 0)
    def _():
        acc_ref[...] += part

    @pl.when(k == nk - 1)
    def _():
        _epilogue(acc_ref[...], epi, extra, half, outs)


def _matmul_kt(a, w_stack, layer, out_dtypes, epi="none", extra=(), tm=1024, tn=1024, tk=2048):
    m, kdim = a.shape
    n = w_stack.shape[2]
    tm, tn, tk = min(tm, m), min(tn, n), min(tk, kdim)
    assert m % tm == 0 and n % tn == 0 and kdim % tk == 0 and kdim > tk, (a.shape, w_stack.shape, tm, tn, tk)
    nk = kdim // tk
    in_specs = [pl.BlockSpec((tm, tk), lambda i, j, k: (i, k)),
                pl.BlockSpec((None, tk, tn), lambda i, j, k: (layer, k, j))]
    if epi == "residual":
        in_specs += [pl.BlockSpec((tm, tn), lambda i, j, k: (i, j))]
    outs = pl.pallas_call(
        functools.partial(_mm_kt_kernel, epi=epi, nk=nk, n_out=len(out_dtypes), half=0),
        out_shape=[jax.ShapeDtypeStruct((m, n), dt) for dt in out_dtypes],
        grid=(m // tm, n // tn, nk),
        in_specs=in_specs,
        out_specs=[pl.BlockSpec((tm, tn), lambda i, j, k: (i, j)) for _ in out_dtypes],
        scratch_shapes=[pltpu.VMEM((tm, tn), F32)],
        compiler_params=_params(("parallel", "parallel", "arbitrary")),
        name="matmul_kt_" + epi,
    )(a, w_stack, *extra)
    return tuple(outs)


def _ordered_key(score):
    bits = pltpu.bitcast(score, jnp.int32)
    return jnp.where(bits < 0, bits ^ jnp.int32(0x7FFFFFFF), bits)


def _kth_largest_key(count_ge, shape, n_sel):
    def body(b, t):
        cand = t + lax.shift_left(jnp.int32(1), jnp.int32(31) - b)
        return jnp.where(count_ge(cand) >= float(n_sel), cand, t)

    return lax.fori_loop(0, 32, body, jnp.full(shape, INT32_MIN, jnp.int32))


def _topk_lanes(g, valid, n_sel):
    lane = lax.broadcasted_iota(jnp.int32, g.shape, 1).astype(F32)
    gm = jnp.where(valid, g, NEG_INF)
    sel = jnp.zeros(g.shape, jnp.bool_)
    for _ in range(n_sel):
        mx = jnp.max(gm, axis=1, keepdims=True)
        first = jnp.min(jnp.where(gm == mx, lane, float(LANES)), axis=1, keepdims=True)
        pick = lane == first
        sel = sel | pick
        gm = jnp.where(pick, NEG_INF, gm)
    return sel & valid


def _softmax_pv(logits2, v):
    p = jnp.exp2(logits2 - jnp.max(logits2, axis=1, keepdims=True))
    l = jnp.sum(p, axis=1, keepdims=True)
    return jnp.dot(p.astype(BF16), v, preferred_element_type=F32) / l


def _softmax_pv_t(logits2_t, vt):
    p = jnp.exp2(logits2_t - jnp.max(logits2_t, axis=0, keepdims=True))
    l = jnp.sum(p, axis=0, keepdims=True)
    ot = jnp.dot(vt, p.astype(BF16), preferred_element_type=F32)
    return jnp.transpose(ot / l)


def _topk_rows(g, valid, n_sel):
    rid = lax.broadcasted_iota(jnp.int32, g.shape, 0).astype(F32)
    gm = jnp.where(valid, g, NEG_INF)
    sel = jnp.zeros(g.shape, jnp.bool_)
    for _ in range(n_sel):
        mx = jnp.max(gm, axis=0, keepdims=True)
        first = jnp.min(jnp.where(gm == mx, rid, float(g.shape[0])), axis=0, keepdims=True)
        pick = rid == first
        sel = sel | pick
        gm = jnp.where(pick, NEG_INF, gm)
    return sel & valid


def _width_classes(seq, tile):
    widths = sorted({w for w in (seq // 4, seq // 2, seq) if w % tile == 0 and w >= tile})
    classes, lo = [], 0
    for w in widths:
        classes.append((lo, w // tile, w))
        lo = w // tile
    return classes


def _dsa_prompt_kernel(qa_ref, ka_ref, va_ref, qi_ref, ki_ref, wi_ref, o_ref, ki_s, qcat_s, wt_s, key_s, bias_s,
                       *, tq, seq, n_sel, n_heads, n_idx_heads, groups):
    i = pl.program_id(1)
    n_valid = i + 1
    kblk = 32

    @pl.when(i == 0)
    def _():
        ki_s[...] = ki_ref[:, :IDX_DIM].astype(BF16)

    for h in range(n_idx_heads):
        qcat_s[h * tq:(h + 1) * tq, :] = qi_ref[:, h * IDX_DIM:(h + 1) * IDX_DIM]
    wt_s[...] = jnp.transpose(wi_ref[...]) * ((IDX_DIM ** -0.5) * (n_idx_heads ** -0.5))

    krow = lax.broadcasted_iota(jnp.int32, (tq, tq), 0)
    qcol = lax.broadcasted_iota(jnp.int32, (tq, tq), 1)

    def visible(c):
        return krow <= qcol + jnp.where(c == i, 0, tq)

    for c in range(seq // tq):
        @pl.when(c < n_valid)
        def _(c=c):
            d_all = lax.dot_general(ki_s[c * tq:(c + 1) * tq, :], qcat_s[...], _NT,
                                    preferred_element_type=F32)
            for kb in range(tq // kblk):
                rows = slice(kb * kblk, (kb + 1) * kblk)
                s = None
                for h in range(n_idx_heads):
                    term = jnp.maximum(d_all[rows, h * tq:(h + 1) * tq], 0.0) * wt_s[IDX_DIM + h:IDX_DIM + h + 1, :]
                    s = term if s is None else s + term
                key_s[c, rows, :] = _ordered_key(jnp.where(visible(c)[rows, :], s, NEG_INF))

    def count_ge(cand):
        def body(c, acc):
            hit = (key_s[c] >= cand).astype(F32)
            return acc + jnp.sum(hit.reshape(tq // 8, 8, tq), axis=0)

        acc = lax.fori_loop(0, n_valid, body, jnp.zeros((8, tq), F32))
        return jnp.sum(acc, axis=0, keepdims=True)

    thr = _kth_largest_key(count_ge, (1, tq), n_sel)

    def kv_cols(h):
        return slice((h // groups) * HEAD_DIM, (h // groups + 1) * HEAD_DIM)

    def attend(width):
        for c in range(width // tq):
            @pl.when(c < n_valid)
            def _(c=c):
                ok = (key_s[c] >= thr) & visible(c)
                bias_s[:, c * tq:(c + 1) * tq] = jnp.transpose(jnp.where(ok, 0.0, NEG_INF))

            @pl.when(c >= n_valid)
            def _(c=c):
                bias_s[:, c * tq:(c + 1) * tq] = jnp.full((tq, tq), NEG_INF, F32)

        def qk(h):
            return lax.dot_general(qa_ref[:, h * HEAD_DIM:(h + 1) * HEAD_DIM], ka_ref[0:width, kv_cols(h)], _NT,
                                   preferred_element_type=F32)

        raw = qk(0)
        for h in range(n_heads):
            nxt = qk(h + 1) if h + 1 < n_heads else None
            o = _softmax_pv(raw + bias_s[:, 0:width], va_ref[0:width, kv_cols(h)])
            o_ref[:, h * HEAD_DIM:(h + 1) * HEAD_DIM] = o.astype(o_ref.dtype)
            raw = nxt

    for lo, hi, width in _width_classes(seq, tq):
        @pl.when((i >= lo) & (i < hi))
        def _(width=width):
            attend(width)


def _dsa_prompt(hq, hk_bf, hv_bf, hqi, hkiwi, *, batch, seq, n_heads, n_idx_heads, groups):
    tq = min(256, seq)
    assert seq % tq == 0 and tq % LANES == 0
    nq = seq // tq
    aw = n_heads * HEAD_DIM
    kvw = aw // groups
    n_sel = min(IDX_TOPK, seq // 4)
    kern = functools.partial(_dsa_prompt_kernel, tq=tq, seq=seq, n_sel=n_sel, n_heads=n_heads,
                             n_idx_heads=n_idx_heads, groups=groups)
    return pl.pallas_call(
        kern,
        out_shape=jax.ShapeDtypeStruct((batch * seq, aw), BF16),
        grid=(batch, nq),
        in_specs=[pl.BlockSpec((tq, aw), lambda b, i: (b * nq + i, 0)),
                  pl.BlockSpec((seq, kvw), lambda b, i: (b, 0)),
                  pl.BlockSpec((seq, kvw), lambda b, i: (b, 0)),
                  pl.BlockSpec((tq, n_idx_heads * IDX_DIM), lambda b, i: (b * nq + i, 0)),
                  pl.BlockSpec((seq, LANES), lambda b, i: (b, 0)),
                  pl.BlockSpec((tq, LANES), lambda b, i: (b * nq + i, 0))],
        out_specs=pl.BlockSpec((tq, aw), lambda b, i: (b * nq + i, 0)),
        scratch_shapes=[pltpu.VMEM((seq, IDX_DIM), BF16), pltpu.VMEM((n_idx_heads * tq, IDX_DIM), BF16),
                        pltpu.VMEM((LANES, tq), F32), pltpu.VMEM((nq, tq, tq), jnp.int32),
                        pltpu.VMEM((tq, seq), F32)],
        compiler_params=_params(("parallel", "arbitrary")),
        name="dsa_prompt",
    )(hq, hk_bf, hv_bf, hqi, hkiwi, hkiwi)


def _moba_prompt_kernel(qb_ref, kb_ref, kmean_ref, vb_ref, o_ref, km_s, vt_s, *, nb, n_sel, n_heads, groups):
    i = pl.program_id(1)
    blk = MOBA_BLOCK

    @pl.when(i == 0)
    def _():
        km_s[...] = jnp.zeros_like(km_s)
        per = kmean_ref.shape[1]
        for j in range(nb):
            km_s[j:j + 1, :] = kmean_ref[j // per, j % per:j % per + 1, :]
        for kv in range(n_heads // groups):
            v = vb_ref[:, kv * HEAD_DIM:(kv + 1) * HEAD_DIM].astype(F32)
            vt_s[kv] = jnp.transpose(v).astype(BF16)

    past = lax.broadcasted_iota(jnp.int32, (LANES, blk), 0) < i
    tril_bias = jnp.where(lax.broadcasted_iota(jnp.int32, (blk, blk), 0)
                          <= lax.broadcasted_iota(jnp.int32, (blk, blk), 1), 0.0, NEG_INF)

    def kv_cols(h):
        return slice((h // groups) * HEAD_DIM, (h // groups + 1) * HEAD_DIM)

    def attend(width):
        def qk(h):
            q = qb_ref[:, h * HEAD_DIM:(h + 1) * HEAD_DIM]
            gate = lax.dot_general(km_s[:, kv_cols(h)].astype(BF16), q, _NT, preferred_element_type=F32)
            return gate, lax.dot_general(kb_ref[0:width, kv_cols(h)], q, _NT, preferred_element_type=F32)

        cur = qk(0)
        for h in range(n_heads):
            nxt = qk(h + 1) if h + 1 < n_heads else None
            gate, raw = cur
            sel_bias = jnp.where(_topk_rows(gate, past, n_sel), 0.0, NEG_INF)
            pieces = []
            for j in range(width // blk):
                bias = jnp.where(j == i, tril_bias, sel_bias[j:j + 1, :])
                pieces.append(raw[j * blk:(j + 1) * blk, :] + bias)
            o = _softmax_pv_t(jnp.concatenate(pieces, axis=0), vt_s[h // groups, :, 0:width])
            o_ref[:, h * HEAD_DIM:(h + 1) * HEAD_DIM] = o.astype(o_ref.dtype)
            cur = nxt

    for lo, hi, width in _width_classes(nb * blk, blk):
        @pl.when((i >= lo) & (i < hi))
        def _(width=width):
            attend(width)


def _moba_prompt(hq, hk_bf, k_means, hv_bf, *, batch, seq, n_heads, groups):
    blk = MOBA_BLOCK
    assert seq % blk == 0
    nb = seq // blk
    g_per_seq = k_means.shape[0] // batch
    assert 1 <= nb <= LANES and g_per_seq * k_means.shape[1] == nb
    aw = n_heads * HEAD_DIM
    kvw = aw // groups
    kern = functools.partial(_moba_prompt_kernel, nb=nb, n_sel=min(MOBA_TOPK, nb), n_heads=n_heads,
                             groups=groups)
    return pl.pallas_call(
        kern,
        out_shape=jax.ShapeDtypeStruct((batch * seq, aw), BF16),
        grid=(batch, nb),
        in_specs=[pl.BlockSpec((blk, aw), lambda b, i: (b * nb + i, 0)),
                  pl.BlockSpec((seq, kvw), lambda b, i: (b, 0)),
                  pl.BlockSpec((g_per_seq, k_means.shape[1], kvw), lambda b, i: (b, 0, 0)),
                  pl.BlockSpec((seq, kvw), lambda b, i: (b, 0))],
        out_specs=pl.BlockSpec((blk, aw), lambda b, i: (b * nb + i, 0)),
        scratch_shapes=[pltpu.VMEM((LANES, kvw), F32), pltpu.VMEM((n_heads // groups, HEAD_DIM, seq), BF16)],
        compiler_params=_params(("parallel", "arbitrary")),
        name="moba_prompt",
    )(hq, hk_bf, k_means, hv_bf)


def _gmlp_kernel(cu_ref, cv_ref, g_ref, b_ref, ws_ref, bs_ref, o_ref, *vn_out, n_groups):
    r = ws_ref.shape[1]
    for ch in range(cv_ref.shape[0] // r):
        rows = slice(ch * r, (ch + 1) * r)
        v = cv_ref[rows, :]
        mu = jnp.mean(v, axis=-1, keepdims=True)
        vc = v - mu
        vn = vc * lax.rsqrt(jnp.mean(vc * vc, axis=-1, keepdims=True) + EPS) * g_ref[...] + b_ref[...]
        if vn_out:
            vn_out[0][rows, :] = vn
        for g in range(n_groups):
            cols = slice(g * C_GROUP_DIM, (g + 1) * C_GROUP_DIM)
            z = jnp.dot(ws_ref[g], vn[:, cols].astype(BF16), preferred_element_type=F32) + bs_ref[:, g:g + 1]
            o_ref[rows, cols] = (cu_ref[rows, cols] * z).astype(o_ref.dtype)


def _gmlp(hc, ln_g, ln_b, ws, bs, *, want_vn):
    m = hc.shape[0]
    n_groups, chunk = ws.shape[0], ws.shape[1]
    cw = n_groups * C_GROUP_DIM
    r = min(m, 8 * chunk)
    assert m % r == 0 and r % chunk == 0
    out_shape = [jax.ShapeDtypeStruct((m, cw), BF16)]
    out_specs = [pl.BlockSpec((r, cw), lambda i: (i, 0))]
    if want_vn:
        out_shape.append(jax.ShapeDtypeStruct((m, cw), F32))
        out_specs.append(pl.BlockSpec((r, cw), lambda i: (i, 0)))
    return pl.pallas_call(
        functools.partial(_gmlp_kernel, n_groups=n_groups),
        out_shape=out_shape,
        grid=(m // r,),
        in_specs=[pl.BlockSpec((r, cw), lambda i: (i, 0)),
                  pl.BlockSpec((r, cw), lambda i: (i, 1)),
                  pl.BlockSpec((1, cw), lambda i: (0, 0)),
                  pl.BlockSpec((1, cw), lambda i: (0, 0)),
                  pl.BlockSpec((n_groups, chunk, chunk), lambda i: (0, 0, 0)),
                  pl.BlockSpec((chunk, n_groups), lambda i: (0, 0))],
        out_specs=out_specs,
        compiler_params=_params(("parallel",)),
        name="gmlp",
    )(hc, hc, ln_g.reshape(1, cw), ln_b.reshape(1, cw), ws, bs)


def _merge_kernel(oa_ref, ob_ref, oc_ref, wb_ref, g0_ref, g1_ref, g2_ref, o_ref, wbf_s):
    @pl.when(pl.program_id(1) == 0)
    def _():
        wbf_s[...] = wb_ref[...].astype(BF16)

    acc = None
    for o_n, g_n, n in ((oa_ref, g0_ref, 0), (ob_ref, g1_ref, 1), (oc_ref, g2_ref, 2)):
        br = jnp.dot(o_n[...], wbf_s[n], preferred_element_type=F32)
        term = (1.0 / (1.0 + jnp.exp(-g_n[...]))) * br
        acc = term if acc is None else acc + term
    o_ref[...] = acc.astype(o_ref.dtype)


def _merge(o_a, o_b, o_c, wb_stack, layer, hc, *, d_model, gate_col0):
    m, aw = o_a.shape
    tm, tn = min(1024, m), 512
    nj = d_model // tn
    g0 = gate_col0 // tn
    gate_specs = [pl.BlockSpec((tm, tn), functools.partial(lambda j, i, n: (i, g0 + n * nj + j), n=n))
                  for n in range(N_BRANCH)]
    return pl.pallas_call(
        _merge_kernel,
        out_shape=jax.ShapeDtypeStruct((m, d_model), BF16),
        grid=(nj, m // tm),
        in_specs=[pl.BlockSpec((tm, aw), lambda j, i: (i, 0))] * 3
        + [pl.BlockSpec((None, N_BRANCH, aw, tn), lambda j, i: (layer, 0, 0, j))] + gate_specs,
        out_specs=pl.BlockSpec((tm, tn), lambda j, i: (i, j)),
        scratch_shapes=[pltpu.VMEM((N_BRANCH, aw, tn), BF16)],
        compiler_params=_params(("parallel", "arbitrary")),
        name="merge",
    )(o_a, o_b, o_c, wb_stack, hc, hc, hc)


def _page_specs(block, layer, n_pages, per_step):
    def page_map(b, s, pt, slot):
        return (layer, pt[b * n_pages + s * per_step + slot], 0, 0)

    return [pl.BlockSpec(block, functools.partial(page_map, slot=slot)) for slot in range(per_step)]


def _idx_scores(qi, wi, k, n_new, n_idx_heads, k_is_t=False):
    if k_is_t:
        d = jnp.dot(qi, k, preferred_element_type=F32) * (IDX_DIM ** -0.5)
    else:
        d = lax.dot_general(qi, k, _NT, preferred_element_type=F32) * (IDX_DIM ** -0.5)
    dw = jnp.maximum(d, 0.0) * wi
    acc = dw[0:8]
    for r in range(8, n_idx_heads * n_new, 8):
        acc = acc + dw[r:r + 8]
    s = acc[0:n_new]
    for r in range(n_new, 8, n_new):
        s = s + acc[r:r + n_new]
    return s * (n_idx_heads ** -0.5)


def _dsa_sample_score_kernel(pt_ref, *refs, per_step, n_new, n_idx_heads):
    pages = refs[:per_step]
    qi_ref, wi_ref, o_ref = refs[per_step:]
    keys_t = jnp.concatenate([p[...] for p in pages], axis=1) if per_step > 1 else pages[0][...]
    o_ref[...] = _idx_scores(qi_ref[...], wi_ref[...], keys_t, n_new, n_idx_heads, k_is_t=True)


def _dsa_sample_scores(pt_flat, cache_idx_t, layer, qi_rows, wi_rows, *, dec_batch, n_pages, n_new, n_idx_heads):
    page = cache_idx_t.shape[3]
    assert page == LANES and 8 % n_new == 0 and (n_idx_heads * n_new) % 8 == 0
    per_step = min(PAGES_PER_STEP, n_pages)
    assert n_pages % per_step == 0
    rows = n_idx_heads * n_new
    kern = functools.partial(_dsa_sample_score_kernel, per_step=per_step, n_new=n_new, n_idx_heads=n_idx_heads)
    grid_spec = pltpu.PrefetchScalarGridSpec(
        num_scalar_prefetch=1,
        grid=(dec_batch, n_pages // per_step),
        in_specs=_page_specs((None, None, IDX_DIM, page), layer, n_pages, per_step)
        + [pl.BlockSpec((None, rows, IDX_DIM), lambda b, s, pt: (b, 0, 0)),
           pl.BlockSpec((None, rows, 1), lambda b, s, pt: (b, 0, 0))],
        out_specs=pl.BlockSpec((None, n_new, per_step * LANES), lambda b, s, pt: (b, 0, s)),
    )
    return pl.pallas_call(
        kern,
        out_shape=jax.ShapeDtypeStruct((dec_batch, n_new, n_pages * LANES), F32),
        grid_spec=grid_spec,
        compiler_params=_params(("parallel", "arbitrary")),
        name="dsa_sample_scores",
    )(pt_flat, *([cache_idx_t] * per_step), qi_rows, wi_rows)


def _select_kernel(s_ref, qi_ref, wi_ref, kn_ref, o_ref, on_ref, key_s, keyn_s, *, n_sel, n_new, n_idx_heads, db):
    rows = db * n_new
    t = lax.broadcasted_iota(jnp.int32, (rows, LANES), 0) & (n_new - 1)
    c = lax.broadcasted_iota(jnp.int32, (rows, LANES), 1)
    new_ok = c <= t
    key_s[...] = _ordered_key(s_ref[...])
    for b in range(db):
        s_new = _idx_scores(qi_ref[b], wi_ref[b], kn_ref[b], n_new, n_idx_heads)
        keyn_s[b * n_new:(b + 1) * n_new, :] = _ordered_key(jnp.where(new_ok[0:n_new], s_new, NEG_INF))

    def count_ge(cand):
        return (jnp.sum((key_s[...] >= cand).astype(F32), axis=1, keepdims=True)
                + jnp.sum((keyn_s[...] >= cand).astype(F32), axis=1, keepdims=True))

    thr = _kth_largest_key(count_ge, (rows, 1), n_sel)
    o_ref[...] = (key_s[...] >= thr).astype(F32)
    on_ref[...] = ((keyn_s[...] >= thr) & new_ok).astype(F32)


def _dsa_sample_select(scores, qi_rows, wi_rows, ki_new_pad, n_sel, *, n_idx_heads):
    db, n_new, length = scores.shape
    rows = db * n_new
    assert n_new & (n_new - 1) == 0
    whole = pl.BlockSpec(memory_space=pltpu.VMEM)
    sel, sel_new = pl.pallas_call(
        functools.partial(_select_kernel, n_sel=n_sel, n_new=n_new, n_idx_heads=n_idx_heads, db=db),
        out_shape=[jax.ShapeDtypeStruct((rows, length), F32), jax.ShapeDtypeStruct((rows, LANES), F32)],
        in_specs=[whole] * 4,
        out_specs=[whole] * 2,
        scratch_shapes=[pltpu.VMEM((rows, length), jnp.int32), pltpu.VMEM((rows, LANES), jnp.int32)],
        compiler_params=pltpu.CompilerParams(vmem_limit_bytes=VMEM_LIMIT),
        name="dsa_sample_select",
    )(scores.reshape(rows, length), qi_rows, wi_rows, ki_new_pad)
    return sel.reshape(db, n_new, length), sel_new.reshape(db, n_new, LANES)


def _head_rows(page_refs, kv, n_kv):
    parts = [r[pl.ds(kv, LANES, stride=n_kv), :] for r in page_refs]
    return parts[0] if len(parts) == 1 else jnp.concatenate(parts, axis=0)


def _dsa_sample_attn_kernel(pt_ref, *refs, per_step, n_steps, n_new, n_kv, groups):
    k_pages, v_pages = refs[:per_step], refs[per_step:2 * per_step]
    kn_ref, vn_ref, q_ref, sel_ref, seln_ref, o_ref, m_s, l_s, acc_s = refs[2 * per_step:]
    step = pl.program_id(1)
    rpk = groups * n_new

    @pl.when(step == 0)
    def _():
        m_s[...] = jnp.full_like(m_s, NEG_INF)
        l_s[...] = jnp.zeros_like(l_s)
        acc_s[...] = jnp.zeros_like(acc_s)

    def update(get_k, get_v, sel):
        mask = jnp.concatenate([sel] * groups, axis=0) > 0.0
        rows = [slice(kv * rpk, (kv + 1) * rpk) for kv in range(n_kv)]
        logits = [lax.dot_general(q_ref[rows[kv], :], get_k(kv), _NT, preferred_element_type=F32)
                  for kv in range(n_kv)]
        probs, alphas = [], []
        for kv in range(n_kv):
            s = jnp.where(mask, logits[kv], NEG_INF)
            m_old = m_s[rows[kv], :]
            m_new = jnp.maximum(m_old, jnp.max(s, axis=1, keepdims=True))
            m_safe = jnp.where(m_new == NEG_INF, 0.0, m_new)
            alpha = jnp.exp2(m_old - m_safe)
            pr = jnp.exp2(s - m_safe)
            m_s[rows[kv], :] = m_new
            l_s[rows[kv], :] = alpha * l_s[rows[kv], :] + jnp.sum(pr, axis=1, keepdims=True)
            probs.append(pr)
            alphas.append(alpha)
        for kv in range(n_kv):
            acc_s[rows[kv], :] = alphas[kv] * acc_s[rows[kv], :] + jnp.dot(
                probs[kv], get_v(kv), preferred_element_type=F32)

    update(lambda kv: _head_rows(k_pages, kv, n_kv), lambda kv: _head_rows(v_pages, kv, n_kv), sel_ref[...])

    @pl.when(step == n_steps - 1)
    def _():
        update(lambda kv: kn_ref[kv], lambda kv: vn_ref[kv], seln_ref[...])
        o_ref[...] = acc_s[...] / l_s[...]


def _dsa_sample_attn(pt_flat, cache_k, cache_v, layer, k_new_pad, v_new_pad, q_rows, sel, sel_new, *, dec_batch,
                     n_pages, n_new, n_kv, groups):
    rows = n_kv * groups * n_new
    prow = cache_k.shape[2]
    per_step = min(PAGES_PER_STEP, n_pages)
    assert n_pages % per_step == 0
    n_steps = n_pages // per_step
    kern = functools.partial(_dsa_sample_attn_kernel, per_step=per_step, n_steps=n_steps, n_new=n_new, n_kv=n_kv,
                             groups=groups)
    page_specs = _page_specs((None, None, prow, HEAD_DIM), layer, n_pages, per_step)
    grid_spec = pltpu.PrefetchScalarGridSpec(
        num_scalar_prefetch=1,
        grid=(dec_batch, n_steps),
        in_specs=page_specs + page_specs
        + [pl.BlockSpec((None, n_kv, LANES, HEAD_DIM), lambda b, s, pt: (b, 0, 0, 0)),
           pl.BlockSpec((None, n_kv, LANES, HEAD_DIM), lambda b, s, pt: (b, 0, 0, 0)),
           pl.BlockSpec((None, rows, HEAD_DIM), lambda b, s, pt: (b, 0, 0)),
           pl.BlockSpec((None, n_new, per_step * LANES), lambda b, s, pt: (b, 0, s)),
           pl.BlockSpec((None, n_new, LANES), lambda b, s, pt: (b, 0, 0))],
        out_specs=pl.BlockSpec((None, rows, HEAD_DIM), lambda b, s, pt: (b, 0, 0)),
        scratch_shapes=[pltpu.VMEM((rows, 1), F32), pltpu.VMEM((rows, 1), F32),
                        pltpu.VMEM((rows, HEAD_DIM), F32)],
    )
    return pl.pallas_call(
        kern,
        out_shape=jax.ShapeDtypeStruct((dec_batch, rows, HEAD_DIM), F32),
        grid_spec=grid_spec,
        compiler_params=_params(("parallel", "arbitrary")),
        name="dsa_sample_attn",
    )(pt_flat, *([cache_k] * per_step), *([cache_v] * per_step), k_new_pad, v_new_pad, q_rows, sel, sel_new)


def _moba_sample_kernel(pt_ref, *refs, bps, n_steps, nb, n_sel, n_new, n_kv, groups, past_len):
    k_pages, v_pages = refs[:2 * bps], refs[2 * bps:4 * bps]
    kn_ref, vn_ref, q_ref, o_ref, gate_s, m_s, l_s, acc_s = refs[4 * bps:]
    step = pl.program_id(1)
    rpk = groups * n_new
    rows_all = n_kv * rpk
    lane = lax.broadcasted_iota(jnp.int32, (rows_all, LANES), 1)

    @pl.when(step == 0)
    def _():
        gate_s[...] = jnp.zeros_like(gate_s)
        m_s[...] = jnp.zeros_like(m_s)
        l_s[...] = jnp.zeros_like(l_s)

    gate, m_blk, l_blk = gate_s[...], m_s[...], l_s[...]
    blk = MOBA_BLOCK
    kv_rows = [slice(kv * rpk, (kv + 1) * rpk) for kv in range(n_kv)]
    ks = [_head_rows(k_pages, kv, n_kv) for kv in range(n_kv)]
    logits = [lax.dot_general(q_ref[kv_rows[kv], :], ks[kv], _NT, preferred_element_type=F32)
              for kv in range(n_kv)]
    probs = {}
    for bb in range(bps):
        g_cols, m_cols, l_cols = [], [], []
        for kv in range(n_kv):
            kmean = jnp.mean(ks[kv][bb * blk:(bb + 1) * blk, :], axis=0, keepdims=True)
            g_cols.append(jnp.sum(q_ref[kv_rows[kv], :] * kmean, axis=1, keepdims=True))
            s = logits[kv][:, bb * blk:(bb + 1) * blk]
            m = jnp.max(s, axis=1, keepdims=True)
            pr = jnp.exp2(s - m)
            m_cols.append(m)
            l_cols.append(jnp.sum(pr, axis=1, keepdims=True))
            probs[bb, kv] = pr
        here = lane == step * bps + bb
        gate = jnp.where(here, jnp.concatenate(g_cols, axis=0), gate)
        m_blk = jnp.where(here, jnp.concatenate(m_cols, axis=0), m_blk)
        l_blk = jnp.where(here, jnp.concatenate(l_cols, axis=0), l_blk)
    for bb in range(bps):
        for kv in range(n_kv):
            v = _head_rows(v_pages[2 * bb:2 * bb + 2], kv, n_kv)
            acc_s[step * bps + bb, kv_rows[kv], :] = jnp.dot(probs[bb, kv], v, preferred_element_type=F32)
    gate_s[...] = gate
    m_s[...] = m_blk
    l_s[...] = l_blk

    @pl.when(step == n_steps - 1)
    def _():
        t_row = lax.broadcasted_iota(jnp.int32, (rpk, LANES), 0) & (n_new - 1)
        c_col = lax.broadcasted_iota(jnp.int32, (rpk, LANES), 1)
        own_ok = c_col <= t_row
        t_all = lax.broadcasted_iota(jnp.int32, (rows_all, LANES), 0) & (n_new - 1)
        own_blk = jnp.right_shift(past_len + t_all, MOBA_BLOCK.bit_length() - 1)
        valid = (lane < own_blk) & (lane < nb)
        sel = _topk_lanes(gate, valid, n_sel)
        for kv in range(n_kv):
            rows = slice(kv * rpk, (kv + 1) * rpk)
            q = q_ref[rows, :]
            s = lax.dot_general(q, kn_ref[kv], _NT, preferred_element_type=F32)
            s = jnp.where(own_ok, s, NEG_INF)
            m_o = jnp.max(s, axis=1, keepdims=True)
            m_sel = jnp.where(sel[rows, :], m_blk[rows, :], NEG_INF)
            m_all = jnp.maximum(m_o, jnp.max(m_sel, axis=1, keepdims=True))
            pr = jnp.exp2(s - m_all)
            w = jnp.where(sel[rows, :], jnp.exp2(m_sel - m_all), 0.0)
            den = jnp.sum(pr, axis=1, keepdims=True) + jnp.sum(w * l_blk[rows, :], axis=1, keepdims=True)
            num = jnp.dot(pr, vn_ref[kv], preferred_element_type=F32)
            for jb in range(nb):
                num = num + w[:, jb:jb + 1] * acc_s[jb, rows, :]
            o_ref[rows, :] = num / den


def _moba_sample(pt_flat, cache_k, cache_v, layer, k_new_pad, v_new_pad, q_rows, *, dec_batch, n_pages, n_new,
                 n_kv, groups, past_len):
    assert past_len % MOBA_BLOCK == 0 and n_new < MOBA_BLOCK and MOBA_BLOCK == 2 * LANES
    assert n_new & (n_new - 1) == 0
    nb = (past_len + n_new) // MOBA_BLOCK
    assert 1 <= nb <= LANES and n_pages == 2 * nb
    bps = min(PAGES_PER_STEP // 2, nb)
    assert nb % bps == 0
    n_steps = nb // bps
    rows = n_kv * groups * n_new
    prow = cache_k.shape[2]
    kern = functools.partial(_moba_sample_kernel, bps=bps, n_steps=n_steps, nb=nb, n_sel=min(MOBA_TOPK, nb),
                             n_new=n_new, n_kv=n_kv, groups=groups, past_len=past_len)
    page_specs = _page_specs((None, None, prow, HEAD_DIM), layer, n_pages, 2 * bps)
    grid_spec = pltpu.PrefetchScalarGridSpec(
        num_scalar_prefetch=1,
        grid=(dec_batch, n_steps),
        in_specs=page_specs + page_specs
        + [pl.BlockSpec((None, n_kv, LANES, HEAD_DIM), lambda b, s, pt: (b, 0, 0, 0)),
           pl.BlockSpec((None, n_kv, LANES, HEAD_DIM), lambda b, s, pt: (b, 0, 0, 0)),
           pl.BlockSpec((None, rows, HEAD_DIM), lambda b, s, pt: (b, 0, 0))],
        out_specs=pl.BlockSpec((None, rows, HEAD_DIM), lambda b, s, pt: (b, 0, 0)),
        scratch_shapes=[pltpu.VMEM((rows, LANES), F32), pltpu.VMEM((rows, LANES), F32),
                        pltpu.VMEM((rows, LANES), F32), pltpu.VMEM((nb, rows, HEAD_DIM), F32)],
    )
    return pl.pallas_call(
        kern,
        out_shape=jax.ShapeDtypeStruct((dec_batch, rows, HEAD_DIM), F32),
        grid_spec=grid_spec,
        compiler_params=_params(("parallel", "arbitrary")),
        name="moba_sample",
    )(pt_flat, *([cache_k] * (2 * bps)), *([cache_v] * (2 * bps)), k_new_pad, v_new_pad, q_rows)


def _rope_tables(pos, rot, width, n_rope_lanes=LANES):
    half = rot // 2
    inv = jnp.power(ROPE_THETA, -jnp.arange(half, dtype=F32) / half)
    ang = pos.astype(F32)[:, None] * inv[None, :]
    cos, sin = jnp.cos(ang), jnp.sin(ang)
    t = pos.shape[0]
    zeros = jnp.zeros((t, half), F32)
    rest0 = jnp.zeros((t, width - rot), F32)
    c = jnp.concatenate([cos, cos, jnp.ones((t, width - rot), F32)], axis=1)
    sa = jnp.concatenate([-sin, zeros, rest0], axis=1)
    sb = jnp.concatenate([zeros, sin, rest0], axis=1)
    reps = n_rope_lanes // width
    pad = LANES - n_rope_lanes
    c = jnp.concatenate([jnp.tile(c, (1, reps)), jnp.ones((t, pad), F32)], axis=1)
    sa = jnp.concatenate([jnp.tile(sa, (1, reps)), jnp.zeros((t, pad), F32)], axis=1)
    sb = jnp.concatenate([jnp.tile(sb, (1, reps)), jnp.zeros((t, pad), F32)], axis=1)
    return c, sa, sb


def _kv_major_rows(a, db, n_new, n_kv, groups):
    a = a.reshape(db, n_new, n_kv, groups, HEAD_DIM)
    return jnp.transpose(a, (0, 2, 3, 1, 4)).reshape(db, n_kv * groups * n_new, HEAD_DIM)


def _from_kv_major_rows(o, db, n_new, n_kv, groups):
    o = o.reshape(db, n_kv, groups, n_new, HEAD_DIM)
    return jnp.transpose(o, (0, 3, 1, 2, 4)).reshape(db * n_new, n_kv * groups * HEAD_DIM)


def _new_kv_pad(k, db, n_new, n_kv):
    k = jnp.transpose(k.reshape(db, n_new, n_kv, HEAD_DIM), (0, 2, 1, 3))
    return jnp.pad(k, ((0, 0), (0, 0), (0, LANES - n_new), (0, 0)))


def kernel(x_prompt, x_sample, cache_a_k, cache_a_v, cache_a_idx, cache_b_k, cache_b_v, page_table, w_in, norm1_g,
           c_ln_g, c_ln_b, w_spatial, b_spatial, w_branch, w_out, norm2_g, w_ff1, w_ff2, norm_f_g):
    batch, seq, d_model = x_prompt.shape
    db, n_new, _ = x_sample.shape
    depth, n_phys, page, n_kv, _ = cache_a_k.shape
    n_pages = page_table.shape[1]
    past_len = n_pages * page
    n_heads = d_model // 256
    groups = n_heads // n_kv
    aw = n_heads * HEAD_DIM
    kvw = n_kv * HEAD_DIM
    n_idx_heads = d_model // 128
    iw = n_idx_heads * IDX_DIM
    n_cg = w_spatial.shape[1]
    cw = n_cg * C_GROUP_DIM
    rot, idx_rot = HEAD_DIM // 4, IDX_DIM // 4
    assert page == LANES and aw == 2 * kvw and cw % 512 == 0 and seq % CHUNK == 0 and n_new <= CHUNK

    sizes = (aw, kvw, kvw, iw, IDX_DIM, n_idx_heads, aw, kvw, kvw, cw, cw, N_BRANCH * d_model)
    offs = [0]
    for s in sizes:
        offs.append(offs[-1] + s)
    (o_qa, o_ka, o_va, o_qi, o_ki, o_wi, o_qb, o_kb, o_vb, o_cu, o_cv, o_gl, _) = offs

    pos_p = jnp.arange(seq, dtype=jnp.int32)
    pos_s = jnp.tile(past_len + jnp.arange(n_new, dtype=jnp.int32), db)
    tabs = {
        "p": (_rope_tables(pos_p, rot, HEAD_DIM), _rope_tables(pos_p, idx_rot, IDX_DIM),
              _rope_tables(pos_p, idx_rot, IDX_DIM, IDX_DIM)),
        "s": (_rope_tables(pos_s, rot, HEAD_DIM), _rope_tables(pos_s, idx_rot, IDX_DIM),
              _rope_tables(pos_s, idx_rot, IDX_DIM, IDX_DIM)),
    }

    cache_ak = cache_a_k.reshape(depth, n_phys, page * n_kv, HEAD_DIM)
    cache_av = cache_a_v.reshape(depth, n_phys, page * n_kv, HEAD_DIM)
    cache_bk = cache_b_k.reshape(depth, n_phys, page * n_kv, HEAD_DIM)
    cache_bv = cache_b_v.reshape(depth, n_phys, page * n_kv, HEAD_DIM)
    pt_flat = page_table.reshape(-1).astype(jnp.int32)

    xp = x_prompt.reshape(batch * seq, d_model)
    xs = x_sample.reshape(db * n_new, d_model)
    rs = db * n_new
    tri = jnp.tril(jnp.ones((CHUNK, CHUNK), dtype=bool))
    sel_s = min(IDX_TOPK, (past_len + n_new) // 4)

    outs = {k: [] for k in ("pa_k", "pa_v", "pa_i", "pb_k", "pb_v", "sa_k", "sa_v", "sa_i", "sb_k", "sb_v", "sc_v")}

    w_in_t = jnp.swapaxes(w_in, 1, 2)
    cache_idx_t = jnp.swapaxes(cache_a_idx, 2, 3)

    kv_bufs = {(which, name): None for which in "ps" for name in ("ka", "kb", "va", "vb")}

    def project(x, l, which):
        t128, t64, tkiwi = tabs[which]
        xn = _rms_norm(x, norm1_g[l], BF16)

        def proj(n_off, n, out_dtypes, epi="none", extra=(), half=0, out_scale=1.0, **kw):
            return _matmul_ws(xn, w_in_t, l, out_dtypes, epi, extra, half, n_off=n_off, n=n, w_rows_are_n=True,
                              out_scale=out_scale, **kw)

        def kv_proj(name, n_off, epi="none", extra=(), half=0, want_means=False):
            res = proj(n_off, kvw, (BF16,), epi, extra, half, kv_out=(kv_bufs[which, name], depth, n_kv),
                       want_means=want_means)
            kv_bufs[which, name] = res[1]
            return (res[0], res[2]) if want_means else res[0]

        (hqa,) = proj(o_qa, aw, (BF16,), "rope", t128, rot // 2, Q_SCALE)
        (hqb,) = proj(o_qb, aw, (BF16,), "rope", t128, rot // 2, Q_SCALE)
        hka_bf = kv_proj("ka", o_ka, "rope", t128, rot // 2)
        hkb_bf = kv_proj("kb", o_kb, "rope", t128, rot // 2, want_means=(which == "p"))
        hva_bf = kv_proj("va", o_va)
        hvb_bf = kv_proj("vb", o_vb)
        (hqi,) = proj(o_qi, iw, (BF16,), "rope", t64, idx_rot // 2)
        (hkiwi,) = proj(o_ki, LANES, (F32,), "rope", tkiwi, idx_rot // 2)
        (hc,) = proj(o_cu, 2 * cw + N_BRANCH * d_model, (F32,))
        return hqa, hqb, hka_bf, hkb_bf, hva_bf, hvb_bf, hqi, hkiwi, hc

    def new_rows(name, l):
        return kv_bufs["s", name].reshape(depth, rs, kvw)[l]

    def merge_ffn(x, o_a, o_b, o_c, hc, l):
        merged = _merge(o_a, o_b, o_c, w_branch, l, hc, d_model=d_model, gate_col0=2 * cw)
        (x1,) = _matmul_ws(merged, w_out, l, (F32,), "residual", (x,))
        hn = _rms_norm(x1, norm2_g[l], BF16)
        (h1,) = _matmul_ws(hn, w_ff1, l, (BF16,), "relu2")
        (x2,) = _matmul_kt(h1, w_ff2, l, (F32,), "residual", (x1,))
        return x2

    for l in range(depth):
        hqa, hqb, hka_bf, (hkb_bf, kb_means), hva_bf, hvb_bf, hqi, hkiwi, hc = project(xp, l, "p")
        o_a = _dsa_prompt(hqa, hka_bf, hva_bf, hqi, hkiwi, batch=batch, seq=seq, n_heads=n_heads,
                          n_idx_heads=n_idx_heads, groups=groups)
        o_b = _moba_prompt(hqb, hkb_bf, kb_means, hvb_bf, batch=batch, seq=seq, n_heads=n_heads, groups=groups)
        ws_p = jnp.where(tri[None], w_spatial[l], 0).astype(BF16)
        bs_p = b_spatial[l].T
        (o_c,) = _gmlp(hc, c_ln_g[l], c_ln_b[l], ws_p, bs_p, want_vn=False)
        xp = merge_ffn(xp, o_a, o_b, o_c, hc, l)
        outs["pa_i"].append(hkiwi[:, :IDX_DIM].reshape(batch, seq, IDX_DIM))

        hqa, hqb, hka_bf, hkb_bf, hva_bf, hvb_bf, hqi, hkiwi, hc = project(xs, l, "s")
        qi_rows = jnp.transpose(hqi.astype(F32).reshape(db, n_new, n_idx_heads, IDX_DIM), (0, 2, 1, 3)).reshape(
            db, n_idx_heads * n_new, IDX_DIM)
        wi_rows = jnp.transpose(hkiwi[:, IDX_DIM:IDX_DIM + n_idx_heads].reshape(db, n_new, n_idx_heads),
                                (0, 2, 1)).reshape(db, n_idx_heads * n_new, 1)
        ki_new = jnp.pad(hkiwi[:, :IDX_DIM].reshape(db, n_new, IDX_DIM), ((0, 0), (0, LANES - n_new), (0, 0)))
        scores = _dsa_sample_scores(pt_flat, cache_idx_t, l, qi_rows, wi_rows, dec_batch=db, n_pages=n_pages,
                                    n_new=n_new, n_idx_heads=n_idx_heads)
        sel, sel_new = _dsa_sample_select(scores, qi_rows, wi_rows, ki_new, sel_s, n_idx_heads=n_idx_heads)
        qa_rows = _kv_major_rows(hqa.astype(F32), db, n_new, n_kv, groups)
        o_a = _dsa_sample_attn(pt_flat, cache_ak, cache_av, l, _new_kv_pad(new_rows("ka", l), db, n_new, n_kv),
                               _new_kv_pad(new_rows("va", l), db, n_new, n_kv), qa_rows, sel, sel_new, dec_batch=db,
                               n_pages=n_pages, n_new=n_new, n_kv=n_kv, groups=groups)
        o_a = _from_kv_major_rows(o_a, db, n_new, n_kv, groups).astype(BF16)
        qb_rows = _kv_major_rows(hqb.astype(F32), db, n_new, n_kv, groups)
        o_b = _moba_sample(pt_flat, cache_bk, cache_bv, l, _new_kv_pad(new_rows("kb", l), db, n_new, n_kv),
                           _new_kv_pad(new_rows("vb", l), db, n_new, n_kv), qb_rows, dec_batch=db, n_pages=n_pages,
                           n_new=n_new, n_kv=n_kv, groups=groups, past_len=past_len)
        o_b = _from_kv_major_rows(o_b, db, n_new, n_kv, groups).astype(BF16)
        ws_c = jnp.where(tri[None, :n_new, :n_new], w_spatial[l][:, :n_new, :n_new], 0)
        ws_s = jnp.einsum("ab,gij->gaibj", jnp.eye(db, dtype=F32), ws_c).reshape(n_cg, rs, rs).astype(BF16)
        bs_s = jnp.tile(b_spatial[l][:, :n_new].T, (db, 1))
        o_c, vn = _gmlp(hc, c_ln_g[l], c_ln_b[l], ws_s, bs_s, want_vn=True)
        xs = merge_ffn(xs, o_a, o_b, o_c, hc, l)
        outs["sa_i"].append(hkiwi[:, :IDX_DIM].reshape(db, n_new, IDX_DIM))
        outs["sc_v"].append(vn.reshape(db, n_new, cw))

    y_prompt = _rms_norm(xp, norm_f_g, F32).reshape(batch, seq, d_model)
    y_sample = _rms_norm(xs, norm_f_g, F32).reshape(db, n_new, d_model)

    def kv_result(which, name):
        lead = (depth, batch, seq) if which == "p" else (depth, db, n_new)
        return kv_bufs[which, name].reshape(lead + (n_kv, HEAD_DIM))

    return (y_prompt, y_sample, kv_result("p", "ka"), kv_result("p", "va"), jnp.stack(outs["pa_i"]),
            kv_result("p", "kb"), kv_result("p", "vb"), kv_result("s", "ka"), kv_result("s", "va"),
            jnp.stack(outs["sa_i"]), kv_result("s", "kb"), kv_result("s", "vb"), jnp.stack(outs["sc_v"]))
```

```python
import functools
import math

import jax
import jax.numpy as jnp
from jax import lax
from jax.experimental import pallas as pl
from jax.experimental.pallas import tpu as pltpu

HEAD_DIM = 128
ROPE_THETA = 500000.0
IDX_DIM = 64
IDX_TOPK = 256
MOBA_BLOCK = 256
MOBA_TOPK = 3
CHUNK = 128
C_GROUP_DIM = 128
N_BRANCH = 3
EPS = 1e-6
LANES = 128
INT32_MIN = -(2 ** 31)
PAGES_PER_STEP = 16
Q_SCALE = HEAD_DIM ** -0.5 * math.log2(math.e)

F32 = jnp.float32
BF16 = jnp.bfloat16
NEG_INF = float("-inf")
VMEM_LIMIT = 56 * 1024 * 1024

_NT = (((1,), (1,)), ((), ()))


def _params(sem):
    return pltpu.CompilerParams(dimension_semantics=sem, vmem_limit_bytes=VMEM_LIMIT)


def _rms_kernel(x_ref, g_ref, o_ref):
    x = x_ref[...]
    y = x * lax.rsqrt(jnp.mean(x * x, axis=-1, keepdims=True) + EPS)
    o_ref[...] = (y * g_ref[...]).astype(o_ref.dtype)


def _rms_norm(x, g, out_dtype):
    m, d = x.shape
    tm = min(1024, m)
    return pl.pallas_call(
        _rms_kernel,
        out_shape=jax.ShapeDtypeStruct((m, d), out_dtype),
        grid=(m // tm,),
        in_specs=[pl.BlockSpec((tm, d), lambda i: (i, 0)),
                  pl.BlockSpec((1, d), lambda i: (0, 0))],
        out_specs=pl.BlockSpec((tm, d), lambda i: (i, 0)),
        compiler_params=_params(("parallel",)),
        name="rms_norm",
    )(x, g.reshape(1, d))


def _rope_epilogue(r, c, sa, sb, half):
    outs = []
    for s in range(r.shape[1] // LANES):
        a = r[:, s * LANES:(s + 1) * LANES]
        outs.append(a * c + pltpu.roll(a, LANES - half, 1) * sa + pltpu.roll(a, half, 1) * sb)
    return outs[0] if len(outs) == 1 else jnp.concatenate(outs, axis=1)


_N_EXTRA = {"none": 0, "relu2": 0, "rope": 3, "residual": 1}


def _epilogue(r, epi, extra, half, outs, out_scale=1.0):
    if epi == "rope":
        r = _rope_epilogue(r, extra[0][...], extra[1][...], extra[2][...], half)
        if out_scale != 1.0:
            r = r * out_scale
    elif epi == "relu2":
        r = jnp.square(jnp.maximum(r, 0.0))
    elif epi == "residual":
        r = extra[0][...] + r
    for o_ref in outs:
        o_ref[...] = r.astype(o_ref.dtype)
    return r


def _mm_ws_kernel(*refs, epi, n_out, half, w_rows_are_n, out_scale, n_kv, has_buf, want_means):
    a_ref, w_ref = refs[0], refs[1]
    extra = refs[2:2 + _N_EXTRA[epi]]
    first_out = 2 + _N_EXTRA[epi] + (1 if has_buf else 0)
    outs = refs[first_out:first_out + n_out]
    wbf_s = refs[-1]

    @pl.when(pl.program_id(1) == 0)
    def _():
        wbf_s[...] = (w_ref[0] if w_rows_are_n else w_ref[...]).astype(BF16)

    if w_rows_are_n:
        part = lax.dot_general(a_ref[...], wbf_s[...], _NT, preferred_element_type=F32)
    else:
        part = jnp.dot(a_ref[...], wbf_s[...], preferred_element_type=F32)
    r = _epilogue(part, epi, extra, half, outs, out_scale)
    if n_kv:
        kv_ref = refs[first_out + n_out]
        tm = r.shape[0]
        for kv in range(n_kv):
            kv_ref[pl.ds(kv, tm, stride=n_kv), :] = r[:, kv * HEAD_DIM:(kv + 1) * HEAD_DIM]
    if want_means:
        mean_ref = refs[first_out + n_out + 1]
        for g in range(r.shape[0] // MOBA_BLOCK):
            mean_ref[g:g + 1, :] = jnp.mean(r[g * MOBA_BLOCK:(g + 1) * MOBA_BLOCK, :], axis=0, keepdims=True)


def _matmul_ws(a, w_stack, layer, out_dtypes, epi="none", extra=(), half=0, n_off=0, n=None, w_rows_are_n=False,
               out_scale=1.0, kv_out=None, want_means=False, tm=1024, tn=1024):
    m, kdim = a.shape
    n_tot = w_stack.shape[1] if w_rows_are_n else w_stack.shape[2]
    n = n_tot - n_off if n is None else n
    tm, tn = min(tm, m), min(tn, n)
    if epi == "rope":
        tm = min(tm, extra[0].shape[0])
    assert m % tm == 0 and n % tn == 0 and n_off + n <= n_tot, (a.shape, w_stack.shape, tm, tn)
    in_specs = [pl.BlockSpec((tm, kdim), lambda j, i: (i, 0))]
    if w_rows_are_n:
        assert n_off % 8 == 0 and n_off + (n // tn) * tn <= n_tot
        in_specs.append(pl.BlockSpec((pl.Element(1), pl.Element(tn), pl.Element(kdim)),
                                     lambda j, i: (layer, pl.multiple_of(n_off + j * tn, 8), 0)))
        w_scratch = pltpu.VMEM((tn, kdim), BF16)
    else:
        assert n_off % tn == 0
        in_specs.append(pl.BlockSpec((None, kdim, tn), lambda j, i: (layer, 0, n_off // tn + j)))
        w_scratch = pltpu.VMEM((kdim, tn), BF16)
    if epi == "rope":
        nt = extra[0].shape[0] // tm
        in_specs += [pl.BlockSpec((tm, LANES), lambda j, i: (i % nt, 0))] * 3
    elif epi == "residual":
        in_specs += [pl.BlockSpec((tm, tn), lambda j, i: (i, j))]
    out_shape = [jax.ShapeDtypeStruct((m, n), dt) for dt in out_dtypes]
    out_specs = [pl.BlockSpec((tm, tn), lambda j, i: (i, j)) for _ in out_dtypes]
    operands, aliases, n_kv = [a, w_stack, *extra], {}, 0
    if kv_out is not None:
        buf, n_layers, n_kv = kv_out
        assert n == tn == n_kv * HEAD_DIM
        if buf is not None:
            aliases = {len(operands): len(out_shape)}
            operands.append(buf)
            in_specs.append(pl.BlockSpec(memory_space=pl.ANY))
        out_shape.append(jax.ShapeDtypeStruct((n_layers * m * n_kv, HEAD_DIM), F32))
        out_specs.append(pl.BlockSpec((tm * n_kv, HEAD_DIM), lambda j, i: (layer * (m // tm) + i, 0)))
    if want_means:
        assert kv_out is not None and tm % MOBA_BLOCK == 0
        out_shape.append(jax.ShapeDtypeStruct((m // tm, tm // MOBA_BLOCK, n), F32))
        out_specs.append(pl.BlockSpec((None, tm // MOBA_BLOCK, n), lambda j, i: (i, 0, 0)))
    outs = pl.pallas_call(
        functools.partial(_mm_ws_kernel, epi=epi, n_out=len(out_dtypes), half=half, w_rows_are_n=w_rows_are_n,
                          out_scale=out_scale, n_kv=n_kv, has_buf=bool(aliases), want_means=want_means),
        out_shape=out_shape,
        grid=(n // tn, m // tm),
        in_specs=in_specs,
        out_specs=out_specs,
        scratch_shapes=[w_scratch],
        input_output_aliases=aliases,
        compiler_params=_params(("parallel", "arbitrary")),
        name="matmul_" + epi,
    )(*operands)
    return tuple(outs)


def _mm_kt_kernel(*refs, epi, nk, n_out, half):
    a_ref, w_ref = refs[0], refs[1]
    extra = refs[2:2 + _N_EXTRA[epi]]
    outs = refs[2 + _N_EXTRA[epi]:2 + _N_EXTRA[epi] + n_out]
    acc_ref = refs[-1]
    k = pl.program_id(2)
    part = jnp.dot(a_ref[...], w_ref[...].astype(BF16), preferred_element_type=F32)

    @pl.when(k == 0)
    def _():
        acc_ref[...] = part

    @pl.when(k > 0)
    def _():
        acc_ref[...] += part

    @pl.when(k == nk - 1)
    def _():
        _epilogue(acc_ref[...], epi, extra, half, outs)


def _matmul_kt(a, w_stack, layer, out_dtypes, epi="none", extra=(), tm=1024, tn=1024, tk=2048):
    m, kdim = a.shape
    n = w_stack.shape[2]
    tm, tn, tk = min(tm, m), min(tn, n), min(tk, kdim)
    assert m % tm == 0 and n % tn == 0 and kdim % tk == 0 and kdim > tk, (a.shape, w_stack.shape, tm, tn, tk)
    nk = kdim // tk
    in_specs = [pl.BlockSpec((tm, tk), lambda i, j, k: (i, k)),
                pl.BlockSpec((None, tk, tn), lambda i, j, k: (layer, k, j))]
    if epi == "residual":
        in_specs += [pl.BlockSpec((tm, tn), lambda i, j, k: (i, j))]
    outs = pl.pallas_call(
        functools.partial(_mm_kt_kernel, epi=epi, nk=nk, n_out=len(out_dtypes), half=0),
        out_shape=[jax.ShapeDtypeStruct((m, n), dt) for dt in out_dtypes],
        grid=(m // tm, n // tn, nk),
        in_specs=in_specs,
        out_specs=[pl.BlockSpec((tm, tn), lambda i, j, k: (i, j)) for _ in out_dtypes],
        scratch_shapes=[pltpu.VMEM((tm, tn), F32)],
        compiler_params=_params(("parallel", "parallel", "arbitrary")),
        name="matmul_kt_" + epi,
    )(a, w_stack, *extra)
    return tuple(outs)


def _ordered_key(score):
    bits = pltpu.bitcast(score, jnp.int32)
    return jnp.where(bits < 0, bits ^ jnp.int32(0x7FFFFFFF), bits)


def _kth_largest_key(count_ge, shape, n_sel):
    def body(b, t):
        cand = t + lax.shift_left(jnp.int32(1), jnp.int32(31) - b)
        return jnp.where(count_ge(cand) >= float(n_sel), cand, t)

    return lax.fori_loop(0, 32, body, jnp.full(shape, INT32_MIN, jnp.int32))


def _topk_lanes(g, valid, n_sel):
    lane = lax.broadcasted_iota(jnp.int32, g.shape, 1).astype(F32)
    gm = jnp.where(valid, g, NEG_INF)
    sel = jnp.zeros(g.shape, jnp.bool_)
    for _ in range(n_sel):
        mx = jnp.max(gm, axis=1, keepdims=True)
        first = jnp.min(jnp.where(gm == mx, lane, float(LANES)), axis=1, keepdims=True)
        pick = lane == first
        sel = sel | pick
        gm = jnp.where(pick, NEG_INF, gm)
    return sel & valid


def _softmax_pv(logits2, v):
    p = jnp.exp2(logits2 - jnp.max(logits2, axis=1, keepdims=True))
    l = jnp.sum(p, axis=1, keepdims=True)
    return jnp.dot(p.astype(BF16), v, preferred_element_type=F32) / l


def _softmax_pv_t(logits2_t, vt):
    p = jnp.exp2(logits2_t - jnp.max(logits2_t, axis=0, keepdims=True))
    l = jnp.sum(p, axis=0, keepdims=True)
    ot = jnp.dot(vt, p.astype(BF16), preferred_element_type=F32)
    return jnp.transpose(ot / l)


def _topk_rows(g, valid, n_sel):
    rid = lax.broadcasted_iota(jnp.int32, g.shape, 0).astype(F32)
    gm = jnp.where(valid, g, NEG_INF)
    sel = jnp.zeros(g.shape, jnp.bool_)
    for _ in range(n_sel):
        mx = jnp.max(gm, axis=0, keepdims=True)
        first = jnp.min(jnp.where(gm == mx, rid, float(g.shape[0])), axis=0, keepdims=True)
        pick = rid == first
        sel = sel | pick
        gm = jnp.where(pick, NEG_INF, gm)
    return sel & valid


def _width_classes(seq, tile):
    widths = sorted({w for w in (seq // 4, seq // 2, seq) if w % tile == 0 and w >= tile})
    classes, lo = [], 0
    for w in widths:
        classes.append((lo, w // tile, w))
        lo = w // tile
    return classes


def _dsa_prompt_kernel(qa_ref, ka_ref, va_ref, qi_ref, ki_ref, wi_ref, o_ref, ki_s, qcat_s, wt_s, key_s, bias_s,
                       tie_s, *, tq, seq, n_sel, n_heads, n_idx_heads, groups):
    i = pl.program_id(1)
    n_valid = i + 1
    kblk = 32

    @pl.when(i == 0)
    def _():
        ki_s[...] = ki_ref[:, :IDX_DIM].astype(BF16)

    for h in range(n_idx_heads):
        qcat_s[h * tq:(h + 1) * tq, :] = qi_ref[:, h * IDX_DIM:(h + 1) * IDX_DIM]
    wt_s[...] = jnp.transpose(wi_ref[...]) * ((IDX_DIM ** -0.5) * (n_idx_heads ** -0.5))

    krow = lax.broadcasted_iota(jnp.int32, (tq, tq), 0)
    qcol = lax.broadcasted_iota(jnp.int32, (tq, tq), 1)

    def visible(c):
        return krow <= qcol + jnp.where(c == i, 0, tq)

    for c in range(seq // tq):
        @pl.when(c < n_valid)
        def _(c=c):
            d_all = lax.dot_general(ki_s[c * tq:(c + 1) * tq, :], qcat_s[...], _NT,
                                    preferred_element_type=F32)
            for kb in range(tq // kblk):
                rows = slice(kb * kblk, (kb + 1) * kblk)
                s = None
                for h in range(n_idx_heads):
                    term = jnp.maximum(d_all[rows, h * tq:(h + 1) * tq], 0.0) * wt_s[IDX_DIM + h:IDX_DIM + h + 1, :]
                    s = term if s is None else s + term
                key_s[c, rows, :] = _ordered_key(jnp.where(visible(c)[rows, :], s, NEG_INF))

    def count_ge(cand):
        def body(c, acc):
            hit = (key_s[c] >= cand).astype(F32)
            return acc + jnp.sum(hit.reshape(tq // 8, 8, tq), axis=0)

        acc = lax.fori_loop(0, n_valid, body, jnp.zeros((8, tq), F32))
        return jnp.sum(acc, axis=0, keepdims=True)

    thr = _kth_largest_key(count_ge, (1, tq), n_sel)
    need = float(n_sel) - count_ge(thr + 1)
    prefix_ones = jnp.where(qcol <= krow, 1.0, 0.0).astype(BF16)
    tie_s[...] = jnp.zeros_like(tie_s)

    def kv_cols(h):
        return slice((h // groups) * HEAD_DIM, (h // groups + 1) * HEAD_DIM)

    def attend(width):
        for c in range(width // tq):
            @pl.when(c < n_valid)
            def _(c=c):
                k, vis = key_s[c], visible(c)
                tie = jnp.where((k == thr) & vis, 1.0, 0.0)
                rank = jnp.dot(prefix_ones, tie.astype(BF16), preferred_element_type=F32) + tie_s[...]
                ok = ((k > thr) & vis) | ((tie > 0.0) & (rank <= need))
                bias_s[:, c * tq:(c + 1) * tq] = jnp.transpose(jnp.where(ok, 0.0, NEG_INF))
                tie_s[...] = tie_s[...] + jnp.sum(tie, axis=0, keepdims=True)

            @pl.when(c >= n_valid)
            def _(c=c):
                bias_s[:, c * tq:(c + 1) * tq] = jnp.full((tq, tq), NEG_INF, F32)

        def qk(h):
            return lax.dot_general(qa_ref[:, h * HEAD_DIM:(h + 1) * HEAD_DIM], ka_ref[0:width, kv_cols(h)], _NT,
                                   preferred_element_type=F32)

        raw = qk(0)
        for h in range(n_heads):
            nxt = qk(h + 1) if h + 1 < n_heads else None
            o = _softmax_pv(raw + bias_s[:, 0:width], va_ref[0:width, kv_cols(h)])
            o_ref[:, h * HEAD_DIM:(h + 1) * HEAD_DIM] = o.astype(o_ref.dtype)
            raw = nxt

    for lo, hi, width in _width_classes(seq, tq):
        @pl.when((i >= lo) & (i < hi))
        def _(width=width):
            attend(width)


def _dsa_prompt(hq, hk_bf, hv_bf, hqi, hkiwi, *, batch, seq, n_heads, n_idx_heads, groups):
    tq = min(256, seq)
    assert seq % tq == 0 and tq % LANES == 0
    nq = seq // tq
    aw = n_heads * HEAD_DIM
    kvw = aw // groups
    n_sel = min(IDX_TOPK, seq // 4)
    kern = functools.partial(_dsa_prompt_kernel, tq=tq, seq=seq, n_sel=n_sel, n_heads=n_heads,
                             n_idx_heads=n_idx_heads, groups=groups)
    return pl.pallas_call(
        kern,
        out_shape=jax.ShapeDtypeStruct((batch * seq, aw), BF16),
        grid=(batch, nq),
        in_specs=[pl.BlockSpec((tq, aw), lambda b, i: (b * nq + i, 0)),
                  pl.BlockSpec((seq, kvw), lambda b, i: (b, 0)),
                  pl.BlockSpec((seq, kvw), lambda b, i: (b, 0)),
                  pl.BlockSpec((tq, n_idx_heads * IDX_DIM), lambda b, i: (b * nq + i, 0)),
                  pl.BlockSpec((seq, LANES), lambda b, i: (b, 0)),
                  pl.BlockSpec((tq, LANES), lambda b, i: (b * nq + i, 0))],
        out_specs=pl.BlockSpec((tq, aw), lambda b, i: (b * nq + i, 0)),
        scratch_shapes=[pltpu.VMEM((seq, IDX_DIM), BF16), pltpu.VMEM((n_idx_heads * tq, IDX_DIM), BF16),
                        pltpu.VMEM((LANES, tq), F32), pltpu.VMEM((nq, tq, tq), jnp.int32),
                        pltpu.VMEM((tq, seq), F32), pltpu.VMEM((1, tq), F32)],
        compiler_params=_params(("parallel", "arbitrary")),
        name="dsa_prompt",
    )(hq, hk_bf, hv_bf, hqi, hkiwi, hkiwi)


def _moba_prompt_kernel(qb_ref, kb_ref, kmean_ref, vb_ref, o_ref, km_s, vt_s, *, nb, n_sel, n_heads, groups):
    i = pl.program_id(1)
    blk = MOBA_BLOCK

    @pl.when(i == 0)
    def _():
        km_s[...] = jnp.zeros_like(km_s)
        per = kmean_ref.shape[1]
        for j in range(nb):
            km_s[j:j + 1, :] = kmean_ref[j // per, j % per:j % per + 1, :]
        for kv in range(n_heads // groups):
            v = vb_ref[:, kv * HEAD_DIM:(kv + 1) * HEAD_DIM].astype(F32)
            vt_s[kv] = jnp.transpose(v).astype(BF16)

    past = lax.broadcasted_iota(jnp.int32, (LANES, blk), 0) < i
    tril_bias = jnp.where(lax.broadcasted_iota(jnp.int32, (blk, blk), 0)
                          <= lax.broadcasted_iota(jnp.int32, (blk, blk), 1), 0.0, NEG_INF)

    def kv_cols(h):
        return slice((h // groups) * HEAD_DIM, (h // groups + 1) * HEAD_DIM)

    def attend(width):
        def qk(h):
            q = qb_ref[:, h * HEAD_DIM:(h + 1) * HEAD_DIM]
            gate = lax.dot_general(km_s[:, kv_cols(h)].astype(BF16), q, _NT, preferred_element_type=F32)
            return gate, lax.dot_general(kb_ref[0:width, kv_cols(h)], q, _NT, preferred_element_type=F32)

        cur = qk(0)
        for h in range(n_heads):
            nxt = qk(h + 1) if h + 1 < n_heads else None
            gate, raw = cur
            sel_bias = jnp.where(_topk_rows(gate, past, n_sel), 0.0, NEG_INF)
            pieces = []
            for j in range(width // blk):
                bias = jnp.where(j == i, tril_bias, sel_bias[j:j + 1, :])
                pieces.append(raw[j * blk:(j + 1) * blk, :] + bias)
            o = _softmax_pv_t(jnp.concatenate(pieces, axis=0), vt_s[h // groups, :, 0:width])
            o_ref[:, h * HEAD_DIM:(h + 1) * HEAD_DIM] = o.astype(o_ref.dtype)
            cur = nxt

    for lo, hi, width in _width_classes(nb * blk, blk):
        @pl.when((i >= lo) & (i < hi))
        def _(width=width):
            attend(width)


def _moba_prompt(hq, hk_bf, k_means, hv_bf, *, batch, seq, n_heads, groups):
    blk = MOBA_BLOCK
    assert seq % blk == 0
    nb = seq // blk
    g_per_seq = k_means.shape[0] // batch
    assert 1 <= nb <= LANES and g_per_seq * k_means.shape[1] == nb
    aw = n_heads * HEAD_DIM
    kvw = aw // groups
    kern = functools.partial(_moba_prompt_kernel, nb=nb, n_sel=min(MOBA_TOPK, nb), n_heads=n_heads,
                             groups=groups)
    return pl.pallas_call(
        kern,
        out_shape=jax.ShapeDtypeStruct((batch * seq, aw), BF16),
        grid=(batch, nb),
        in_specs=[pl.BlockSpec((blk, aw), lambda b, i: (b * nb + i, 0)),
                  pl.BlockSpec((seq, kvw), lambda b, i: (b, 0)),
                  pl.BlockSpec((g_per_seq, k_means.shape[1], kvw), lambda b, i: (b, 0, 0)),
                  pl.BlockSpec((seq, kvw), lambda b, i: (b, 0))],
        out_specs=pl.BlockSpec((blk, aw), lambda b, i: (b * nb + i, 0)),
        scratch_shapes=[pltpu.VMEM((LANES, kvw), F32), pltpu.VMEM((n_heads // groups, HEAD_DIM, seq), BF16)],
        compiler_params=_params(("parallel", "arbitrary")),
        name="moba_prompt",
    )(hq, hk_bf, k_means, hv_bf)


def _gmlp_kernel(cu_ref, cv_ref, g_ref, b_ref, ws_ref, bs_ref, o_ref, *vn_out, n_groups):
    r = ws_ref.shape[1]
    for ch in range(cv_ref.shape[0] // r):
        rows = slice(ch * r, (ch + 1) * r)
        v = cv_ref[rows, :]
        mu = jnp.mean(v, axis=-1, keepdims=True)
        vc = v - mu
        vn = vc * lax.rsqrt(jnp.mean(vc * vc, axis=-1, keepdims=True) + EPS) * g_ref[...] + b_ref[...]
        if vn_out:
            vn_out[0][rows, :] = vn
        for g in range(n_groups):
            cols = slice(g * C_GROUP_DIM, (g + 1) * C_GROUP_DIM)
            z = jnp.dot(ws_ref[g], vn[:, cols].astype(BF16), preferred_element_type=F32) + bs_ref[:, g:g + 1]
            o_ref[rows, cols] = (cu_ref[rows, cols] * z).astype(o_ref.dtype)


def _gmlp(hc, ln_g, ln_b, ws, bs, *, want_vn):
    m = hc.shape[0]
    n_groups, chunk = ws.shape[0], ws.shape[1]
    cw = n_groups * C_GROUP_DIM
    r = min(m, 8 * chunk)
    assert m % r == 0 and r % chunk == 0
    out_shape = [jax.ShapeDtypeStruct((m, cw), BF16)]
    out_specs = [pl.BlockSpec((r, cw), lambda i: (i, 0))]
    if want_vn:
        out_shape.append(jax.ShapeDtypeStruct((m, cw), F32))
        out_specs.append(pl.BlockSpec((r, cw), lambda i: (i, 0)))
    return pl.pallas_call(
        functools.partial(_gmlp_kernel, n_groups=n_groups),
        out_shape=out_shape,
        grid=(m // r,),
        in_specs=[pl.BlockSpec((r, cw), lambda i: (i, 0)),
                  pl.BlockSpec((r, cw), lambda i: (i, 1)),
                  pl.BlockSpec((1, cw), lambda i: (0, 0)),
                  pl.BlockSpec((1, cw), lambda i: (0, 0)),
                  pl.BlockSpec((n_groups, chunk, chunk), lambda i: (0, 0, 0)),
                  pl.BlockSpec((chunk, n_groups), lambda i: (0, 0))],
        out_specs=out_specs,
        compiler_params=_params(("parallel",)),
        name="gmlp",
    )(hc, hc, ln_g.reshape(1, cw), ln_b.reshape(1, cw), ws, bs)


def _merge_kernel(oa_ref, ob_ref, oc_ref, wb_ref, g0_ref, g1_ref, g2_ref, o_ref, wbf_s):
    @pl.when(pl.program_id(1) == 0)
    def _():
        wbf_s[...] = wb_ref[...].astype(BF16)

    acc = None
    for o_n, g_n, n in ((oa_ref, g0_ref, 0), (ob_ref, g1_ref, 1), (oc_ref, g2_ref, 2)):
        br = jnp.dot(o_n[...], wbf_s[n], preferred_element_type=F32)
        term = (1.0 / (1.0 + jnp.exp(-g_n[...]))) * br
        acc = term if acc is None else acc + term
    o_ref[...] = acc.astype(o_ref.dtype)


def _merge(o_a, o_b, o_c, wb_stack, layer, hc, *, d_model, gate_col0):
    m, aw = o_a.shape
    tm, tn = min(1024, m), 512
    nj = d_model // tn
    g0 = gate_col0 // tn
    gate_specs = [pl.BlockSpec((tm, tn), functools.partial(lambda j, i, n: (i, g0 + n * nj + j), n=n))
                  for n in range(N_BRANCH)]
    return pl.pallas_call(
        _merge_kernel,
        out_shape=jax.ShapeDtypeStruct((m, d_model), BF16),
        grid=(nj, m // tm),
        in_specs=[pl.BlockSpec((tm, aw), lambda j, i: (i, 0))] * 3
        + [pl.BlockSpec((None, N_BRANCH, aw, tn), lambda j, i: (layer, 0, 0, j))] + gate_specs,
        out_specs=pl.BlockSpec((tm, tn), lambda j, i: (i, j)),
        scratch_shapes=[pltpu.VMEM((N_BRANCH, aw, tn), BF16)],
        compiler_params=_params(("parallel", "arbitrary")),
        name="merge",
    )(o_a, o_b, o_c, wb_stack, hc, hc, hc)


def _page_specs(block, layer, n_pages, per_step):
    def page_map(b, s, pt, slot):
        return (layer, pt[b * n_pages + s * per_step + slot], 0, 0)

    return [pl.BlockSpec(block, functools.partial(page_map, slot=slot)) for slot in range(per_step)]


def _idx_scores(qi, wi, k, n_new, n_idx_heads, k_is_t=False):
    if k_is_t:
        d = jnp.dot(qi, k, preferred_element_type=F32) * (IDX_DIM ** -0.5)
    else:
        d = lax.dot_general(qi, k, _NT, preferred_element_type=F32) * (IDX_DIM ** -0.5)
    dw = jnp.maximum(d, 0.0) * wi
    acc = dw[0:8]
    for r in range(8, n_idx_heads * n_new, 8):
        acc = acc + dw[r:r + 8]
    s = acc[0:n_new]
    for r in range(n_new, 8, n_new):
        s = s + acc[r:r + n_new]
    return s * (n_idx_heads ** -0.5)


def _dsa_sample_score_kernel(pt_ref, *refs, per_step, n_new, n_idx_heads):
    pages = refs[:per_step]
    qi_ref, wi_ref, o_ref = refs[per_step:]
    keys_t = jnp.concatenate([p[...] for p in pages], axis=1) if per_step > 1 else pages[0][...]
    o_ref[...] = _idx_scores(qi_ref[...], wi_ref[...], keys_t, n_new, n_idx_heads, k_is_t=True)


def _dsa_sample_scores(pt_flat, cache_idx_t, layer, qi_rows, wi_rows, *, dec_batch, n_pages, n_new, n_idx_heads):
    page = cache_idx_t.shape[3]
    assert page == LANES and 8 % n_new == 0 and (n_idx_heads * n_new) % 8 == 0
    per_step = min(PAGES_PER_STEP, n_pages)
    assert n_pages % per_step == 0
    rows = n_idx_heads * n_new
    kern = functools.partial(_dsa_sample_score_kernel, per_step=per_step, n_new=n_new, n_idx_heads=n_idx_heads)
    grid_spec = pltpu.PrefetchScalarGridSpec(
        num_scalar_prefetch=1,
        grid=(dec_batch, n_pages // per_step),
        in_specs=_page_specs((None, None, IDX_DIM, page), layer, n_pages, per_step)
        + [pl.BlockSpec((None, rows, IDX_DIM), lambda b, s, pt: (b, 0, 0)),
           pl.BlockSpec((None, rows, 1), lambda b, s, pt: (b, 0, 0))],
        out_specs=pl.BlockSpec((None, n_new, per_step * LANES), lambda b, s, pt: (b, 0, s)),
    )
    return pl.pallas_call(
        kern,
        out_shape=jax.ShapeDtypeStruct((dec_batch, n_new, n_pages * LANES), F32),
        grid_spec=grid_spec,
        compiler_params=_params(("parallel", "arbitrary")),
        name="dsa_sample_scores",
    )(pt_flat, *([cache_idx_t] * per_step), qi_rows, wi_rows)


def _select_kernel(s_ref, qi_ref, wi_ref, kn_ref, o_ref, on_ref, key_s, keyn_s, *, n_sel, n_new, n_idx_heads, db):
    rows = db * n_new
    t = lax.broadcasted_iota(jnp.int32, (rows, LANES), 0) & (n_new - 1)
    c = lax.broadcasted_iota(jnp.int32, (rows, LANES), 1)
    new_ok = c <= t
    key_s[...] = _ordered_key(s_ref[...])
    for b in range(db):
        s_new = _idx_scores(qi_ref[b], wi_ref[b], kn_ref[b], n_new, n_idx_heads)
        keyn_s[b * n_new:(b + 1) * n_new, :] = _ordered_key(jnp.where(new_ok[0:n_new], s_new, NEG_INF))

    def count_ge(cand):
        return (jnp.sum((key_s[...] >= cand).astype(F32), axis=1, keepdims=True)
                + jnp.sum((keyn_s[...] >= cand).astype(F32), axis=1, keepdims=True))

    thr = _kth_largest_key(count_ge, (rows, 1), n_sel)
    need = float(n_sel) - count_ge(thr + 1)
    prefix_ones = jnp.where(lax.broadcasted_iota(jnp.int32, (LANES, LANES), 0)
                            <= lax.broadcasted_iota(jnp.int32, (LANES, LANES), 1), 1.0, 0.0).astype(BF16)
    seen = jnp.zeros((rows, 1), F32)

    def pick(k, vis, seen):
        tie = jnp.where((k == thr) & vis, 1.0, 0.0)
        rank = jnp.dot(tie.astype(BF16), prefix_ones, preferred_element_type=F32) + seen
        ok = ((k > thr) & vis) | ((tie > 0.0) & (rank <= need))
        return jnp.where(ok, 1.0, 0.0), seen + jnp.sum(tie, axis=1, keepdims=True)

    every = c >= 0
    for blk in range(key_s.shape[1] // LANES):
        cols = slice(blk * LANES, (blk + 1) * LANES)
        o_ref[:, cols], seen = pick(key_s[:, cols], every, seen)
    on_ref[...], seen = pick(keyn_s[...], new_ok, seen)


def _dsa_sample_select(scores, qi_rows, wi_rows, ki_new_pad, n_sel, *, n_idx_heads):
    db, n_new, length = scores.shape
    rows = db * n_new
    assert n_new & (n_new - 1) == 0
    whole = pl.BlockSpec(memory_space=pltpu.VMEM)
    sel, sel_new = pl.pallas_call(
        functools.partial(_select_kernel, n_sel=n_sel, n_new=n_new, n_idx_heads=n_idx_heads, db=db),
        out_shape=[jax.ShapeDtypeStruct((rows, length), F32), jax.ShapeDtypeStruct((rows, LANES), F32)],
        in_specs=[whole] * 4,
        out_specs=[whole] * 2,
        scratch_shapes=[pltpu.VMEM((rows, length), jnp.int32), pltpu.VMEM((rows, LANES), jnp.int32)],
        compiler_params=pltpu.CompilerParams(vmem_limit_bytes=VMEM_LIMIT),
        name="dsa_sample_select",
    )(scores.reshape(rows, length), qi_rows, wi_rows, ki_new_pad)
    return sel.reshape(db, n_new, length), sel_new.reshape(db, n_new, LANES)


def _head_rows(page_refs, kv, n_kv):
    parts = [r[pl.ds(kv, LANES, stride=n_kv), :] for r in page_refs]
    return parts[0] if len(parts) == 1 else jnp.concatenate(parts, axis=0)


def _dsa_sample_attn_kernel(pt_ref, *refs, per_step, n_steps, n_new, n_kv, groups):
    k_pages, v_pages = refs[:per_step], refs[per_step:2 * per_step]
    kn_ref, vn_ref, q_ref, sel_ref, seln_ref, o_ref, m_s, l_s, acc_s = refs[2 * per_step:]
    step = pl.program_id(1)
    rpk = groups * n_new

    @pl.when(step == 0)
    def _():
        m_s[...] = jnp.full_like(m_s, NEG_INF)
        l_s[...] = jnp.zeros_like(l_s)
        acc_s[...] = jnp.zeros_like(acc_s)

    def update(get_k, get_v, sel):
        mask = jnp.concatenate([sel] * groups, axis=0) > 0.0
        rows = [slice(kv * rpk, (kv + 1) * rpk) for kv in range(n_kv)]
        logits = [lax.dot_general(q_ref[rows[kv], :], get_k(kv), _NT, preferred_element_type=F32)
                  for kv in range(n_kv)]
        probs, alphas = [], []
        for kv in range(n_kv):
            s = jnp.where(mask, logits[kv], NEG_INF)
            m_old = m_s[rows[kv], :]
            m_new = jnp.maximum(m_old, jnp.max(s, axis=1, keepdims=True))
            m_safe = jnp.where(m_new == NEG_INF, 0.0, m_new)
            alpha = jnp.exp2(m_old - m_safe)
            pr = jnp.exp2(s - m_safe)
            m_s[rows[kv], :] = m_new
            l_s[rows[kv], :] = alpha * l_s[rows[kv], :] + jnp.sum(pr, axis=1, keepdims=True)
            probs.append(pr)
            alphas.append(alpha)
        for kv in range(n_kv):
            acc_s[rows[kv], :] = alphas[kv] * acc_s[rows[kv], :] + jnp.dot(
                probs[kv], get_v(kv), preferred_element_type=F32)

    update(lambda kv: _head_rows(k_pages, kv, n_kv), lambda kv: _head_rows(v_pages, kv, n_kv), sel_ref[...])

    @pl.when(step == n_steps - 1)
    def _():
        update(lambda kv: kn_ref[kv], lambda kv: vn_ref[kv], seln_ref[...])
        o_ref[...] = acc_s[...] / l_s[...]


def _dsa_sample_attn(pt_flat, cache_k, cache_v, layer, k_new_pad, v_new_pad, q_rows, sel, sel_new, *, dec_batch,
                     n_pages, n_new, n_kv, groups):
    rows = n_kv * groups * n_new
    prow = cache_k.shape[2]
    per_step = min(PAGES_PER_STEP, n_pages)
    assert n_pages % per_step == 0
    n_steps = n_pages // per_step
    kern = functools.partial(_dsa_sample_attn_kernel, per_step=per_step, n_steps=n_steps, n_new=n_new, n_kv=n_kv,
                             groups=groups)
    page_specs = _page_specs((None, None, prow, HEAD_DIM), layer, n_pages, per_step)
    grid_spec = pltpu.PrefetchScalarGridSpec(
        num_scalar_prefetch=1,
        grid=(dec_batch, n_steps),
        in_specs=page_specs + page_specs
        + [pl.BlockSpec((None, n_kv, LANES, HEAD_DIM), lambda b, s, pt: (b, 0, 0, 0)),
           pl.BlockSpec((None, n_kv, LANES, HEAD_DIM), lambda b, s, pt: (b, 0, 0, 0)),
           pl.BlockSpec((None, rows, HEAD_DIM), lambda b, s, pt: (b, 0, 0)),
           pl.BlockSpec((None, n_new, per_step * LANES), lambda b, s, pt: (b, 0, s)),
           pl.BlockSpec((None, n_new, LANES), lambda b, s, pt: (b, 0, 0))],
        out_specs=pl.BlockSpec((None, rows, HEAD_DIM), lambda b, s, pt: (b, 0, 0)),
        scratch_shapes=[pltpu.VMEM((rows, 1), F32), pltpu.VMEM((rows, 1), F32),
                        pltpu.VMEM((rows, HEAD_DIM), F32)],
    )
    return pl.pallas_call(
        kern,
        out_shape=jax.ShapeDtypeStruct((dec_batch, rows, HEAD_DIM), F32),
        grid_spec=grid_spec,
        compiler_params=_params(("parallel", "arbitrary")),
        name="dsa_sample_attn",
    )(pt_flat, *([cache_k] * per_step), *([cache_v] * per_step), k_new_pad, v_new_pad, q_rows, sel, sel_new)


def _moba_sample_kernel(pt_ref, *refs, bps, n_steps, nb, n_sel, n_new, n_kv, groups, past_len):
    k_pages, v_pages = refs[:2 * bps], refs[2 * bps:4 * bps]
    kn_ref, vn_ref, q_ref, o_ref, gate_s, m_s, l_s, acc_s = refs[4 * bps:]
    step = pl.program_id(1)
    rpk = groups * n_new
    rows_all = n_kv * rpk
    lane = lax.broadcasted_iota(jnp.int32, (rows_all, LANES), 1)

    @pl.when(step == 0)
    def _():
        gate_s[...] = jnp.zeros_like(gate_s)
        m_s[...] = jnp.zeros_like(m_s)
        l_s[...] = jnp.zeros_like(l_s)

    gate, m_blk, l_blk = gate_s[...], m_s[...], l_s[...]
    blk = MOBA_BLOCK
    kv_rows = [slice(kv * rpk, (kv + 1) * rpk) for kv in range(n_kv)]
    ks = [_head_rows(k_pages, kv, n_kv) for kv in range(n_kv)]
    logits = [lax.dot_general(q_ref[kv_rows[kv], :], ks[kv], _NT, preferred_element_type=F32)
              for kv in range(n_kv)]
    probs = {}
    for bb in range(bps):
        g_cols, m_cols, l_cols = [], [], []
        for kv in range(n_kv):
            kmean = jnp.mean(ks[kv][bb * blk:(bb + 1) * blk, :], axis=0, keepdims=True)
            g_cols.append(jnp.sum(q_ref[kv_rows[kv], :] * kmean, axis=1, keepdims=True))
            s = logits[kv][:, bb * blk:(bb + 1) * blk]
            m = jnp.max(s, axis=1, keepdims=True)
            pr = jnp.exp2(s - m)
            m_cols.append(m)
            l_cols.append(jnp.sum(pr, axis=1, keepdims=True))
            probs[bb, kv] = pr
        here = lane == step * bps + bb
        gate = jnp.where(here, jnp.concatenate(g_cols, axis=0), gate)
        m_blk = jnp.where(here, jnp.concatenate(m_cols, axis=0), m_blk)
        l_blk = jnp.where(here, jnp.concatenate(l_cols, axis=0), l_blk)
    for bb in range(bps):
        for kv in range(n_kv):
            v = _head_rows(v_pages[2 * bb:2 * bb + 2], kv, n_kv)
            acc_s[step * bps + bb, kv_rows[kv], :] = jnp.dot(probs[bb, kv], v, preferred_element_type=F32)
    gate_s[...] = gate
    m_s[...] = m_blk
    l_s[...] = l_blk

    @pl.when(step == n_steps - 1)
    def _():
        t_row = lax.broadcasted_iota(jnp.int32, (rpk, LANES), 0) & (n_new - 1)
        c_col = lax.broadcasted_iota(jnp.int32, (rpk, LANES), 1)
        own_ok = c_col <= t_row
        t_all = lax.broadcasted_iota(jnp.int32, (rows_all, LANES), 0) & (n_new - 1)
        own_blk = jnp.right_shift(past_len + t_all, MOBA_BLOCK.bit_length() - 1)
        valid = (lane < own_blk) & (lane < nb)
        sel = _topk_lanes(gate, valid, n_sel)
        for kv in range(n_kv):
            rows = slice(kv * rpk, (kv + 1) * rpk)
            q = q_ref[rows, :]
            s = lax.dot_general(q, kn_ref[kv], _NT, preferred_element_type=F32)
            s = jnp.where(own_ok, s, NEG_INF)
            m_o = jnp.max(s, axis=1, keepdims=True)
            m_sel = jnp.where(sel[rows, :], m_blk[rows, :], NEG_INF)
            m_all = jnp.maximum(m_o, jnp.max(m_sel, axis=1, keepdims=True))
            pr = jnp.exp2(s - m_all)
            w = jnp.where(sel[rows, :], jnp.exp2(m_sel - m_all), 0.0)
            den = jnp.sum(pr, axis=1, keepdims=True) + jnp.sum(w * l_blk[rows, :], axis=1, keepdims=True)
            num = jnp.dot(pr, vn_ref[kv], preferred_element_type=F32)
            for jb in range(nb):
                num = num + w[:, jb:jb + 1] * acc_s[jb, rows, :]
            o_ref[rows, :] = num / den


def _moba_sample(pt_flat, cache_k, cache_v, layer, k_new_pad, v_new_pad, q_rows, *, dec_batch, n_pages, n_new,
                 n_kv, groups, past_len):
    assert past_len % MOBA_BLOCK == 0 and n_new < MOBA_BLOCK and MOBA_BLOCK == 2 * LANES
    assert n_new & (n_new - 1) == 0
    nb = (past_len + n_new) // MOBA_BLOCK
    assert 1 <= nb <= LANES and n_pages == 2 * nb
    bps = min(PAGES_PER_STEP // 2, nb)
    assert nb % bps == 0
    n_steps = nb // bps
    rows = n_kv * groups * n_new
    prow = cache_k.shape[2]
    kern = functools.partial(_moba_sample_kernel, bps=bps, n_steps=n_steps, nb=nb, n_sel=min(MOBA_TOPK, nb),
                             n_new=n_new, n_kv=n_kv, groups=groups, past_len=past_len)
    page_specs = _page_specs((None, None, prow, HEAD_DIM), layer, n_pages, 2 * bps)
    grid_spec = pltpu.PrefetchScalarGridSpec(
        num_scalar_prefetch=1,
        grid=(dec_batch, n_steps),
        in_specs=page_specs + page_specs
        + [pl.BlockSpec((None, n_kv, LANES, HEAD_DIM), lambda b, s, pt: (b, 0, 0, 0)),
           pl.BlockSpec((None, n_kv, LANES, HEAD_DIM), lambda b, s, pt: (b, 0, 0, 0)),
           pl.BlockSpec((None, rows, HEAD_DIM), lambda b, s, pt: (b, 0, 0))],
        out_specs=pl.BlockSpec((None, rows, HEAD_DIM), lambda b, s, pt: (b, 0, 0)),
        scratch_shapes=[pltpu.VMEM((rows, LANES), F32), pltpu.VMEM((rows, LANES), F32),
                        pltpu.VMEM((rows, LANES), F32), pltpu.VMEM((nb, rows, HEAD_DIM), F32)],
    )
    return pl.pallas_call(
        kern,
        out_shape=jax.ShapeDtypeStruct((dec_batch, rows, HEAD_DIM), F32),
        grid_spec=grid_spec,
        compiler_params=_params(("parallel", "arbitrary")),
        name="moba_sample",
    )(pt_flat, *([cache_k] * (2 * bps)), *([cache_v] * (2 * bps)), k_new_pad, v_new_pad, q_rows)


def _rope_tables(pos, rot, width, n_rope_lanes=LANES):
    half = rot // 2
    inv = jnp.power(ROPE_THETA, -jnp.arange(half, dtype=F32) / half)
    ang = pos.astype(F32)[:, None] * inv[None, :]
    cos, sin = jnp.cos(ang), jnp.sin(ang)
    t = pos.shape[0]
    zeros = jnp.zeros((t, half), F32)
    rest0 = jnp.zeros((t, width - rot), F32)
    c = jnp.concatenate([cos, cos, jnp.ones((t, width - rot), F32)], axis=1)
    sa = jnp.concatenate([-sin, zeros, rest0], axis=1)
    sb = jnp.concatenate([zeros, sin, rest0], axis=1)
    reps = n_rope_lanes // width
    pad = LANES - n_rope_lanes
    c = jnp.concatenate([jnp.tile(c, (1, reps)), jnp.ones((t, pad), F32)], axis=1)
    sa = jnp.concatenate([jnp.tile(sa, (1, reps)), jnp.zeros((t, pad), F32)], axis=1)
    sb = jnp.concatenate([jnp.tile(sb, (1, reps)), jnp.zeros((t, pad), F32)], axis=1)
    return c, sa, sb


def _kv_major_rows(a, db, n_new, n_kv, groups):
    a = a.reshape(db, n_new, n_kv, groups, HEAD_DIM)
    return jnp.transpose(a, (0, 2, 3, 1, 4)).reshape(db, n_kv * groups * n_new, HEAD_DIM)


def _from_kv_major_rows(o, db, n_new, n_kv, groups):
    o = o.reshape(db, n_kv, groups, n_new, HEAD_DIM)
    return jnp.transpose(o, (0, 3, 1, 2, 4)).reshape(db * n_new, n_kv * groups * HEAD_DIM)


def _new_kv_pad(k, db, n_new, n_kv):
    k = jnp.transpose(k.reshape(db, n_new, n_kv, HEAD_DIM), (0, 2, 1, 3))
    return jnp.pad(k, ((0, 0), (0, 0), (0, LANES - n_new), (0, 0)))


def kernel(x_prompt, x_sample, cache_a_k, cache_a_v, cache_a_idx, cache_b_k, cache_b_v, page_table, w_in, norm1_g,
           c_ln_g, c_ln_b, w_spatial, b_spatial, w_branch, w_out, norm2_g, w_ff1, w_ff2, norm_f_g):
    batch, seq, d_model = x_prompt.shape
    db, n_new, _ = x_sample.shape
    depth, n_phys, page, n_kv, _ = cache_a_k.shape
    n_pages = page_table.shape[1]
    past_len = n_pages * page
    n_heads = d_model // 256
    groups = n_heads // n_kv
    aw = n_heads * HEAD_DIM
    kvw = n_kv * HEAD_DIM
    n_idx_heads = d_model // 128
    iw = n_idx_heads * IDX_DIM
    n_cg = w_spatial.shape[1]
    cw = n_cg * C_GROUP_DIM
    rot, idx_rot = HEAD_DIM // 4, IDX_DIM // 4
    assert page == LANES and aw == 2 * kvw and cw % 512 == 0 and seq % CHUNK == 0 and n_new <= CHUNK

    sizes = (aw, kvw, kvw, iw, IDX_DIM, n_idx_heads, aw, kvw, kvw, cw, cw, N_BRANCH * d_model)
    offs = [0]
    for s in sizes:
        offs.append(offs[-1] + s)
    (o_qa, o_ka, o_va, o_qi, o_ki, o_wi, o_qb, o_kb, o_vb, o_cu, o_cv, o_gl, _) = offs

    pos_p = jnp.arange(seq, dtype=jnp.int32)
    pos_s = jnp.tile(past_len + jnp.arange(n_new, dtype=jnp.int32), db)
    tabs = {
        "p": (_rope_tables(pos_p, rot, HEAD_DIM), _rope_tables(pos_p, idx_rot, IDX_DIM),
              _rope_tables(pos_p, idx_rot, IDX_DIM, IDX_DIM)),
        "s": (_rope_tables(pos_s, rot, HEAD_DIM), _rope_tables(pos_s, idx_rot, IDX_DIM),
              _rope_tables(pos_s, idx_rot, IDX_DIM, IDX_DIM)),
    }

    cache_ak = cache_a_k.reshape(depth, n_phys, page * n_kv, HEAD_DIM)
    cache_av = cache_a_v.reshape(depth, n_phys, page * n_kv, HEAD_DIM)
    cache_bk = cache_b_k.reshape(depth, n_phys, page * n_kv, HEAD_DIM)
    cache_bv = cache_b_v.reshape(depth, n_phys, page * n_kv, HEAD_DIM)
    pt_flat = page_table.reshape(-1).astype(jnp.int32)

    xp = x_prompt.reshape(batch * seq, d_model)
    xs = x_sample.reshape(db * n_new, d_model)
    rs = db * n_new
    tri = jnp.tril(jnp.ones((CHUNK, CHUNK), dtype=bool))
    sel_s = min(IDX_TOPK, (past_len + n_new) // 4)

    outs = {k: [] for k in ("pa_k", "pa_v", "pa_i", "pb_k", "pb_v", "sa_k", "sa_v", "sa_i", "sb_k", "sb_v", "sc_v")}

    w_in_t = jnp.swapaxes(w_in, 1, 2)
    cache_idx_t = jnp.swapaxes(cache_a_idx, 2, 3)

    kv_bufs = {(which, name): None for which in "ps" for name in ("ka", "kb", "va", "vb")}

    def project(x, l, which):
        t128, t64, tkiwi = tabs[which]
        xn = _rms_norm(x, norm1_g[l], BF16)

        def proj(n_off, n, out_dtypes, epi="none", extra=(), half=0, out_scale=1.0, **kw):
            return _matmul_ws(xn, w_in_t, l, out_dtypes, epi, extra, half, n_off=n_off, n=n, w_rows_are_n=True,
                              out_scale=out_scale, **kw)

        def kv_proj(name, n_off, epi="none", extra=(), half=0, want_means=False):
            res = proj(n_off, kvw, (BF16,), epi, extra, half, kv_out=(kv_bufs[which, name], depth, n_kv),
                       want_means=want_means)
            kv_bufs[which, name] = res[1]
            return (res[0], res[2]) if want_means else res[0]

        (hqa,) = proj(o_qa, aw, (BF16,), "rope", t128, rot // 2, Q_SCALE)
        (hqb,) = proj(o_qb, aw, (BF16,), "rope", t128, rot // 2, Q_SCALE)
        hka_bf = kv_proj("ka", o_ka, "rope", t128, rot // 2)
        hkb_bf = kv_proj("kb", o_kb, "rope", t128, rot // 2, want_means=(which == "p"))
        hva_bf = kv_proj("va", o_va)
        hvb_bf = kv_proj("vb", o_vb)
        (hqi,) = proj(o_qi, iw, (BF16,), "rope", t64, idx_rot // 2)
        (hkiwi,) = proj(o_ki, LANES, (F32,), "rope", tkiwi, idx_rot // 2)
        (hc,) = proj(o_cu, 2 * cw + N_BRANCH * d_model, (F32,))
        return hqa, hqb, hka_bf, hkb_bf, hva_bf, hvb_bf, hqi, hkiwi, hc

    def new_rows(name, l):
        return kv_bufs["s", name].reshape(depth, rs, kvw)[l]

    def merge_ffn(x, o_a, o_b, o_c, hc, l):
        merged = _merge(o_a, o_b, o_c, w_branch, l, hc, d_model=d_model, gate_col0=2 * cw)
        (x1,) = _matmul_ws(merged, w_out, l, (F32,), "residual", (x,))
        hn = _rms_norm(x1, norm2_g[l], BF16)
        (h1,) = _matmul_ws(hn, w_ff1, l, (BF16,), "relu2")
        (x2,) = _matmul_kt(h1, w_ff2, l, (F32,), "residual", (x1,))
        return x2

    for l in range(depth):
        hqa, hqb, hka_bf, (hkb_bf, kb_means), hva_bf, hvb_bf, hqi, hkiwi, hc = project(xp, l, "p")
        o_a = _dsa_prompt(hqa, hka_bf, hva_bf, hqi, hkiwi, batch=batch, seq=seq, n_heads=n_heads,
                          n_idx_heads=n_idx_heads, groups=groups)
        o_b = _moba_prompt(hqb, hkb_bf, kb_means, hvb_bf, batch=batch, seq=seq, n_heads=n_heads, groups=groups)
        ws_p = jnp.where(tri[None], w_spatial[l], 0).astype(BF16)
        bs_p = b_spatial[l].T
        (o_c,) = _gmlp(hc, c_ln_g[l], c_ln_b[l], ws_p, bs_p, want_vn=False)
        xp = merge_ffn(xp, o_a, o_b, o_c, hc, l)
        outs["pa_i"].append(hkiwi[:, :IDX_DIM].reshape(batch, seq, IDX_DIM))

        hqa, hqb, hka_bf, hkb_bf, hva_bf, hvb_bf, hqi, hkiwi, hc = project(xs, l, "s")
        qi_rows = jnp.transpose(hqi.astype(F32).reshape(db, n_new, n_idx_heads, IDX_DIM), (0, 2, 1, 3)).reshape(
            db, n_idx_heads * n_new, IDX_DIM)
        wi_rows = jnp.transpose(hkiwi[:, IDX_DIM:IDX_DIM + n_idx_heads].reshape(db, n_new, n_idx_heads),
                                (0, 2, 1)).reshape(db, n_idx_heads * n_new, 1)
        ki_new = jnp.pad(hkiwi[:, :IDX_DIM].reshape(db, n_new, IDX_DIM), ((0, 0), (0, LANES - n_new), (0, 0)))
        scores = _dsa_sample_scores(pt_flat, cache_idx_t, l, qi_rows, wi_rows, dec_batch=db, n_pages=n_pages,
                                    n_new=n_new, n_idx_heads=n_idx_heads)
        sel, sel_new = _dsa_sample_select(scores, qi_rows, wi_rows, ki_new, sel_s, n_idx_heads=n_idx_heads)
        qa_rows = _kv_major_rows(hqa.astype(F32), db, n_new, n_kv, groups)
        o_a = _dsa_sample_attn(pt_flat, cache_ak, cache_av, l, _new_kv_pad(new_rows("ka", l), db, n_new, n_kv),
                               _new_kv_pad(new_rows("va", l), db, n_new, n_kv), qa_rows, sel, sel_new, dec_batch=db,
                               n_pages=n_pages, n_new=n_new, n_kv=n_kv, groups=groups)
        o_a = _from_kv_major_rows(o_a, db, n_new, n_kv, groups).astype(BF16)
        qb_rows = _kv_major_rows(hqb.astype(F32), db, n_new, n_kv, groups)
        o_b = _moba_sample(pt_flat, cache_bk, cache_bv, l, _new_kv_pad(new_rows("kb", l), db, n_new, n_kv),
                           _new_kv_pad(new_rows("vb", l), db, n_new, n_kv), qb_rows, dec_batch=db, n_pages=n_pages,
                           n_new=n_new, n_kv=n_kv, groups=groups, past_len=past_len)
        o_b = _from_kv_major_rows(o_b, db, n_new, n_kv, groups).astype(BF16)
        ws_c = jnp.where(tri[None, :n_new, :n_new], w_spatial[l][:, :n_new, :n_new], 0)
        ws_s = jnp.einsum("ab,gij->gaibj", jnp.eye(db, dtype=F32), ws_c).reshape(n_cg, rs, rs).astype(BF16)
        bs_s = jnp.tile(b_spatial[l][:, :n_new].T, (db, 1))
        o_c, vn = _gmlp(hc, c_ln_g[l], c_ln_b[l], ws_s, bs_s, want_vn=True)
        xs = merge_ffn(xs, o_a, o_b, o_c, hc, l)
        outs["sa_i"].append(hkiwi[:, :IDX_DIM].reshape(db, n_new, IDX_DIM))
        outs["sc_v"].append(vn.reshape(db, n_new, cw))

    y_prompt = _rms_norm(xp, norm_f_g, F32).reshape(batch, seq, d_model)
    y_sample = _rms_norm(xs, norm_f_g, F32).reshape(db, n_new, d_model)

    def kv_result(which, name):
        lead = (depth, batch, seq) if which == "p" else (depth, db, n_new)
        return kv_bufs[which, name].reshape(lead + (n_kv, HEAD_DIM))

    return (y_prompt, y_sample, kv_result("p", "ka"), kv_result("p", "va"), jnp.stack(outs["pa_i"]),
            kv_result("p", "kb"), kv_result("p", "vb"), kv_result("s", "ka"), kv_result("s", "va"),
            jnp.stack(outs["sa_i"]), kv_result("s", "kb"), kv_result("s", "vb"), jnp.stack(outs["sc_v"]))
```

```python
import functools
import math

import jax
import jax.numpy as jnp
from jax import lax
from jax.experimental import pallas as pl
from jax.experimental.pallas import tpu as pltpu

HEAD_DIM = 128
ROPE_THETA = 500000.0
IDX_DIM = 64
IDX_TOPK = 256
MOBA_BLOCK = 256
MOBA_TOPK = 3
CHUNK = 128
C_GROUP_DIM = 128
N_BRANCH = 3
EPS = 1e-6
LANES = 128
INT32_MIN = -(2 ** 31)
PAGES_PER_STEP = 16
Q_SCALE = HEAD_DIM ** -0.5 * math.log2(math.e)

F32 = jnp.float32
BF16 = jnp.bfloat16
NEG_INF = float("-inf")
VMEM_LIMIT = 56 * 1024 * 1024

_NT = (((1,), (1,)), ((), ()))


def _params(sem):
    return pltpu.CompilerParams(dimension_semantics=sem, vmem_limit_bytes=VMEM_LIMIT)


def _rms_kernel(x_ref, g_ref, o_ref):
    x = x_ref[...]
    y = x * lax.rsqrt(jnp.mean(x * x, axis=-1, keepdims=True) + EPS)
    o_ref[...] = (y * g_ref[...]).astype(o_ref.dtype)


def _rms_norm(x, g, out_dtype):
    m, d = x.shape
    tm = min(1024, m)
    return pl.pallas_call(
        _rms_kernel,
        out_shape=jax.ShapeDtypeStruct((m, d), out_dtype),
        grid=(m // tm,),
        in_specs=[pl.BlockSpec((tm, d), lambda i: (i, 0)),
                  pl.BlockSpec((1, d), lambda i: (0, 0))],
        out_specs=pl.BlockSpec((tm, d), lambda i: (i, 0)),
        compiler_params=_params(("parallel",)),
        name="rms_norm",
    )(x, g.reshape(1, d))


def _rope_epilogue(r, c, sa, sb, half):
    outs = []
    for s in range(r.shape[1] // LANES):
        a = r[:, s * LANES:(s + 1) * LANES]
        outs.append(a * c + pltpu.roll(a, LANES - half, 1) * sa + pltpu.roll(a, half, 1) * sb)
    return outs[0] if len(outs) == 1 else jnp.concatenate(outs, axis=1)


_N_EXTRA = {"none": 0, "relu2": 0, "rope": 3, "residual": 1}


def _epilogue(r, epi, extra, half, outs, out_scale=1.0):
    if epi == "rope":
        r = _rope_epilogue(r, extra[0][...], extra[1][...], extra[2][...], half)
        if out_scale != 1.0:
            r = r * out_scale
    elif epi == "relu2":
        r = jnp.square(jnp.maximum(r, 0.0))
    elif epi == "residual":
        r = extra[0][...] + r
    for o_ref in outs:
        o_ref[...] = r.astype(o_ref.dtype)
    return r


def _mm_ws_kernel(*refs, epi, n_out, half, w_rows_are_n, out_scale, n_kv, has_buf, want_means):
    a_ref, w_ref = refs[0], refs[1]
    extra = refs[2:2 + _N_EXTRA[epi]]
    first_out = 2 + _N_EXTRA[epi] + (1 if has_buf else 0)
    outs = refs[first_out:first_out + n_out]
    wbf_s = refs[-1]

    @pl.when(pl.program_id(1) == 0)
    def _():
        wbf_s[...] = (w_ref[0] if w_rows_are_n else w_ref[...]).astype(BF16)

    if w_rows_are_n:
        part = lax.dot_general(a_ref[...], wbf_s[...], _NT, preferred_element_type=F32)
    else:
        part = jnp.dot(a_ref[...], wbf_s[...], preferred_element_type=F32)
    r = _epilogue(part, epi, extra, half, outs, out_scale)
    if n_kv:
        kv_ref = refs[first_out + n_out]
        tm = r.shape[0]
        for kv in range(n_kv):
            kv_ref[pl.ds(kv, tm, stride=n_kv), :] = r[:, kv * HEAD_DIM:(kv + 1) * HEAD_DIM]
    if want_means:
        mean_ref = refs[first_out + n_out + 1]
        for g in range(r.shape[0] // MOBA_BLOCK):
            mean_ref[g:g + 1, :] = jnp.mean(r[g * MOBA_BLOCK:(g + 1) * MOBA_BLOCK, :], axis=0, keepdims=True)


def _matmul_ws(a, w_stack, layer, out_dtypes, epi="none", extra=(), half=0, n_off=0, n=None, w_rows_are_n=False,
               out_scale=1.0, kv_out=None, want_means=False, tm=1024, tn=1024):
    m, kdim = a.shape
    n_tot = w_stack.shape[1] if w_rows_are_n else w_stack.shape[2]
    n = n_tot - n_off if n is None else n
    tm, tn = min(tm, m), min(tn, n)
    if epi == "rope":
        tm = min(tm, extra[0].shape[0])
    assert m % tm == 0 and n % tn == 0 and n_off + n <= n_tot, (a.shape, w_stack.shape, tm, tn)
    in_specs = [pl.BlockSpec((tm, kdim), lambda j, i: (i, 0))]
    if w_rows_are_n:
        assert n_off % 8 == 0 and n_off + (n // tn) * tn <= n_tot
        in_specs.append(pl.BlockSpec((pl.Element(1), pl.Element(tn), pl.Element(kdim)),
                                     lambda j, i: (layer, pl.multiple_of(n_off + j * tn, 8), 0)))
        w_scratch = pltpu.VMEM((tn, kdim), BF16)
    else:
        assert n_off % tn == 0
        in_specs.append(pl.BlockSpec((None, kdim, tn), lambda j, i: (layer, 0, n_off // tn + j)))
        w_scratch = pltpu.VMEM((kdim, tn), BF16)
    if epi == "rope":
        nt = extra[0].shape[0] // tm
        in_specs += [pl.BlockSpec((tm, LANES), lambda j, i: (i % nt, 0))] * 3
    elif epi == "residual":
        in_specs += [pl.BlockSpec((tm, tn), lambda j, i: (i, j))]
    out_shape = [jax.ShapeDtypeStruct((m, n), dt) for dt in out_dtypes]
    out_specs = [pl.BlockSpec((tm, tn), lambda j, i: (i, j)) for _ in out_dtypes]
    operands, aliases, n_kv = [a, w_stack, *extra], {}, 0
    if kv_out is not None:
        buf, n_layers, n_kv = kv_out
        assert n == tn == n_kv * HEAD_DIM
        if buf is not None:
            aliases = {len(operands): len(out_shape)}
            operands.append(buf)
            in_specs.append(pl.BlockSpec(memory_space=pl.ANY))
        out_shape.append(jax.ShapeDtypeStruct((n_layers * m * n_kv, HEAD_DIM), F32))
        out_specs.append(pl.BlockSpec((tm * n_kv, HEAD_DIM), lambda j, i: (layer * (m // tm) + i, 0)))
    if want_means:
        assert kv_out is not None and tm % MOBA_BLOCK == 0
        out_shape.append(jax.ShapeDtypeStruct((m // tm, tm // MOBA_BLOCK, n), F32))
        out_specs.append(pl.BlockSpec((None, tm // MOBA_BLOCK, n), lambda j, i: (i, 0, 0)))
    outs = pl.pallas_call(
        functools.partial(_mm_ws_kernel, epi=epi, n_out=len(out_dtypes), half=half, w_rows_are_n=w_rows_are_n,
                          out_scale=out_scale, n_kv=n_kv, has_buf=bool(aliases), want_means=want_means),
        out_shape=out_shape,
        grid=(n // tn, m // tm),
        in_specs=in_specs,
        out_specs=out_specs,
        scratch_shapes=[w_scratch],
        input_output_aliases=aliases,
        compiler_params=_params(("parallel", "arbitrary")),
        name="matmul_" + epi,
    )(*operands)
    return tuple(outs)


def _mm_kt_kernel(*refs, epi, nk, n_out, half):
    a_ref, w_ref = refs[0], refs[1]
    extra = refs[2:2 + _N_EXTRA[epi]]
    outs = refs[2 + _N_EXTRA[epi]:2 + _N_EXTRA[epi] + n_out]
    acc_ref = refs[-1]
    k = pl.program_id(2)
    part = jnp.dot(a_ref[...], w_ref[...].astype(BF16), preferred_element_type=F32)

    @pl.when(k == 0)
    def _():
        acc_ref[...] = part

    @pl.when(k > 0)
    def _():
        acc_ref[...] += part

    @pl.when(k == nk - 1)
    def _():
        _epilogue(acc_ref[...], epi, extra, half, outs)


def _matmul_kt(a, w_stack, layer, out_dtypes, epi="none", extra=(), tm=1024, tn=1024, tk=2048):
    m, kdim = a.shape
    n = w_stack.shape[2]
    tm, tn, tk = min(tm, m), min(tn, n), min(tk, kdim)
    assert m % tm == 0 and n % tn == 0 and kdim % tk == 0 and kdim > tk, (a.shape, w_stack.shape, tm, tn, tk)
    nk = kdim // tk
    in_specs = [pl.BlockSpec((tm, tk), lambda i, j, k: (i, k)),
                pl.BlockSpec((None, tk, tn), lambda i, j, k: (layer, k, j))]
    if epi == "residual":
        in_specs += [pl.BlockSpec((tm, tn), lambda i, j, k: (i, j))]
    outs = pl.pallas_call(
        functools.partial(_mm_kt_kernel, epi=epi, nk=nk, n_out=len(out_dtypes), half=0),
        out_shape=[jax.ShapeDtypeStruct((m, n), dt) for dt in out_dtypes],
        grid=(m // tm, n // tn, nk),
        in_specs=in_specs,
        out_specs=[pl.BlockSpec((tm, tn), lambda i, j, k: (i, j)) for _ in out_dtypes],
        scratch_shapes=[pltpu.VMEM((tm, tn), F32)],
        compiler_params=_params(("parallel", "parallel", "arbitrary")),
        name="matmul_kt_" + epi,
    )(a, w_stack, *extra)
    return tuple(outs)


def _ordered_key(score):
    bits = pltpu.bitcast(jnp.where(score == 0.0, 0.0, score), jnp.int32)
    return jnp.where(bits < 0, bits ^ jnp.int32(0x7FFFFFFF), bits)


def _kth_largest_key(count_ge, shape, n_sel):
    def body(b, t):
        cand = t + lax.shift_left(jnp.int32(1), jnp.int32(31) - b)
        return jnp.where(count_ge(cand) >= float(n_sel), cand, t)

    return lax.fori_loop(0, 32, body, jnp.full(shape, INT32_MIN, jnp.int32))


def _topk_lanes(g, valid, n_sel):
    lane = lax.broadcasted_iota(jnp.int32, g.shape, 1).astype(F32)
    gm = jnp.where(valid, g, NEG_INF)
    sel = jnp.zeros(g.shape, jnp.bool_)
    for _ in range(n_sel):
        mx = jnp.max(gm, axis=1, keepdims=True)
        first = jnp.min(jnp.where(gm == mx, lane, float(LANES)), axis=1, keepdims=True)
        pick = lane == first
        sel = sel | pick
        gm = jnp.where(pick, NEG_INF, gm)
    return sel & valid


def _softmax_pv(logits2, v):
    p = jnp.exp2(logits2 - jnp.max(logits2, axis=1, keepdims=True))
    l = jnp.sum(p, axis=1, keepdims=True)
    return jnp.dot(p.astype(BF16), v, preferred_element_type=F32) / l


def _softmax_pv_t(logits2_t, vt):
    p = jnp.exp2(logits2_t - jnp.max(logits2_t, axis=0, keepdims=True))
    l = jnp.sum(p, axis=0, keepdims=True)
    ot = jnp.dot(vt, p.astype(BF16), preferred_element_type=F32)
    return jnp.transpose(ot / l)


def _topk_rows(g, valid, n_sel):
    rid = lax.broadcasted_iota(jnp.int32, g.shape, 0).astype(F32)
    gm = jnp.where(valid, g, NEG_INF)
    sel = jnp.zeros(g.shape, jnp.bool_)
    for _ in range(n_sel):
        mx = jnp.max(gm, axis=0, keepdims=True)
        first = jnp.min(jnp.where(gm == mx, rid, float(g.shape[0])), axis=0, keepdims=True)
        pick = rid == first
        sel = sel | pick
        gm = jnp.where(pick, NEG_INF, gm)
    return sel & valid


def _width_classes(seq, tile):
    widths = sorted({w for w in (seq // 4, seq // 2, seq) if w % tile == 0 and w >= tile})
    classes, lo = [], 0
    for w in widths:
        classes.append((lo, w // tile, w))
        lo = w // tile
    return classes


def _dsa_prompt_kernel(qa_ref, ka_ref, va_ref, qi_ref, ki_ref, wi_ref, o_ref, ki_s, qcat_s, wt_s, key_s, bias_s,
                       tie_s, *, tq, seq, n_sel, n_heads, n_idx_heads, groups):
    i = pl.program_id(1)
    n_valid = i + 1
    kblk = 32

    @pl.when(i == 0)
    def _():
        ki_s[...] = ki_ref[:, :IDX_DIM].astype(BF16)

    for h in range(n_idx_heads):
        qcat_s[h * tq:(h + 1) * tq, :] = qi_ref[:, h * IDX_DIM:(h + 1) * IDX_DIM]
    wt_s[...] = jnp.transpose(wi_ref[...]) * ((IDX_DIM ** -0.5) * (n_idx_heads ** -0.5))

    krow = lax.broadcasted_iota(jnp.int32, (tq, tq), 0)
    qcol = lax.broadcasted_iota(jnp.int32, (tq, tq), 1)

    def visible(c):
        return krow <= qcol + jnp.where(c == i, 0, tq)

    for c in range(seq // tq):
        @pl.when(c < n_valid)
        def _(c=c):
            d_all = lax.dot_general(ki_s[c * tq:(c + 1) * tq, :], qcat_s[...], _NT,
                                    preferred_element_type=F32)
            for kb in range(tq // kblk):
                rows = slice(kb * kblk, (kb + 1) * kblk)
                s = None
                for h in range(n_idx_heads):
                    term = jnp.maximum(d_all[rows, h * tq:(h + 1) * tq], 0.0) * wt_s[IDX_DIM + h:IDX_DIM + h + 1, :]
                    s = term if s is None else s + term
                key_s[c, rows, :] = _ordered_key(jnp.where(visible(c)[rows, :], s, NEG_INF))

    def count_ge(cand):
        def body(c, acc):
            hit = (key_s[c] >= cand).astype(F32)
            return acc + jnp.sum(hit.reshape(tq // 8, 8, tq), axis=0)

        acc = lax.fori_loop(0, n_valid, body, jnp.zeros((8, tq), F32))
        return jnp.sum(acc, axis=0, keepdims=True)

    thr = _kth_largest_key(count_ge, (1, tq), n_sel)
    need = float(n_sel) - count_ge(thr + 1)
    prefix_ones = jnp.where(qcol <= krow, 1.0, 0.0).astype(BF16)
    tie_s[...] = jnp.zeros_like(tie_s)

    def kv_cols(h):
        return slice((h // groups) * HEAD_DIM, (h // groups + 1) * HEAD_DIM)

    def attend(width):
        for c in range(width // tq):
            @pl.when(c < n_valid)
            def _(c=c):
                k, vis = key_s[c], visible(c)
                tie = (k == thr) & vis
                rank = jnp.dot(prefix_ones, jnp.where(tie, 1.0, 0.0).astype(BF16),
                               preferred_element_type=F32) + tie_s[...]
                ok = ((k > thr) & vis) | (tie & (rank <= need))
                bias_s[:, c * tq:(c + 1) * tq] = jnp.transpose(jnp.where(ok, 0.0, NEG_INF))
                tie_s[...] = rank[tq - 1:tq, :]

            @pl.when(c >= n_valid)
            def _(c=c):
                bias_s[:, c * tq:(c + 1) * tq] = jnp.full((tq, tq), NEG_INF, F32)

        def qk(h):
            return lax.dot_general(qa_ref[:, h * HEAD_DIM:(h + 1) * HEAD_DIM], ka_ref[0:width, kv_cols(h)], _NT,
                                   preferred_element_type=F32)

        raw = qk(0)
        for h in range(n_heads):
            nxt = qk(h + 1) if h + 1 < n_heads else None
            o = _softmax_pv(raw + bias_s[:, 0:width], va_ref[0:width, kv_cols(h)])
            o_ref[:, h * HEAD_DIM:(h + 1) * HEAD_DIM] = o.astype(o_ref.dtype)
            raw = nxt

    for lo, hi, width in _width_classes(seq, tq):
        @pl.when((i >= lo) & (i < hi))
        def _(width=width):
            attend(width)


def _dsa_prompt(hq, hk_bf, hv_bf, hqi, hkiwi, *, batch, seq, n_heads, n_idx_heads, groups):
    tq = min(256, seq)
    assert seq % tq == 0 and tq % LANES == 0
    nq = seq // tq
    aw = n_heads * HEAD_DIM
    kvw = aw // groups
    n_sel = min(IDX_TOPK, seq // 4)
    kern = functools.partial(_dsa_prompt_kernel, tq=tq, seq=seq, n_sel=n_sel, n_heads=n_heads,
                             n_idx_heads=n_idx_heads, groups=groups)
    return pl.pallas_call(
        kern,
        out_shape=jax.ShapeDtypeStruct((batch * seq, aw), BF16),
        grid=(batch, nq),
        in_specs=[pl.BlockSpec((tq, aw), lambda b, i: (b * nq + i, 0)),
                  pl.BlockSpec((seq, kvw), lambda b, i: (b, 0)),
                  pl.BlockSpec((seq, kvw), lambda b, i: (b, 0)),
                  pl.BlockSpec((tq, n_idx_heads * IDX_DIM), lambda b, i: (b * nq + i, 0)),
                  pl.BlockSpec((seq, LANES), lambda b, i: (b, 0)),
                  pl.BlockSpec((tq, LANES), lambda b, i: (b * nq + i, 0))],
        out_specs=pl.BlockSpec((tq, aw), lambda b, i: (b * nq + i, 0)),
        scratch_shapes=[pltpu.VMEM((seq, IDX_DIM), BF16), pltpu.VMEM((n_idx_heads * tq, IDX_DIM), BF16),
                        pltpu.VMEM((LANES, tq), F32), pltpu.VMEM((nq, tq, tq), jnp.int32),
                        pltpu.VMEM((tq, seq), F32), pltpu.VMEM((1, tq), F32)],
        compiler_params=_params(("parallel", "arbitrary")),
        name="dsa_prompt",
    )(hq, hk_bf, hv_bf, hqi, hkiwi, hkiwi)


def _moba_prompt_kernel(qb_ref, kb_ref, kmean_ref, vb_ref, o_ref, km_s, vt_s, *, nb, n_sel, n_heads, groups):
    i = pl.program_id(1)
    blk = MOBA_BLOCK

    @pl.when(i == 0)
    def _():
        km_s[...] = jnp.zeros_like(km_s)
        per = kmean_ref.shape[1]
        for j in range(nb):
            km_s[j:j + 1, :] = kmean_ref[j // per, j % per:j % per + 1, :]
        for kv in range(n_heads // groups):
            v = vb_ref[:, kv * HEAD_DIM:(kv + 1) * HEAD_DIM].astype(F32)
            vt_s[kv] = jnp.transpose(v).astype(BF16)

    past = lax.broadcasted_iota(jnp.int32, (LANES, blk), 0) < i
    tril_bias = jnp.where(lax.broadcasted_iota(jnp.int32, (blk, blk), 0)
                          <= lax.broadcasted_iota(jnp.int32, (blk, blk), 1), 0.0, NEG_INF)

    def kv_cols(h):
        return slice((h // groups) * HEAD_DIM, (h // groups + 1) * HEAD_DIM)

    def attend(width):
        def qk(h):
            q = qb_ref[:, h * HEAD_DIM:(h + 1) * HEAD_DIM]
            gate = lax.dot_general(km_s[:, kv_cols(h)].astype(BF16), q, _NT, preferred_element_type=F32)
            return gate, lax.dot_general(kb_ref[0:width, kv_cols(h)], q, _NT, preferred_element_type=F32)

        cur = qk(0)
        for h in range(n_heads):
            nxt = qk(h + 1) if h + 1 < n_heads else None
            gate, raw = cur
            sel_bias = jnp.where(_topk_rows(gate, past, n_sel), 0.0, NEG_INF)
            pieces = []
            for j in range(width // blk):
                bias = jnp.where(j == i, tril_bias, sel_bias[j:j + 1, :])
                pieces.append(raw[j * blk:(j + 1) * blk, :] + bias)
            o = _softmax_pv_t(jnp.concatenate(pieces, axis=0), vt_s[h // groups, :, 0:width])
            o_ref[:, h * HEAD_DIM:(h + 1) * HEAD_DIM] = o.astype(o_ref.dtype)
            cur = nxt

    for lo, hi, width in _width_classes(nb * blk, blk):
        @pl.when((i >= lo) & (i < hi))
        def _(width=width):
            attend(width)


def _moba_prompt(hq, hk_bf, k_means, hv_bf, *, batch, seq, n_heads, groups):
    blk = MOBA_BLOCK
    assert seq % blk == 0
    nb = seq // blk
    g_per_seq = k_means.shape[0] // batch
    assert 1 <= nb <= LANES and g_per_seq * k_means.shape[1] == nb
    aw = n_heads * HEAD_DIM
    kvw = aw // groups
    kern = functools.partial(_moba_prompt_kernel, nb=nb, n_sel=min(MOBA_TOPK, nb), n_heads=n_heads,
                             groups=groups)
    return pl.pallas_call(
        kern,
        out_shape=jax.ShapeDtypeStruct((batch * seq, aw), BF16),
        grid=(batch, nb),
        in_specs=[pl.BlockSpec((blk, aw), lambda b, i: (b * nb + i, 0)),
                  pl.BlockSpec((seq, kvw), lambda b, i: (b, 0)),
                  pl.BlockSpec((g_per_seq, k_means.shape[1], kvw), lambda b, i: (b, 0, 0)),
                  pl.BlockSpec((seq, kvw), lambda b, i: (b, 0))],
        out_specs=pl.BlockSpec((blk, aw), lambda b, i: (b * nb + i, 0)),
        scratch_shapes=[pltpu.VMEM((LANES, kvw), F32), pltpu.VMEM((n_heads // groups, HEAD_DIM, seq), BF16)],
        compiler_params=_params(("parallel", "arbitrary")),
        name="moba_prompt",
    )(hq, hk_bf, k_means, hv_bf)


def _gmlp_kernel(cu_ref, cv_ref, g_ref, b_ref, ws_ref, bs_ref, o_ref, *vn_out, n_groups):
    r = ws_ref.shape[1]
    for ch in range(cv_ref.shape[0] // r):
        rows = slice(ch * r, (ch + 1) * r)
        v = cv_ref[rows, :]
        mu = jnp.mean(v, axis=-1, keepdims=True)
        vc = v - mu
        vn = vc * lax.rsqrt(jnp.mean(vc * vc, axis=-1, keepdims=True) + EPS) * g_ref[...] + b_ref[...]
        if vn_out:
            vn_out[0][rows, :] = vn
        for g in range(n_groups):
            cols = slice(g * C_GROUP_DIM, (g + 1) * C_GROUP_DIM)
            z = jnp.dot(ws_ref[g], vn[:, cols].astype(BF16), preferred_element_type=F32) + bs_ref[:, g:g + 1]
            o_ref[rows, cols] = (cu_ref[rows, cols] * z).astype(o_ref.dtype)


def _gmlp(hc, ln_g, ln_b, ws, bs, *, want_vn):
    m = hc.shape[0]
    n_groups, chunk = ws.shape[0], ws.shape[1]
    cw = n_groups * C_GROUP_DIM
    r = min(m, 8 * chunk)
    assert m % r == 0 and r % chunk == 0
    out_shape = [jax.ShapeDtypeStruct((m, cw), BF16)]
    out_specs = [pl.BlockSpec((r, cw), lambda i: (i, 0))]
    if want_vn:
        out_shape.append(jax.ShapeDtypeStruct((m, cw), F32))
        out_specs.append(pl.BlockSpec((r, cw), lambda i: (i, 0)))
    return pl.pallas_call(
        functools.partial(_gmlp_kernel, n_groups=n_groups),
        out_shape=out_shape,
        grid=(m // r,),
        in_specs=[pl.BlockSpec((r, cw), lambda i: (i, 0)),
                  pl.BlockSpec((r, cw), lambda i: (i, 1)),
                  pl.BlockSpec((1, cw), lambda i: (0, 0)),
                  pl.BlockSpec((1, cw), lambda i: (0, 0)),
                  pl.BlockSpec((n_groups, chunk, chunk), lambda i: (0, 0, 0)),
                  pl.BlockSpec((chunk, n_groups), lambda i: (0, 0))],
        out_specs=out_specs,
        compiler_params=_params(("parallel",)),
        name="gmlp",
    )(hc, hc, ln_g.reshape(1, cw), ln_b.reshape(1, cw), ws, bs)


def _merge_kernel(oa_ref, ob_ref, oc_ref, wb_ref, g0_ref, g1_ref, g2_ref, o_ref, wbf_s):
    @pl.when(pl.program_id(1) == 0)
    def _():
        wbf_s[...] = wb_ref[...].astype(BF16)

    acc = None
    for o_n, g_n, n in ((oa_ref, g0_ref, 0), (ob_ref, g1_ref, 1), (oc_ref, g2_ref, 2)):
        br = jnp.dot(o_n[...], wbf_s[n], preferred_element_type=F32)
        term = (1.0 / (1.0 + jnp.exp(-g_n[...]))) * br
        acc = term if acc is None else acc + term
    o_ref[...] = acc.astype(o_ref.dtype)


def _merge(o_a, o_b, o_c, wb_stack, layer, hc, *, d_model, gate_col0):
    m, aw = o_a.shape
    tm, tn = min(1024, m), 512
    nj = d_model // tn
    g0 = gate_col0 // tn
    gate_specs = [pl.BlockSpec((tm, tn), functools.partial(lambda j, i, n: (i, g0 + n * nj + j), n=n))
                  for n in range(N_BRANCH)]
    return pl.pallas_call(
        _merge_kernel,
        out_shape=jax.ShapeDtypeStruct((m, d_model), BF16),
        grid=(nj, m // tm),
        in_specs=[pl.BlockSpec((tm, aw), lambda j, i: (i, 0))] * 3
        + [pl.BlockSpec((None, N_BRANCH, aw, tn), lambda j, i: (layer, 0, 0, j))] + gate_specs,
        out_specs=pl.BlockSpec((tm, tn), lambda j, i: (i, j)),
        scratch_shapes=[pltpu.VMEM((N_BRANCH, aw, tn), BF16)],
        compiler_params=_params(("parallel", "arbitrary")),
        name="merge",
    )(o_a, o_b, o_c, wb_stack, hc, hc, hc)


def _page_specs(block, layer, n_pages, per_step):
    def page_map(b, s, pt, slot):
        return (layer, pt[b * n_pages + s * per_step + slot], 0, 0)

    return [pl.BlockSpec(block, functools.partial(page_map, slot=slot)) for slot in range(per_step)]


def _idx_scores(qi, wi, k, n_new, n_idx_heads, k_is_t=False):
    if k_is_t:
        d = jnp.dot(qi, k, preferred_element_type=F32) * (IDX_DIM ** -0.5)
    else:
        d = lax.dot_general(qi, k, _NT, preferred_element_type=F32) * (IDX_DIM ** -0.5)
    dw = jnp.maximum(d, 0.0) * wi
    acc = dw[0:8]
    for r in range(8, n_idx_heads * n_new, 8):
        acc = acc + dw[r:r + 8]
    s = acc[0:n_new]
    for r in range(n_new, 8, n_new):
        s = s + acc[r:r + n_new]
    return s * (n_idx_heads ** -0.5)


def _dsa_sample_score_kernel(pt_ref, *refs, per_step, n_new, n_idx_heads):
    pages = refs[:per_step]
    qi_ref, wi_ref, o_ref = refs[per_step:]
    keys_t = jnp.concatenate([p[...] for p in pages], axis=1) if per_step > 1 else pages[0][...]
    o_ref[...] = _idx_scores(qi_ref[...], wi_ref[...], keys_t, n_new, n_idx_heads, k_is_t=True)


def _dsa_sample_scores(pt_flat, cache_idx_t, layer, qi_rows, wi_rows, *, dec_batch, n_pages, n_new, n_idx_heads):
    page = cache_idx_t.shape[3]
    assert page == LANES and 8 % n_new == 0 and (n_idx_heads * n_new) % 8 == 0
    per_step = min(PAGES_PER_STEP, n_pages)
    assert n_pages % per_step == 0
    rows = n_idx_heads * n_new
    kern = functools.partial(_dsa_sample_score_kernel, per_step=per_step, n_new=n_new, n_idx_heads=n_idx_heads)
    grid_spec = pltpu.PrefetchScalarGridSpec(
        num_scalar_prefetch=1,
        grid=(dec_batch, n_pages // per_step),
        in_specs=_page_specs((None, None, IDX_DIM, page), layer, n_pages, per_step)
        + [pl.BlockSpec((None, rows, IDX_DIM), lambda b, s, pt: (b, 0, 0)),
           pl.BlockSpec((None, rows, 1), lambda b, s, pt: (b, 0, 0))],
        out_specs=pl.BlockSpec((None, n_new, per_step * LANES), lambda b, s, pt: (b, 0, s)),
    )
    return pl.pallas_call(
        kern,
        out_shape=jax.ShapeDtypeStruct((dec_batch, n_new, n_pages * LANES), F32),
        grid_spec=grid_spec,
        compiler_params=_params(("parallel", "arbitrary")),
        name="dsa_sample_scores",
    )(pt_flat, *([cache_idx_t] * per_step), qi_rows, wi_rows)


def _select_kernel(s_ref, qi_ref, wi_ref, kn_ref, o_ref, on_ref, key_s, keyn_s, *, n_sel, n_new, n_idx_heads, db):
    rows = db * n_new
    t = lax.broadcasted_iota(jnp.int32, (rows, LANES), 0) & (n_new - 1)
    c = lax.broadcasted_iota(jnp.int32, (rows, LANES), 1)
    new_ok = c <= t
    key_s[...] = _ordered_key(s_ref[...])
    for b in range(db):
        s_new = _idx_scores(qi_ref[b], wi_ref[b], kn_ref[b], n_new, n_idx_heads)
        keyn_s[b * n_new:(b + 1) * n_new, :] = _ordered_key(jnp.where(new_ok[0:n_new], s_new, NEG_INF))

    def count_ge(cand):
        return (jnp.sum((key_s[...] >= cand).astype(F32), axis=1, keepdims=True)
                + jnp.sum((keyn_s[...] >= cand).astype(F32), axis=1, keepdims=True))

    thr = _kth_largest_key(count_ge, (rows, 1), n_sel)
    need = float(n_sel) - count_ge(thr + 1)
    prefix_ones = jnp.where(lax.broadcasted_iota(jnp.int32, (LANES, LANES), 0)
                            <= lax.broadcasted_iota(jnp.int32, (LANES, LANES), 1), 1.0, 0.0).astype(BF16)
    seen = jnp.zeros((rows, 1), F32)

    def pick(k, vis, seen):
        tie = (k == thr) & vis
        rank = jnp.dot(jnp.where(tie, 1.0, 0.0).astype(BF16), prefix_ones,
                       preferred_element_type=F32) + seen
        ok = ((k > thr) & vis) | (tie & (rank <= need))
        return jnp.where(ok, 1.0, 0.0), rank[:, LANES - 1:LANES]

    every = c >= 0
    for blk in range(key_s.shape[1] // LANES):
        cols = slice(blk * LANES, (blk + 1) * LANES)
        o_ref[:, cols], seen = pick(key_s[:, cols], every, seen)
    on_ref[...], seen = pick(keyn_s[...], new_ok, seen)


def _dsa_sample_select(scores, qi_rows, wi_rows, ki_new_pad, n_sel, *, n_idx_heads):
    db, n_new, length = scores.shape
    rows = db * n_new
    assert n_new & (n_new - 1) == 0
    whole = pl.BlockSpec(memory_space=pltpu.VMEM)
    sel, sel_new = pl.pallas_call(
        functools.partial(_select_kernel, n_sel=n_sel, n_new=n_new, n_idx_heads=n_idx_heads, db=db),
        out_shape=[jax.ShapeDtypeStruct((rows, length), F32), jax.ShapeDtypeStruct((rows, LANES), F32)],
        in_specs=[whole] * 4,
        out_specs=[whole] * 2,
        scratch_shapes=[pltpu.VMEM((rows, length), jnp.int32), pltpu.VMEM((rows, LANES), jnp.int32)],
        compiler_params=pltpu.CompilerParams(vmem_limit_bytes=VMEM_LIMIT),
        name="dsa_sample_select",
    )(scores.reshape(rows, length), qi_rows, wi_rows, ki_new_pad)
    return sel.reshape(db, n_new, length), sel_new.reshape(db, n_new, LANES)


def _head_rows(page_refs, kv, n_kv):
    parts = [r[pl.ds(kv, LANES, stride=n_kv), :] for r in page_refs]
    return parts[0] if len(parts) == 1 else jnp.concatenate(parts, axis=0)


def _dsa_sample_attn_kernel(pt_ref, *refs, per_step, n_steps, n_new, n_kv, groups):
    k_pages, v_pages = refs[:per_step], refs[per_step:2 * per_step]
    kn_ref, vn_ref, q_ref, sel_ref, seln_ref, o_ref, m_s, l_s, acc_s = refs[2 * per_step:]
    step = pl.program_id(1)
    rpk = groups * n_new

    @pl.when(step == 0)
    def _():
        m_s[...] = jnp.full_like(m_s, NEG_INF)
        l_s[...] = jnp.zeros_like(l_s)
        acc_s[...] = jnp.zeros_like(acc_s)

    def update(get_k, get_v, sel):
        mask = jnp.concatenate([sel] * groups, axis=0) > 0.0
        rows = [slice(kv * rpk, (kv + 1) * rpk) for kv in range(n_kv)]
        logits = [lax.dot_general(q_ref[rows[kv], :], get_k(kv), _NT, preferred_element_type=F32)
                  for kv in range(n_kv)]
        probs, alphas = [], []
        for kv in range(n_kv):
            s = jnp.where(mask, logits[kv], NEG_INF)
            m_old = m_s[rows[kv], :]
            m_new = jnp.maximum(m_old, jnp.max(s, axis=1, keepdims=True))
            m_safe = jnp.where(m_new == NEG_INF, 0.0, m_new)
            alpha = jnp.exp2(m_old - m_safe)
            pr = jnp.exp2(s - m_safe)
            m_s[rows[kv], :] = m_new
            l_s[rows[kv], :] = alpha * l_s[rows[kv], :] + jnp.sum(pr, axis=1, keepdims=True)
            probs.append(pr)
            alphas.append(alpha)
        for kv in range(n_kv):
            acc_s[rows[kv], :] = alphas[kv] * acc_s[rows[kv], :] + jnp.dot(
                probs[kv], get_v(kv), preferred_element_type=F32)

    update(lambda kv: _head_rows(k_pages, kv, n_kv), lambda kv: _head_rows(v_pages, kv, n_kv), sel_ref[...])

    @pl.when(step == n_steps - 1)
    def _():
        update(lambda kv: kn_ref[kv], lambda kv: vn_ref[kv], seln_ref[...])
        o_ref[...] = acc_s[...] / l_s[...]


def _dsa_sample_attn(pt_flat, cache_k, cache_v, layer, k_new_pad, v_new_pad, q_rows, sel, sel_new, *, dec_batch,
                     n_pages, n_new, n_kv, groups):
    rows = n_kv * groups * n_new
    prow = cache_k.shape[2]
    per_step = min(PAGES_PER_STEP, n_pages)
    assert n_pages % per_step == 0
    n_steps = n_pages // per_step
    kern = functools.partial(_dsa_sample_attn_kernel, per_step=per_step, n_steps=n_steps, n_new=n_new, n_kv=n_kv,
                             groups=groups)
    page_specs = _page_specs((None, None, prow, HEAD_DIM), layer, n_pages, per_step)
    grid_spec = pltpu.PrefetchScalarGridSpec(
        num_scalar_prefetch=1,
        grid=(dec_batch, n_steps),
        in_specs=page_specs + page_specs
        + [pl.BlockSpec((None, n_kv, LANES, HEAD_DIM), lambda b, s, pt: (b, 0, 0, 0)),
           pl.BlockSpec((None, n_kv, LANES, HEAD_DIM), lambda b, s, pt: (b, 0, 0, 0)),
           pl.BlockSpec((None, rows, HEAD_DIM), lambda b, s, pt: (b, 0, 0)),
           pl.BlockSpec((None, n_new, per_step * LANES), lambda b, s, pt: (b, 0, s)),
           pl.BlockSpec((None, n_new, LANES), lambda b, s, pt: (b, 0, 0))],
        out_specs=pl.BlockSpec((None, rows, HEAD_DIM), lambda b, s, pt: (b, 0, 0)),
        scratch_shapes=[pltpu.VMEM((rows, 1), F32), pltpu.VMEM((rows, 1), F32),
                        pltpu.VMEM((rows, HEAD_DIM), F32)],
    )
    return pl.pallas_call(
        kern,
        out_shape=jax.ShapeDtypeStruct((dec_batch, rows, HEAD_DIM), F32),
        grid_spec=grid_spec,
        compiler_params=_params(("parallel", "arbitrary")),
        name="dsa_sample_attn",
    )(pt_flat, *([cache_k] * per_step), *([cache_v] * per_step), k_new_pad, v_new_pad, q_rows, sel, sel_new)


def _moba_sample_kernel(pt_ref, *refs, bps, n_steps, nb, n_sel, n_new, n_kv, groups, past_len):
    k_pages, v_pages = refs[:2 * bps], refs[2 * bps:4 * bps]
    kn_ref, vn_ref, q_ref, o_ref, gate_s, m_s, l_s, acc_s = refs[4 * bps:]
    step = pl.program_id(1)
    rpk = groups * n_new
    rows_all = n_kv * rpk
    lane = lax.broadcasted_iota(jnp.int32, (rows_all, LANES), 1)

    @pl.when(step == 0)
    def _():
        gate_s[...] = jnp.zeros_like(gate_s)
        m_s[...] = jnp.zeros_like(m_s)
        l_s[...] = jnp.zeros_like(l_s)

    gate, m_blk, l_blk = gate_s[...], m_s[...], l_s[...]
    blk = MOBA_BLOCK
    kv_rows = [slice(kv * rpk, (kv + 1) * rpk) for kv in range(n_kv)]
    ks = [_head_rows(k_pages, kv, n_kv) for kv in range(n_kv)]
    logits = [lax.dot_general(q_ref[kv_rows[kv], :], ks[kv], _NT, preferred_element_type=F32)
              for kv in range(n_kv)]
    probs = {}
    for bb in range(bps):
        g_cols, m_cols, l_cols = [], [], []
        for kv in range(n_kv):
            kmean = jnp.mean(ks[kv][bb * blk:(bb + 1) * blk, :], axis=0, keepdims=True)
            g_cols.append(jnp.sum(q_ref[kv_rows[kv], :] * kmean, axis=1, keepdims=True))
            s = logits[kv][:, bb * blk:(bb + 1) * blk]
            m = jnp.max(s, axis=1, keepdims=True)
            pr = jnp.exp2(s - m)
            m_cols.append(m)
            l_cols.append(jnp.sum(pr, axis=1, keepdims=True))
            probs[bb, kv] = pr
        here = lane == step * bps + bb
        gate = jnp.where(here, jnp.concatenate(g_cols, axis=0), gate)
        m_blk = jnp.where(here, jnp.concatenate(m_cols, axis=0), m_blk)
        l_blk = jnp.where(here, jnp.concatenate(l_cols, axis=0), l_blk)
    for bb in range(bps):
        for kv in range(n_kv):
            v = _head_rows(v_pages[2 * bb:2 * bb + 2], kv, n_kv)
            acc_s[step * bps + bb, kv_rows[kv], :] = jnp.dot(probs[bb, kv], v, preferred_element_type=F32)
    gate_s[...] = gate
    m_s[...] = m_blk
    l_s[...] = l_blk

    @pl.when(step == n_steps - 1)
    def _():
        t_row = lax.broadcasted_iota(jnp.int32, (rpk, LANES), 0) & (n_new - 1)
        c_col = lax.broadcasted_iota(jnp.int32, (rpk, LANES), 1)
        own_ok = c_col <= t_row
        t_all = lax.broadcasted_iota(jnp.int32, (rows_all, LANES), 0) & (n_new - 1)
        own_blk = jnp.right_shift(past_len + t_all, MOBA_BLOCK.bit_length() - 1)
        valid = (lane < own_blk) & (lane < nb)
        sel = _topk_lanes(gate, valid, n_sel)
        for kv in range(n_kv):
            rows = slice(kv * rpk, (kv + 1) * rpk)
            q = q_ref[rows, :]
            s = lax.dot_general(q, kn_ref[kv], _NT, preferred_element_type=F32)
            s = jnp.where(own_ok, s, NEG_INF)
            m_o = jnp.max(s, axis=1, keepdims=True)
            m_sel = jnp.where(sel[rows, :], m_blk[rows, :], NEG_INF)
            m_all = jnp.maximum(m_o, jnp.max(m_sel, axis=1, keepdims=True))
            pr = jnp.exp2(s - m_all)
            w = jnp.where(sel[rows, :], jnp.exp2(m_sel - m_all), 0.0)
            den = jnp.sum(pr, axis=1, keepdims=True) + jnp.sum(w * l_blk[rows, :], axis=1, keepdims=True)
            num = jnp.dot(pr, vn_ref[kv], preferred_element_type=F32)
            for jb in range(nb):
                num = num + w[:, jb:jb + 1] * acc_s[jb, rows, :]
            o_ref[rows, :] = num / den


def _moba_sample(pt_flat, cache_k, cache_v, layer, k_new_pad, v_new_pad, q_rows, *, dec_batch, n_pages, n_new,
                 n_kv, groups, past_len):
    assert past_len % MOBA_BLOCK == 0 and n_new < MOBA_BLOCK and MOBA_BLOCK == 2 * LANES
    assert n_new & (n_new - 1) == 0
    nb = (past_len + n_new) // MOBA_BLOCK
    assert 1 <= nb <= LANES and n_pages == 2 * nb
    bps = min(PAGES_PER_STEP // 2, nb)
    assert nb % bps == 0
    n_steps = nb // bps
    rows = n_kv * groups * n_new
    prow = cache_k.shape[2]
    kern = functools.partial(_moba_sample_kernel, bps=bps, n_steps=n_steps, nb=nb, n_sel=min(MOBA_TOPK, nb),
                             n_new=n_new, n_kv=n_kv, groups=groups, past_len=past_len)
    page_specs = _page_specs((None, None, prow, HEAD_DIM), layer, n_pages, 2 * bps)
    grid_spec = pltpu.PrefetchScalarGridSpec(
        num_scalar_prefetch=1,
        grid=(dec_batch, n_steps),
        in_specs=page_specs + page_specs
        + [pl.BlockSpec((None, n_kv, LANES, HEAD_DIM), lambda b, s, pt: (b, 0, 0, 0)),
           pl.BlockSpec((None, n_kv, LANES, HEAD_DIM), lambda b, s, pt: (b, 0, 0, 0)),
           pl.BlockSpec((None, rows, HEAD_DIM), lambda b, s, pt: (b, 0, 0))],
        out_specs=pl.BlockSpec((None, rows, HEAD_DIM), lambda b, s, pt: (b, 0, 0)),
        scratch_shapes=[pltpu.VMEM((rows, LANES), F32), pltpu.VMEM((rows, LANES), F32),
                        pltpu.VMEM((rows, LANES), F32), pltpu.VMEM((nb, rows, HEAD_DIM), F32)],
    )
    return pl.pallas_call(
        kern,
        out_shape=jax.ShapeDtypeStruct((dec_batch, rows, HEAD_DIM), F32),
        grid_spec=grid_spec,
        compiler_params=_params(("parallel", "arbitrary")),
        name="moba_sample",
    )(pt_flat, *([cache_k] * (2 * bps)), *([cache_v] * (2 * bps)), k_new_pad, v_new_pad, q_rows)


def _rope_tables(pos, rot, width, n_rope_lanes=LANES):
    half = rot // 2
    inv = jnp.power(ROPE_THETA, -jnp.arange(half, dtype=F32) / half)
    ang = pos.astype(F32)[:, None] * inv[None, :]
    cos, sin = jnp.cos(ang), jnp.sin(ang)
    t = pos.shape[0]
    zeros = jnp.zeros((t, half), F32)
    rest0 = jnp.zeros((t, width - rot), F32)
    c = jnp.concatenate([cos, cos, jnp.ones((t, width - rot), F32)], axis=1)
    sa = jnp.concatenate([-sin, zeros, rest0], axis=1)
    sb = jnp.concatenate([zeros, sin, rest0], axis=1)
    reps = n_rope_lanes // width
    pad = LANES - n_rope_lanes
    c = jnp.concatenate([jnp.tile(c, (1, reps)), jnp.ones((t, pad), F32)], axis=1)
    sa = jnp.concatenate([jnp.tile(sa, (1, reps)), jnp.zeros((t, pad), F32)], axis=1)
    sb = jnp.concatenate([jnp.tile(sb, (1, reps)), jnp.zeros((t, pad), F32)], axis=1)
    return c, sa, sb


def _kv_major_rows(a, db, n_new, n_kv, groups):
    a = a.reshape(db, n_new, n_kv, groups, HEAD_DIM)
    return jnp.transpose(a, (0, 2, 3, 1, 4)).reshape(db, n_kv * groups * n_new, HEAD_DIM)


def _from_kv_major_rows(o, db, n_new, n_kv, groups):
    o = o.reshape(db, n_kv, groups, n_new, HEAD_DIM)
    return jnp.transpose(o, (0, 3, 1, 2, 4)).reshape(db * n_new, n_kv * groups * HEAD_DIM)


def _new_kv_pad(k, db, n_new, n_kv):
    k = jnp.transpose(k.reshape(db, n_new, n_kv, HEAD_DIM), (0, 2, 1, 3))
    return jnp.pad(k, ((0, 0), (0, 0), (0, LANES - n_new), (0, 0)))


def kernel(x_prompt, x_sample, cache_a_k, cache_a_v, cache_a_idx, cache_b_k, cache_b_v, page_table, w_in, norm1_g,
           c_ln_g, c_ln_b, w_spatial, b_spatial, w_branch, w_out, norm2_g, w_ff1, w_ff2, norm_f_g):
    batch, seq, d_model = x_prompt.shape
    db, n_new, _ = x_sample.shape
    depth, n_phys, page, n_kv, _ = cache_a_k.shape
    n_pages = page_table.shape[1]
    past_len = n_pages * page
    n_heads = d_model // 256
    groups = n_heads // n_kv
    aw = n_heads * HEAD_DIM
    kvw = n_kv * HEAD_DIM
    n_idx_heads = d_model // 128
    iw = n_idx_heads * IDX_DIM
    n_cg = w_spatial.shape[1]
    cw = n_cg * C_GROUP_DIM
    rot, idx_rot = HEAD_DIM // 4, IDX_DIM // 4
    assert page == LANES and aw == 2 * kvw and cw % 512 == 0 and seq % CHUNK == 0 and n_new <= CHUNK

    sizes = (aw, kvw, kvw, iw, IDX_DIM, n_idx_heads, aw, kvw, kvw, cw, cw, N_BRANCH * d_model)
    offs = [0]
    for s in sizes:
        offs.append(offs[-1] + s)
    (o_qa, o_ka, o_va, o_qi, o_ki, o_wi, o_qb, o_kb, o_vb, o_cu, o_cv, o_gl, _) = offs

    pos_p = jnp.arange(seq, dtype=jnp.int32)
    pos_s = jnp.tile(past_len + jnp.arange(n_new, dtype=jnp.int32), db)
    tabs = {
        "p": (_rope_tables(pos_p, rot, HEAD_DIM), _rope_tables(pos_p, idx_rot, IDX_DIM),
              _rope_tables(pos_p, idx_rot, IDX_DIM, IDX_DIM)),
        "s": (_rope_tables(pos_s, rot, HEAD_DIM), _rope_tables(pos_s, idx_rot, IDX_DIM),
              _rope_tables(pos_s, idx_rot, IDX_DIM, IDX_DIM)),
    }

    cache_ak = cache_a_k.reshape(depth, n_phys, page * n_kv, HEAD_DIM)
    cache_av = cache_a_v.reshape(depth, n_phys, page * n_kv, HEAD_DIM)
    cache_bk = cache_b_k.reshape(depth, n_phys, page * n_kv, HEAD_DIM)
    cache_bv = cache_b_v.reshape(depth, n_phys, page * n_kv, HEAD_DIM)
    pt_flat = page_table.reshape(-1).astype(jnp.int32)

    xp = x_prompt.reshape(batch * seq, d_model)
    xs = x_sample.reshape(db * n_new, d_model)
    rs = db * n_new
    tri = jnp.tril(jnp.ones((CHUNK, CHUNK), dtype=bool))
    sel_s = min(IDX_TOPK, (past_len + n_new) // 4)

    outs = {k: [] for k in ("pa_k", "pa_v", "pa_i", "pb_k", "pb_v", "sa_k", "sa_v", "sa_i", "sb_k", "sb_v", "sc_v")}

    w_in_t = jnp.swapaxes(w_in, 1, 2)
    cache_idx_t = jnp.swapaxes(cache_a_idx, 2, 3)

    kv_bufs = {(which, name): None for which in "ps" for name in ("ka", "kb", "va", "vb")}

    def project(x, l, which):
        t128, t64, tkiwi = tabs[which]
        xn = _rms_norm(x, norm1_g[l], BF16)

        def proj(n_off, n, out_dtypes, epi="none", extra=(), half=0, out_scale=1.0, **kw):
            return _matmul_ws(xn, w_in_t, l, out_dtypes, epi, extra, half, n_off=n_off, n=n, w_rows_are_n=True,
                              out_scale=out_scale, **kw)

        def kv_proj(name, n_off, epi="none", extra=(), half=0, want_means=False):
            res = proj(n_off, kvw, (BF16,), epi, extra, half, kv_out=(kv_bufs[which, name], depth, n_kv),
                       want_means=want_means)
            kv_bufs[which, name] = res[1]
            return (res[0], res[2]) if want_means else res[0]

        (hqa,) = proj(o_qa, aw, (BF16,), "rope", t128, rot // 2, Q_SCALE)
        (hqb,) = proj(o_qb, aw, (BF16,), "rope", t128, rot // 2, Q_SCALE)
        hka_bf = kv_proj("ka", o_ka, "rope", t128, rot // 2)
        hkb_bf = kv_proj("kb", o_kb, "rope", t128, rot // 2, want_means=(which == "p"))
        hva_bf = kv_proj("va", o_va)
        hvb_bf = kv_proj("vb", o_vb)
        (hqi,) = proj(o_qi, iw, (BF16,), "rope", t64, idx_rot // 2)
        (hkiwi,) = proj(o_ki, LANES, (F32,), "rope", tkiwi, idx_rot // 2)
        (hc,) = proj(o_cu, 2 * cw + N_BRANCH * d_model, (F32,))
        return hqa, hqb, hka_bf, hkb_bf, hva_bf, hvb_bf, hqi, hkiwi, hc

    def new_rows(name, l):
        return kv_bufs["s", name].reshape(depth, rs, kvw)[l]

    def merge_ffn(x, o_a, o_b, o_c, hc, l):
        merged = _merge(o_a, o_b, o_c, w_branch, l, hc, d_model=d_model, gate_col0=2 * cw)
        (x1,) = _matmul_ws(merged, w_out, l, (F32,), "residual", (x,))
        hn = _rms_norm(x1, norm2_g[l], BF16)
        (h1,) = _matmul_ws(hn, w_ff1, l, (BF16,), "relu2", tm=2048)
        (x2,) = _matmul_kt(h1, w_ff2, l, (F32,), "residual", (x1,))
        return x2

    for l in range(depth):
        hqa, hqb, hka_bf, (hkb_bf, kb_means), hva_bf, hvb_bf, hqi, hkiwi, hc = project(xp, l, "p")
        o_a = _dsa_prompt(hqa, hka_bf, hva_bf, hqi, hkiwi, batch=batch, seq=seq, n_heads=n_heads,
                          n_idx_heads=n_idx_heads, groups=groups)
        o_b = _moba_prompt(hqb, hkb_bf, kb_means, hvb_bf, batch=batch, seq=seq, n_heads=n_heads, groups=groups)
        ws_p = jnp.where(tri[None], w_spatial[l], 0).astype(BF16)
        bs_p = b_spatial[l].T
        (o_c,) = _gmlp(hc, c_ln_g[l], c_ln_b[l], ws_p, bs_p, want_vn=False)
        xp = merge_ffn(xp, o_a, o_b, o_c, hc, l)
        outs["pa_i"].append(hkiwi[:, :IDX_DIM].reshape(batch, seq, IDX_DIM))

        hqa, hqb, hka_bf, hkb_bf, hva_bf, hvb_bf, hqi, hkiwi, hc = project(xs, l, "s")
        qi_rows = jnp.transpose(hqi.astype(F32).reshape(db, n_new, n_idx_heads, IDX_DIM), (0, 2, 1, 3)).reshape(
            db, n_idx_heads * n_new, IDX_DIM)
        wi_rows = jnp.transpose(hkiwi[:, IDX_DIM:IDX_DIM + n_idx_heads].reshape(db, n_new, n_idx_heads),
                                (0, 2, 1)).reshape(db, n_idx_heads * n_new, 1)
        ki_new = jnp.pad(hkiwi[:, :IDX_DIM].reshape(db, n_new, IDX_DIM), ((0, 0), (0, LANES - n_new), (0, 0)))
        scores = _dsa_sample_scores(pt_flat, cache_idx_t, l, qi_rows, wi_rows, dec_batch=db, n_pages=n_pages,
                                    n_new=n_new, n_idx_heads=n_idx_heads)
        sel, sel_new = _dsa_sample_select(scores, qi_rows, wi_rows, ki_new, sel_s, n_idx_heads=n_idx_heads)
        qa_rows = _kv_major_rows(hqa.astype(F32), db, n_new, n_kv, groups)
        o_a = _dsa_sample_attn(pt_flat, cache_ak, cache_av, l, _new_kv_pad(new_rows("ka", l), db, n_new, n_kv),
                               _new_kv_pad(new_rows("va", l), db, n_new, n_kv), qa_rows, sel, sel_new, dec_batch=db,
                               n_pages=n_pages, n_new=n_new, n_kv=n_kv, groups=groups)
        o_a = _from_kv_major_rows(o_a, db, n_new, n_kv, groups).astype(BF16)
        qb_rows = _kv_major_rows(hqb.astype(F32), db, n_new, n_kv, groups)
        o_b = _moba_sample(pt_flat, cache_bk, cache_bv, l, _new_kv_pad(new_rows("kb", l), db, n_new, n_kv),
                           _new_kv_pad(new_rows("vb", l), db, n_new, n_kv), qb_rows, dec_batch=db, n_pages=n_pages,
                           n_new=n_new, n_kv=n_kv, groups=groups, past_len=past_len)
        o_b = _from_kv_major_rows(o_b, db, n_new, n_kv, groups).astype(BF16)
        ws_c = jnp.where(tri[None, :n_new, :n_new], w_spatial[l][:, :n_new, :n_new], 0)
        ws_s = jnp.einsum("ab,gij->gaibj", jnp.eye(db, dtype=F32), ws_c).reshape(n_cg, rs, rs).astype(BF16)
        bs_s = jnp.tile(b_spatial[l][:, :n_new].T, (db, 1))
        o_c, vn = _gmlp(hc, c_ln_g[l], c_ln_b[l], ws_s, bs_s, want_vn=True)
        xs = merge_ffn(xs, o_a, o_b, o_c, hc, l)
        outs["sa_i"].append(hkiwi[:, :IDX_DIM].reshape(db, n_new, IDX_DIM))
        outs["sc_v"].append(vn.reshape(db, n_new, cw))

    y_prompt = _rms_norm(xp, norm_f_g, F32).reshape(batch, seq, d_model)
    y_sample = _rms_norm(xs, norm_f_g, F32).reshape(db, n_new, d_model)

    def kv_result(which, name):
        lead = (depth, batch, seq) if which == "p" else (depth, db, n_new)
        return kv_bufs[which, name].reshape(lead + (n_kv, HEAD_DIM))

    return (y_prompt, y_sample, kv_result("p", "ka"), kv_result("p", "va"), jnp.stack(outs["pa_i"]),
            kv_result("p", "kb"), kv_result("p", "vb"), kv_result("s", "ka"), kv_result("s", "va"),
            jnp.stack(outs["sa_i"]), kv_result("s", "kb"), kv_result("s", "vb"), jnp.stack(outs["sc_v"]))
```

```python
import functools
import math

import jax
import jax.numpy as jnp
from jax import lax
from jax.experimental import pallas as pl
from jax.experimental.pallas import tpu as pltpu

HEAD_DIM = 128
ROPE_THETA = 500000.0
IDX_DIM = 64
IDX_TOPK = 256
MOBA_BLOCK = 256
MOBA_TOPK = 3
CHUNK = 128
C_GROUP_DIM = 128
N_BRANCH = 3
EPS = 1e-6
LANES = 128
INT32_MIN = -(2 ** 31)
PAGES_PER_STEP = 16
Q_SCALE = HEAD_DIM ** -0.5 * math.log2(math.e)

F32 = jnp.float32
BF16 = jnp.bfloat16
NEG_INF = float("-inf")
VMEM_LIMIT = 56 * 1024 * 1024

_NT = (((1,), (1,)), ((), ()))


def _params(sem):
    return pltpu.CompilerParams(dimension_semantics=sem, vmem_limit_bytes=VMEM_LIMIT)


def _rms_kernel(x_ref, g_ref, o_ref):
    x = x_ref[...]
    y = x * lax.rsqrt(jnp.mean(x * x, axis=-1, keepdims=True) + EPS)
    o_ref[...] = (y * g_ref[...]).astype(o_ref.dtype)


def _rms_norm(x, g, out_dtype):
    m, d = x.shape
    tm = min(1024, m)
    return pl.pallas_call(
        _rms_kernel,
        out_shape=jax.ShapeDtypeStruct((m, d), out_dtype),
        grid=(m // tm,),
        in_specs=[pl.BlockSpec((tm, d), lambda i: (i, 0)),
                  pl.BlockSpec((1, d), lambda i: (0, 0))],
        out_specs=pl.BlockSpec((tm, d), lambda i: (i, 0)),
        compiler_params=_params(("parallel",)),
        name="rms_norm",
    )(x, g.reshape(1, d))


def _rope_epilogue(r, c, sa, sb, half):
    outs = []
    for s in range(r.shape[1] // LANES):
        a = r[:, s * LANES:(s + 1) * LANES]
        outs.append(a * c + pltpu.roll(a, LANES - half, 1) * sa + pltpu.roll(a, half, 1) * sb)
    return outs[0] if len(outs) == 1 else jnp.concatenate(outs, axis=1)


_N_EXTRA = {"none": 0, "relu2": 0, "rope": 3, "residual": 1}


def _epilogue(r, epi, extra, half, outs, out_scale=1.0):
    if epi == "rope":
        r = _rope_epilogue(r, extra[0][...], extra[1][...], extra[2][...], half)
        if out_scale != 1.0:
            r = r * out_scale
    elif epi == "relu2":
        r = jnp.square(jnp.maximum(r, 0.0))
    elif epi == "residual":
        r = extra[0][...] + r
    for o_ref in outs:
        o_ref[...] = r.astype(o_ref.dtype)
    return r


def _mm_ws_kernel(*refs, epi, n_out, half, w_rows_are_n, out_scale, n_kv, has_buf, want_means):
    a_ref, w_ref = refs[0], refs[1]
    extra = refs[2:2 + _N_EXTRA[epi]]
    first_out = 2 + _N_EXTRA[epi] + (1 if has_buf else 0)
    outs = refs[first_out:first_out + n_out]
    wbf_s = refs[-1]

    @pl.when(pl.program_id(1) == 0)
    def _():
        wbf_s[...] = (w_ref[0] if w_rows_are_n else w_ref[...]).astype(BF16)

    if w_rows_are_n:
        part = lax.dot_general(a_ref[...], wbf_s[...], _NT, preferred_element_type=F32)
    else:
        part = jnp.dot(a_ref[...], wbf_s[...], preferred_element_type=F32)
    r = _epilogue(part, epi, extra, half, outs, out_scale)
    if n_kv:
        kv_ref = refs[first_out + n_out]
        tm = r.shape[0]
        for kv in range(n_kv):
            kv_ref[pl.ds(kv, tm, stride=n_kv), :] = r[:, kv * HEAD_DIM:(kv + 1) * HEAD_DIM]
    if want_means:
        mean_ref = refs[first_out + n_out + 1]
        for g in range(r.shape[0] // MOBA_BLOCK):
            mean_ref[g:g + 1, :] = jnp.mean(r[g * MOBA_BLOCK:(g + 1) * MOBA_BLOCK, :], axis=0, keepdims=True)


def _matmul_ws(a, w_stack, layer, out_dtypes, epi="none", extra=(), half=0, n_off=0, n=None, w_rows_are_n=False,
               out_scale=1.0, kv_out=None, want_means=False, tm=1024, tn=1024):
    m, kdim = a.shape
    n_tot = w_stack.shape[1] if w_rows_are_n else w_stack.shape[2]
    n = n_tot - n_off if n is None else n
    tm, tn = min(tm, m), min(tn, n)
    if epi == "rope":
        tm = min(tm, extra[0].shape[0])
    assert m % tm == 0 and n % tn == 0 and n_off + n <= n_tot, (a.shape, w_stack.shape, tm, tn)
    in_specs = [pl.BlockSpec((tm, kdim), lambda j, i: (i, 0))]
    if w_rows_are_n:
        assert n_off % 8 == 0 and n_off + (n // tn) * tn <= n_tot
        in_specs.append(pl.BlockSpec((pl.Element(1), pl.Element(tn), pl.Element(kdim)),
                                     lambda j, i: (layer, pl.multiple_of(n_off + j * tn, 8), 0)))
        w_scratch = pltpu.VMEM((tn, kdim), BF16)
    else:
        assert n_off % tn == 0
        in_specs.append(pl.BlockSpec((None, kdim, tn), lambda j, i: (layer, 0, n_off // tn + j)))
        w_scratch = pltpu.VMEM((kdim, tn), BF16)
    if epi == "rope":
        nt = extra[0].shape[0] // tm
        in_specs += [pl.BlockSpec((tm, LANES), lambda j, i: (i % nt, 0))] * 3
    elif epi == "residual":
        in_specs += [pl.BlockSpec((tm, tn), lambda j, i: (i, j))]
    out_shape = [jax.ShapeDtypeStruct((m, n), dt) for dt in out_dtypes]
    out_specs = [pl.BlockSpec((tm, tn), lambda j, i: (i, j)) for _ in out_dtypes]
    operands, aliases, n_kv = [a, w_stack, *extra], {}, 0
    if kv_out is not None:
        buf, n_layers, n_kv = kv_out
        assert n == tn == n_kv * HEAD_DIM
        if buf is not None:
            aliases = {len(operands): len(out_shape)}
            operands.append(buf)
            in_specs.append(pl.BlockSpec(memory_space=pl.ANY))
        out_shape.append(jax.ShapeDtypeStruct((n_layers * m * n_kv, HEAD_DIM), F32))
        out_specs.append(pl.BlockSpec((tm * n_kv, HEAD_DIM), lambda j, i: (layer * (m // tm) + i, 0)))
    if want_means:
        assert kv_out is not None and tm % MOBA_BLOCK == 0
        out_shape.append(jax.ShapeDtypeStruct((m // tm, tm // MOBA_BLOCK, n), F32))
        out_specs.append(pl.BlockSpec((None, tm // MOBA_BLOCK, n), lambda j, i: (i, 0, 0)))
    outs = pl.pallas_call(
        functools.partial(_mm_ws_kernel, epi=epi, n_out=len(out_dtypes), half=half, w_rows_are_n=w_rows_are_n,
                          out_scale=out_scale, n_kv=n_kv, has_buf=bool(aliases), want_means=want_means),
        out_shape=out_shape,
        grid=(n // tn, m // tm),
        in_specs=in_specs,
        out_specs=out_specs,
        scratch_shapes=[w_scratch],
        input_output_aliases=aliases,
        compiler_params=_params(("parallel", "arbitrary")),
        name="matmul_" + epi,
    )(*operands)
    return tuple(outs)


def _mm_kt_kernel(*refs, epi, nk, n_out, half):
    a_ref, w_ref = refs[0], refs[1]
    extra = refs[2:2 + _N_EXTRA[epi]]
    outs = refs[2 + _N_EXTRA[epi]:2 + _N_EXTRA[epi] + n_out]
    acc_ref = refs[-1]
    k = pl.program_id(2)
    part = jnp.dot(a_ref[...], w_ref[...].astype(BF16), preferred_element_type=F32)

    @pl.when(k == 0)
    def _():
        acc_ref[...] = part

    @pl.when(k > 0)
    def _():
        acc_ref[...] += part

    @pl.when(k == nk - 1)
    def _():
        _epilogue(acc_ref[...], epi, extra, half, outs)


def _matmul_kt(a, w_stack, layer, out_dtypes, epi="none", extra=(), tm=1024, tn=1024, tk=2048):
    m, kdim = a.shape
    n = w_stack.shape[2]
    tm, tn, tk = min(tm, m), min(tn, n), min(tk, kdim)
    assert m % tm == 0 and n % tn == 0 and kdim % tk == 0 and kdim > tk, (a.shape, w_stack.shape, tm, tn, tk)
    nk = kdim // tk
    in_specs = [pl.BlockSpec((tm, tk), lambda i, j, k: (i, k)),
                pl.BlockSpec((None, tk, tn), lambda i, j, k: (layer, k, j))]
    if epi == "residual":
        in_specs += [pl.BlockSpec((tm, tn), lambda i, j, k: (i, j))]
    outs = pl.pallas_call(
        functools.partial(_mm_kt_kernel, epi=epi, nk=nk, n_out=len(out_dtypes), half=0),
        out_shape=[jax.ShapeDtypeStruct((m, n), dt) for dt in out_dtypes],
        grid=(m // tm, n // tn, nk),
        in_specs=in_specs,
        out_specs=[pl.BlockSpec((tm, tn), lambda i, j, k: (i, j)) for _ in out_dtypes],
        scratch_shapes=[pltpu.VMEM((tm, tn), F32)],
        compiler_params=_params(("parallel", "parallel", "arbitrary")),
        name="matmul_kt_" + epi,
    )(a, w_stack, *extra)
    return tuple(outs)


def _ordered_key(score):
    bits = pltpu.bitcast(jnp.where(score == 0.0, 0.0, score), jnp.int32)
    return jnp.where(bits < 0, bits ^ jnp.int32(0x7FFFFFFF), bits)


def _kth_largest_key(count_ge, shape, n_sel):
    def body(b, t):
        cand = t + lax.shift_left(jnp.int32(1), jnp.int32(31) - b)
        return jnp.where(count_ge(cand) >= float(n_sel), cand, t)

    return lax.fori_loop(0, 32, body, jnp.full(shape, INT32_MIN, jnp.int32))


def _topk_lanes(g, valid, n_sel):
    lane = lax.broadcasted_iota(jnp.int32, g.shape, 1).astype(F32)
    gm = jnp.where(valid, g, NEG_INF)
    sel = jnp.zeros(g.shape, jnp.bool_)
    for _ in range(n_sel):
        mx = jnp.max(gm, axis=1, keepdims=True)
        first = jnp.min(jnp.where(gm == mx, lane, float(LANES)), axis=1, keepdims=True)
        pick = lane == first
        sel = sel | pick
        gm = jnp.where(pick, NEG_INF, gm)
    return sel & valid


def _softmax_pv(logits2, v):
    p = jnp.exp2(logits2 - jnp.max(logits2, axis=1, keepdims=True))
    l = jnp.sum(p, axis=1, keepdims=True)
    return jnp.dot(p.astype(BF16), v, preferred_element_type=F32) / l


def _softmax_pv_t(logits2_t, vt):
    p = jnp.exp2(logits2_t - jnp.max(logits2_t, axis=0, keepdims=True))
    l = jnp.sum(p, axis=0, keepdims=True)
    ot = jnp.dot(vt, p.astype(BF16), preferred_element_type=F32)
    return jnp.transpose(ot / l)


def _topk_rows(g, valid, n_sel):
    rid = lax.broadcasted_iota(jnp.int32, g.shape, 0).astype(F32)
    gm = jnp.where(valid, g, NEG_INF)
    sel = jnp.zeros(g.shape, jnp.bool_)
    for _ in range(n_sel):
        mx = jnp.max(gm, axis=0, keepdims=True)
        first = jnp.min(jnp.where(gm == mx, rid, float(g.shape[0])), axis=0, keepdims=True)
        pick = rid == first
        sel = sel | pick
        gm = jnp.where(pick, NEG_INF, gm)
    return sel & valid


def _width_classes(seq, tile):
    widths = sorted({w for w in (seq // 4, seq // 2, seq) if w % tile == 0 and w >= tile})
    classes, lo = [], 0
    for w in widths:
        classes.append((lo, w // tile, w))
        lo = w // tile
    return classes


def _dsa_prompt_kernel(qa_ref, ka_ref, va_ref, qi_ref, ki_ref, wi_ref, o_ref, ki_s, qcat_s, wt_s, key_s, bias_s,
                       tie_s, *, tq, seq, n_sel, n_heads, n_idx_heads, groups):
    i = pl.program_id(1)
    n_valid = i + 1
    kblk = 32

    @pl.when(i == 0)
    def _():
        ki_s[...] = ki_ref[:, :IDX_DIM].astype(BF16)

    for h in range(n_idx_heads):
        qcat_s[h * tq:(h + 1) * tq, :] = qi_ref[:, h * IDX_DIM:(h + 1) * IDX_DIM]
    wt_s[...] = jnp.transpose(wi_ref[...]) * ((IDX_DIM ** -0.5) * (n_idx_heads ** -0.5))

    krow = lax.broadcasted_iota(jnp.int32, (tq, tq), 0)
    qcol = lax.broadcasted_iota(jnp.int32, (tq, tq), 1)

    def visible(c):
        return krow <= qcol + jnp.where(c == i, 0, tq)

    for c in range(seq // tq):
        @pl.when(c < n_valid)
        def _(c=c):
            d_all = lax.dot_general(ki_s[c * tq:(c + 1) * tq, :], qcat_s[...], _NT,
                                    preferred_element_type=F32)
            for kb in range(tq // kblk):
                rows = slice(kb * kblk, (kb + 1) * kblk)
                s = None
                for h in range(n_idx_heads):
                    term = jnp.maximum(d_all[rows, h * tq:(h + 1) * tq], 0.0) * wt_s[IDX_DIM + h:IDX_DIM + h + 1, :]
                    s = term if s is None else s + term
                key_s[c, rows, :] = _ordered_key(jnp.where(visible(c)[rows, :], s, NEG_INF))

    def count_ge(cand):
        def body(c, acc):
            hit = (key_s[c] >= cand).astype(F32)
            return acc + jnp.sum(hit.reshape(4, tq // 32, 8, tq), axis=1)

        acc = lax.fori_loop(0, n_valid, body, jnp.zeros((4, 8, tq), F32))
        return jnp.sum(jnp.sum(acc, axis=0), axis=0, keepdims=True)

    thr = _kth_largest_key(count_ge, (1, tq), n_sel)
    need = float(n_sel) - count_ge(thr + 1)
    prefix_ones = jnp.where(qcol <= krow, 1.0, 0.0).astype(BF16)
    tie_s[...] = jnp.zeros_like(tie_s)

    def kv_cols(h):
        return slice((h // groups) * HEAD_DIM, (h // groups + 1) * HEAD_DIM)

    def attend(width):
        for c in range(width // tq):
            @pl.when(c < n_valid)
            def _(c=c):
                k, vis = key_s[c], visible(c)
                tie = (k == thr) & vis
                rank = jnp.dot(prefix_ones, jnp.where(tie, 1.0, 0.0).astype(BF16),
                               preferred_element_type=F32) + tie_s[...]
                ok = ((k > thr) & vis) | (tie & (rank <= need))
                bias_s[:, c * tq:(c + 1) * tq] = jnp.transpose(jnp.where(ok, 0.0, NEG_INF))
                tie_s[...] = rank[tq - 1:tq, :]

            @pl.when(c >= n_valid)
            def _(c=c):
                bias_s[:, c * tq:(c + 1) * tq] = jnp.full((tq, tq), NEG_INF, F32)

        def qk(h):
            return lax.dot_general(qa_ref[:, h * HEAD_DIM:(h + 1) * HEAD_DIM], ka_ref[0:width, kv_cols(h)], _NT,
                                   preferred_element_type=F32)

        raw = qk(0)
        for h in range(n_heads):
            nxt = qk(h + 1) if h + 1 < n_heads else None
            o = _softmax_pv(raw + bias_s[:, 0:width], va_ref[0:width, kv_cols(h)])
            o_ref[:, h * HEAD_DIM:(h + 1) * HEAD_DIM] = o.astype(o_ref.dtype)
            raw = nxt

    for lo, hi, width in _width_classes(seq, tq):
        @pl.when((i >= lo) & (i < hi))
        def _(width=width):
            attend(width)


def _dsa_prompt(hq, hk_bf, hv_bf, hqi, hkiwi, *, batch, seq, n_heads, n_idx_heads, groups):
    tq = min(256, seq)
    assert seq % tq == 0 and tq % LANES == 0
    nq = seq // tq
    aw = n_heads * HEAD_DIM
    kvw = aw // groups
    n_sel = min(IDX_TOPK, seq // 4)
    kern = functools.partial(_dsa_prompt_kernel, tq=tq, seq=seq, n_sel=n_sel, n_heads=n_heads,
                             n_idx_heads=n_idx_heads, groups=groups)
    return pl.pallas_call(
        kern,
        out_shape=jax.ShapeDtypeStruct((batch * seq, aw), BF16),
        grid=(batch, nq),
        in_specs=[pl.BlockSpec((tq, aw), lambda b, i: (b * nq + i, 0)),
                  pl.BlockSpec((seq, kvw), lambda b, i: (b, 0)),
                  pl.BlockSpec((seq, kvw), lambda b, i: (b, 0)),
                  pl.BlockSpec((tq, n_idx_heads * IDX_DIM), lambda b, i: (b * nq + i, 0)),
                  pl.BlockSpec((seq, LANES), lambda b, i: (b, 0)),
                  pl.BlockSpec((tq, LANES), lambda b, i: (b * nq + i, 0))],
        out_specs=pl.BlockSpec((tq, aw), lambda b, i: (b * nq + i, 0)),
        scratch_shapes=[pltpu.VMEM((seq, IDX_DIM), BF16), pltpu.VMEM((n_idx_heads * tq, IDX_DIM), BF16),
                        pltpu.VMEM((LANES, tq), F32), pltpu.VMEM((nq, tq, tq), jnp.int32),
                        pltpu.VMEM((tq, seq), F32), pltpu.VMEM((1, tq), F32)],
        compiler_params=_params(("parallel", "arbitrary")),
        name="dsa_prompt",
    )(hq, hk_bf, hv_bf, hqi, hkiwi, hkiwi)


def _moba_prompt_kernel(qb_ref, kb_ref, kmean_ref, vb_ref, o_ref, km_s, vt_s, *, nb, n_sel, n_heads, groups):
    i = pl.program_id(1)
    blk = MOBA_BLOCK

    @pl.when(i == 0)
    def _():
        km_s[...] = jnp.zeros_like(km_s)
        per = kmean_ref.shape[1]
        for j in range(nb):
            km_s[j:j + 1, :] = kmean_ref[j // per, j % per:j % per + 1, :]
        for kv in range(n_heads // groups):
            v = vb_ref[:, kv * HEAD_DIM:(kv + 1) * HEAD_DIM].astype(F32)
            vt_s[kv] = jnp.transpose(v).astype(BF16)

    past = lax.broadcasted_iota(jnp.int32, (LANES, blk), 0) < i
    tril_bias = jnp.where(lax.broadcasted_iota(jnp.int32, (blk, blk), 0)
                          <= lax.broadcasted_iota(jnp.int32, (blk, blk), 1), 0.0, NEG_INF)

    def kv_cols(h):
        return slice((h // groups) * HEAD_DIM, (h // groups + 1) * HEAD_DIM)

    def attend(width):
        def qk(h):
            q = qb_ref[:, h * HEAD_DIM:(h + 1) * HEAD_DIM]
            gate = lax.dot_general(km_s[:, kv_cols(h)].astype(BF16), q, _NT, preferred_element_type=F32)
            return gate, lax.dot_general(kb_ref[0:width, kv_cols(h)], q, _NT, preferred_element_type=F32)

        cur = qk(0)
        for h in range(n_heads):
            nxt = qk(h + 1) if h + 1 < n_heads else None
            gate, raw = cur
            sel_bias = jnp.where(_topk_rows(gate, past, n_sel), 0.0, NEG_INF)
            pieces = []
            for j in range(width // blk):
                bias = jnp.where(j == i, tril_bias, sel_bias[j:j + 1, :])
                pieces.append(raw[j * blk:(j + 1) * blk, :] + bias)
            o = _softmax_pv_t(jnp.concatenate(pieces, axis=0), vt_s[h // groups, :, 0:width])
            o_ref[:, h * HEAD_DIM:(h + 1) * HEAD_DIM] = o.astype(o_ref.dtype)
            cur = nxt

    for lo, hi, width in _width_classes(nb * blk, blk):
        @pl.when((i >= lo) & (i < hi))
        def _(width=width):
            attend(width)


def _moba_prompt(hq, hk_bf, k_means, hv_bf, *, batch, seq, n_heads, groups):
    blk = MOBA_BLOCK
    assert seq % blk == 0
    nb = seq // blk
    g_per_seq = k_means.shape[0] // batch
    assert 1 <= nb <= LANES and g_per_seq * k_means.shape[1] == nb
    aw = n_heads * HEAD_DIM
    kvw = aw // groups
    kern = functools.partial(_moba_prompt_kernel, nb=nb, n_sel=min(MOBA_TOPK, nb), n_heads=n_heads,
                             groups=groups)
    return pl.pallas_call(
        kern,
        out_shape=jax.ShapeDtypeStruct((batch * seq, aw), BF16),
        grid=(batch, nb),
        in_specs=[pl.BlockSpec((blk, aw), lambda b, i: (b * nb + i, 0)),
                  pl.BlockSpec((seq, kvw), lambda b, i: (b, 0)),
                  pl.BlockSpec((g_per_seq, k_means.shape[1], kvw), lambda b, i: (b, 0, 0)),
                  pl.BlockSpec((seq, kvw), lambda b, i: (b, 0))],
        out_specs=pl.BlockSpec((blk, aw), lambda b, i: (b * nb + i, 0)),
        scratch_shapes=[pltpu.VMEM((LANES, kvw), F32), pltpu.VMEM((n_heads // groups, HEAD_DIM, seq), BF16)],
        compiler_params=_params(("parallel", "arbitrary")),
        name="moba_prompt",
    )(hq, hk_bf, k_means, hv_bf)


def _gmlp_kernel(cu_ref, cv_ref, g_ref, b_ref, ws_ref, bs_ref, o_ref, *vn_out, n_groups):
    r = ws_ref.shape[1]
    for ch in range(cv_ref.shape[0] // r):
        rows = slice(ch * r, (ch + 1) * r)
        v = cv_ref[rows, :]
        mu = jnp.mean(v, axis=-1, keepdims=True)
        vc = v - mu
        vn = vc * lax.rsqrt(jnp.mean(vc * vc, axis=-1, keepdims=True) + EPS) * g_ref[...] + b_ref[...]
        if vn_out:
            vn_out[0][rows, :] = vn
        for g in range(n_groups):
            cols = slice(g * C_GROUP_DIM, (g + 1) * C_GROUP_DIM)
            z = jnp.dot(ws_ref[g], vn[:, cols].astype(BF16), preferred_element_type=F32) + bs_ref[:, g:g + 1]
            o_ref[rows, cols] = (cu_ref[rows, cols] * z).astype(o_ref.dtype)


def _gmlp(hc, ln_g, ln_b, ws, bs, *, want_vn):
    m = hc.shape[0]
    n_groups, chunk = ws.shape[0], ws.shape[1]
    cw = n_groups * C_GROUP_DIM
    r = min(m, 8 * chunk)
    assert m % r == 0 and r % chunk == 0
    out_shape = [jax.ShapeDtypeStruct((m, cw), BF16)]
    out_specs = [pl.BlockSpec((r, cw), lambda i: (i, 0))]
    if want_vn:
        out_shape.append(jax.ShapeDtypeStruct((m, cw), F32))
        out_specs.append(pl.BlockSpec((r, cw), lambda i: (i, 0)))
    return pl.pallas_call(
        functools.partial(_gmlp_kernel, n_groups=n_groups),
        out_shape=out_shape,
        grid=(m // r,),
        in_specs=[pl.BlockSpec((r, cw), lambda i: (i, 0)),
                  pl.BlockSpec((r, cw), lambda i: (i, 1)),
                  pl.BlockSpec((1, cw), lambda i: (0, 0)),
                  pl.BlockSpec((1, cw), lambda i: (0, 0)),
                  pl.BlockSpec((n_groups, chunk, chunk), lambda i: (0, 0, 0)),
                  pl.BlockSpec((chunk, n_groups), lambda i: (0, 0))],
        out_specs=out_specs,
        compiler_params=_params(("parallel",)),
        name="gmlp",
    )(hc, hc, ln_g.reshape(1, cw), ln_b.reshape(1, cw), ws, bs)


def _merge_kernel(oa_ref, ob_ref, oc_ref, wb_ref, g0_ref, g1_ref, g2_ref, o_ref, wbf_s):
    @pl.when(pl.program_id(1) == 0)
    def _():
        wbf_s[...] = wb_ref[...].astype(BF16)

    acc = None
    for o_n, g_n, n in ((oa_ref, g0_ref, 0), (ob_ref, g1_ref, 1), (oc_ref, g2_ref, 2)):
        br = jnp.dot(o_n[...], wbf_s[n], preferred_element_type=F32)
        term = (1.0 / (1.0 + jnp.exp(-g_n[...]))) * br
        acc = term if acc is None else acc + term
    o_ref[...] = acc.astype(o_ref.dtype)


def _merge(o_a, o_b, o_c, wb_stack, layer, hc, *, d_model, gate_col0):
    m, aw = o_a.shape
    tm, tn = min(1024, m), 512
    nj = d_model // tn
    g0 = gate_col0 // tn
    gate_specs = [pl.BlockSpec((tm, tn), functools.partial(lambda j, i, n: (i, g0 + n * nj + j), n=n))
                  for n in range(N_BRANCH)]
    return pl.pallas_call(
        _merge_kernel,
        out_shape=jax.ShapeDtypeStruct((m, d_model), BF16),
        grid=(nj, m // tm),
        in_specs=[pl.BlockSpec((tm, aw), lambda j, i: (i, 0))] * 3
        + [pl.BlockSpec((None, N_BRANCH, aw, tn), lambda j, i: (layer, 0, 0, j))] + gate_specs,
        out_specs=pl.BlockSpec((tm, tn), lambda j, i: (i, j)),
        scratch_shapes=[pltpu.VMEM((N_BRANCH, aw, tn), BF16)],
        compiler_params=_params(("parallel", "arbitrary")),
        name="merge",
    )(o_a, o_b, o_c, wb_stack, hc, hc, hc)


def _page_specs(block, layer, n_pages, per_step):
    def page_map(b, s, pt, slot):
        return (layer, pt[b * n_pages + s * per_step + slot], 0, 0)

    return [pl.BlockSpec(block, functools.partial(page_map, slot=slot)) for slot in range(per_step)]


def _idx_scores(qi, wi, k, n_new, n_idx_heads, k_is_t=False):
    if k_is_t:
        d = jnp.dot(qi, k, preferred_element_type=F32) * (IDX_DIM ** -0.5)
    else:
        d = lax.dot_general(qi, k, _NT, preferred_element_type=F32) * (IDX_DIM ** -0.5)
    dw = jnp.maximum(d, 0.0) * wi
    acc = dw[0:8]
    for r in range(8, n_idx_heads * n_new, 8):
        acc = acc + dw[r:r + 8]
    s = acc[0:n_new]
    for r in range(n_new, 8, n_new):
        s = s + acc[r:r + n_new]
    return s * (n_idx_heads ** -0.5)


def _dsa_sample_score_kernel(pt_ref, *refs, per_step, n_new, n_idx_heads):
    pages = refs[:per_step]
    qi_ref, wi_ref, o_ref = refs[per_step:]
    keys_t = jnp.concatenate([p[...] for p in pages], axis=1) if per_step > 1 else pages[0][...]
    o_ref[...] = _idx_scores(qi_ref[...], wi_ref[...], keys_t, n_new, n_idx_heads, k_is_t=True)


def _dsa_sample_scores(pt_flat, cache_idx_t, layer, qi_rows, wi_rows, *, dec_batch, n_pages, n_new, n_idx_heads):
    page = cache_idx_t.shape[3]
    assert page == LANES and 8 % n_new == 0 and (n_idx_heads * n_new) % 8 == 0
    per_step = min(PAGES_PER_STEP, n_pages)
    assert n_pages % per_step == 0
    rows = n_idx_heads * n_new
    kern = functools.partial(_dsa_sample_score_kernel, per_step=per_step, n_new=n_new, n_idx_heads=n_idx_heads)
    grid_spec = pltpu.PrefetchScalarGridSpec(
        num_scalar_prefetch=1,
        grid=(dec_batch, n_pages // per_step),
        in_specs=_page_specs((None, None, IDX_DIM, page), layer, n_pages, per_step)
        + [pl.BlockSpec((None, rows, IDX_DIM), lambda b, s, pt: (b, 0, 0)),
           pl.BlockSpec((None, rows, 1), lambda b, s, pt: (b, 0, 0))],
        out_specs=pl.BlockSpec((None, n_new, per_step * LANES), lambda b, s, pt: (b, 0, s)),
    )
    return pl.pallas_call(
        kern,
        out_shape=jax.ShapeDtypeStruct((dec_batch, n_new, n_pages * LANES), F32),
        grid_spec=grid_spec,
        compiler_params=_params(("parallel", "arbitrary")),
        name="dsa_sample_scores",
    )(pt_flat, *([cache_idx_t] * per_step), qi_rows, wi_rows)


def _select_kernel(s_ref, qi_ref, wi_ref, kn_ref, o_ref, on_ref, key_s, keyn_s, *, n_sel, n_new, n_idx_heads, db):
    rows = db * n_new
    t = lax.broadcasted_iota(jnp.int32, (rows, LANES), 0) & (n_new - 1)
    c = lax.broadcasted_iota(jnp.int32, (rows, LANES), 1)
    new_ok = c <= t
    key_s[...] = _ordered_key(s_ref[...])
    for b in range(db):
        s_new = _idx_scores(qi_ref[b], wi_ref[b], kn_ref[b], n_new, n_idx_heads)
        keyn_s[b * n_new:(b + 1) * n_new, :] = _ordered_key(jnp.where(new_ok[0:n_new], s_new, NEG_INF))

    def count_ge(cand):
        return (jnp.sum((key_s[...] >= cand).astype(F32), axis=1, keepdims=True)
                + jnp.sum((keyn_s[...] >= cand).astype(F32), axis=1, keepdims=True))

    thr = _kth_largest_key(count_ge, (rows, 1), n_sel)
    need = float(n_sel) - count_ge(thr + 1)
    prefix_ones = jnp.where(lax.broadcasted_iota(jnp.int32, (LANES, LANES), 0)
                            <= lax.broadcasted_iota(jnp.int32, (LANES, LANES), 1), 1.0, 0.0).astype(BF16)
    seen = jnp.zeros((rows, 1), F32)

    def pick(k, vis, seen):
        tie = (k == thr) & vis
        rank = jnp.dot(jnp.where(tie, 1.0, 0.0).astype(BF16), prefix_ones,
                       preferred_element_type=F32) + seen
        ok = ((k > thr) & vis) | (tie & (rank <= need))
        return jnp.where(ok, 1.0, 0.0), rank[:, LANES - 1:LANES]

    every = c >= 0
    for blk in range(key_s.shape[1] // LANES):
        cols = slice(blk * LANES, (blk + 1) * LANES)
        o_ref[:, cols], seen = pick(key_s[:, cols], every, seen)
    on_ref[...], seen = pick(keyn_s[...], new_ok, seen)


def _dsa_sample_select(scores, qi_rows, wi_rows, ki_new_pad, n_sel, *, n_idx_heads):
    db, n_new, length = scores.shape
    rows = db * n_new
    assert n_new & (n_new - 1) == 0
    whole = pl.BlockSpec(memory_space=pltpu.VMEM)
    sel, sel_new = pl.pallas_call(
        functools.partial(_select_kernel, n_sel=n_sel, n_new=n_new, n_idx_heads=n_idx_heads, db=db),
        out_shape=[jax.ShapeDtypeStruct((rows, length), F32), jax.ShapeDtypeStruct((rows, LANES), F32)],
        in_specs=[whole] * 4,
        out_specs=[whole] * 2,
        scratch_shapes=[pltpu.VMEM((rows, length), jnp.int32), pltpu.VMEM((rows, LANES), jnp.int32)],
        compiler_params=pltpu.CompilerParams(vmem_limit_bytes=VMEM_LIMIT),
        name="dsa_sample_select",
    )(scores.reshape(rows, length), qi_rows, wi_rows, ki_new_pad)
    return sel.reshape(db, n_new, length), sel_new.reshape(db, n_new, LANES)


def _head_rows(page_refs, kv, n_kv):
    parts = [r[pl.ds(kv, LANES, stride=n_kv), :] for r in page_refs]
    return parts[0] if len(parts) == 1 else jnp.concatenate(parts, axis=0)


def _dsa_sample_attn_kernel(pt_ref, *refs, per_step, n_steps, n_new, n_kv, groups):
    k_pages, v_pages = refs[:per_step], refs[per_step:2 * per_step]
    kn_ref, vn_ref, q_ref, sel_ref, seln_ref, o_ref, m_s, l_s, acc_s = refs[2 * per_step:]
    step = pl.program_id(1)
    rpk = groups * n_new

    @pl.when(step == 0)
    def _():
        m_s[...] = jnp.full_like(m_s, NEG_INF)
        l_s[...] = jnp.zeros_like(l_s)
        acc_s[...] = jnp.zeros_like(acc_s)

    def update(get_k, get_v, sel):
        mask = jnp.concatenate([sel] * groups, axis=0) > 0.0
        rows = [slice(kv * rpk, (kv + 1) * rpk) for kv in range(n_kv)]
        logits = [lax.dot_general(q_ref[rows[kv], :], get_k(kv), _NT, preferred_element_type=F32)
                  for kv in range(n_kv)]
        probs, alphas = [], []
        for kv in range(n_kv):
            s = jnp.where(mask, logits[kv], NEG_INF)
            m_old = m_s[rows[kv], :]
            m_new = jnp.maximum(m_old, jnp.max(s, axis=1, keepdims=True))
            m_safe = jnp.where(m_new == NEG_INF, 0.0, m_new)
            alpha = jnp.exp2(m_old - m_safe)
            pr = jnp.exp2(s - m_safe)
            m_s[rows[kv], :] = m_new
            l_s[rows[kv], :] = alpha * l_s[rows[kv], :] + jnp.sum(pr, axis=1, keepdims=True)
            probs.append(pr)
            alphas.append(alpha)
        for kv in range(n_kv):
            acc_s[rows[kv], :] = alphas[kv] * acc_s[rows[kv], :] + jnp.dot(
                probs[kv], get_v(kv), preferred_element_type=F32)

    update(lambda kv: _head_rows(k_pages, kv, n_kv), lambda kv: _head_rows(v_pages, kv, n_kv), sel_ref[...])

    @pl.when(step == n_steps - 1)
    def _():
        update(lambda kv: kn_ref[kv], lambda kv: vn_ref[kv], seln_ref[...])
        o_ref[...] = acc_s[...] / l_s[...]


def _dsa_sample_attn(pt_flat, cache_k, cache_v, layer, k_new_pad, v_new_pad, q_rows, sel, sel_new, *, dec_batch,
                     n_pages, n_new, n_kv, groups):
    rows = n_kv * groups * n_new
    prow = cache_k.shape[2]
    per_step = min(PAGES_PER_STEP, n_pages)
    assert n_pages % per_step == 0
    n_steps = n_pages // per_step
    kern = functools.partial(_dsa_sample_attn_kernel, per_step=per_step, n_steps=n_steps, n_new=n_new, n_kv=n_kv,
                             groups=groups)
    page_specs = _page_specs((None, None, prow, HEAD_DIM), layer, n_pages, per_step)
    grid_spec = pltpu.PrefetchScalarGridSpec(
        num_scalar_prefetch=1,
        grid=(dec_batch, n_steps),
        in_specs=page_specs + page_specs
        + [pl.BlockSpec((None, n_kv, LANES, HEAD_DIM), lambda b, s, pt: (b, 0, 0, 0)),
           pl.BlockSpec((None, n_kv, LANES, HEAD_DIM), lambda b, s, pt: (b, 0, 0, 0)),
           pl.BlockSpec((None, rows, HEAD_DIM), lambda b, s, pt: (b, 0, 0)),
           pl.BlockSpec((None, n_new, per_step * LANES), lambda b, s, pt: (b, 0, s)),
           pl.BlockSpec((None, n_new, LANES), lambda b, s, pt: (b, 0, 0))],
        out_specs=pl.BlockSpec((None, rows, HEAD_DIM), lambda b, s, pt: (b, 0, 0)),
        scratch_shapes=[pltpu.VMEM((rows, 1), F32), pltpu.VMEM((rows, 1), F32),
                        pltpu.VMEM((rows, HEAD_DIM), F32)],
    )
    return pl.pallas_call(
        kern,
        out_shape=jax.ShapeDtypeStruct((dec_batch, rows, HEAD_DIM), F32),
        grid_spec=grid_spec,
        compiler_params=_params(("parallel", "arbitrary")),
        name="dsa_sample_attn",
    )(pt_flat, *([cache_k] * per_step), *([cache_v] * per_step), k_new_pad, v_new_pad, q_rows, sel, sel_new)


def _moba_sample_kernel(pt_ref, *refs, bps, n_steps, nb, n_sel, n_new, n_kv, groups, past_len):
    k_pages, v_pages = refs[:2 * bps], refs[2 * bps:4 * bps]
    kn_ref, vn_ref, q_ref, o_ref, gate_s, m_s, l_s, acc_s = refs[4 * bps:]
    step = pl.program_id(1)
    rpk = groups * n_new
    rows_all = n_kv * rpk
    lane = lax.broadcasted_iota(jnp.int32, (rows_all, LANES), 1)

    @pl.when(step == 0)
    def _():
        gate_s[...] = jnp.zeros_like(gate_s)
        m_s[...] = jnp.zeros_like(m_s)
        l_s[...] = jnp.zeros_like(l_s)

    gate, m_blk, l_blk = gate_s[...], m_s[...], l_s[...]
    blk = MOBA_BLOCK
    kv_rows = [slice(kv * rpk, (kv + 1) * rpk) for kv in range(n_kv)]
    ks = [_head_rows(k_pages, kv, n_kv) for kv in range(n_kv)]
    logits = [lax.dot_general(q_ref[kv_rows[kv], :], ks[kv], _NT, preferred_element_type=F32)
              for kv in range(n_kv)]
    probs = {}
    for bb in range(bps):
        g_cols, m_cols, l_cols = [], [], []
        for kv in range(n_kv):
            kmean = jnp.mean(ks[kv][bb * blk:(bb + 1) * blk, :], axis=0, keepdims=True)
            g_cols.append(jnp.sum(q_ref[kv_rows[kv], :] * kmean, axis=1, keepdims=True))
            s = logits[kv][:, bb * blk:(bb + 1) * blk]
            m = jnp.max(s, axis=1, keepdims=True)
            pr = jnp.exp2(s - m)
            m_cols.append(m)
            l_cols.append(jnp.sum(pr, axis=1, keepdims=True))
            probs[bb, kv] = pr
        here = lane == step * bps + bb
        gate = jnp.where(here, jnp.concatenate(g_cols, axis=0), gate)
        m_blk = jnp.where(here, jnp.concatenate(m_cols, axis=0), m_blk)
        l_blk = jnp.where(here, jnp.concatenate(l_cols, axis=0), l_blk)
    for bb in range(bps):
        for kv in range(n_kv):
            v = _head_rows(v_pages[2 * bb:2 * bb + 2], kv, n_kv)
            acc_s[step * bps + bb, kv_rows[kv], :] = jnp.dot(probs[bb, kv], v, preferred_element_type=F32)
    gate_s[...] = gate
    m_s[...] = m_blk
    l_s[...] = l_blk

    @pl.when(step == n_steps - 1)
    def _():
        t_row = lax.broadcasted_iota(jnp.int32, (rpk, LANES), 0) & (n_new - 1)
        c_col = lax.broadcasted_iota(jnp.int32, (rpk, LANES), 1)
        own_ok = c_col <= t_row
        t_all = lax.broadcasted_iota(jnp.int32, (rows_all, LANES), 0) & (n_new - 1)
        own_blk = jnp.right_shift(past_len + t_all, MOBA_BLOCK.bit_length() - 1)
        valid = (lane < own_blk) & (lane < nb)
        sel = _topk_lanes(gate, valid, n_sel)
        for kv in range(n_kv):
            rows = slice(kv * rpk, (kv + 1) * rpk)
            q = q_ref[rows, :]
            s = lax.dot_general(q, kn_ref[kv], _NT, preferred_element_type=F32)
            s = jnp.where(own_ok, s, NEG_INF)
            m_o = jnp.max(s, axis=1, keepdims=True)
            m_sel = jnp.where(sel[rows, :], m_blk[rows, :], NEG_INF)
            m_all = jnp.maximum(m_o, jnp.max(m_sel, axis=1, keepdims=True))
            pr = jnp.exp2(s - m_all)
            w = jnp.where(sel[rows, :], jnp.exp2(m_sel - m_all), 0.0)
            den = jnp.sum(pr, axis=1, keepdims=True) + jnp.sum(w * l_blk[rows, :], axis=1, keepdims=True)
            num = jnp.dot(pr, vn_ref[kv], preferred_element_type=F32)
            for jb in range(nb):
                num = num + w[:, jb:jb + 1] * acc_s[jb, rows, :]
            o_ref[rows, :] = num / den


def _moba_sample(pt_flat, cache_k, cache_v, layer, k_new_pad, v_new_pad, q_rows, *, dec_batch, n_pages, n_new,
                 n_kv, groups, past_len):
    assert past_len % MOBA_BLOCK == 0 and n_new < MOBA_BLOCK and MOBA_BLOCK == 2 * LANES
    assert n_new & (n_new - 1) == 0
    nb = (past_len + n_new) // MOBA_BLOCK
    assert 1 <= nb <= LANES and n_pages == 2 * nb
    bps = min(PAGES_PER_STEP // 2, nb)
    assert nb % bps == 0
    n_steps = nb // bps
    rows = n_kv * groups * n_new
    prow = cache_k.shape[2]
    kern = functools.partial(_moba_sample_kernel, bps=bps, n_steps=n_steps, nb=nb, n_sel=min(MOBA_TOPK, nb),
                             n_new=n_new, n_kv=n_kv, groups=groups, past_len=past_len)
    page_specs = _page_specs((None, None, prow, HEAD_DIM), layer, n_pages, 2 * bps)
    grid_spec = pltpu.PrefetchScalarGridSpec(
        num_scalar_prefetch=1,
        grid=(dec_batch, n_steps),
        in_specs=page_specs + page_specs
        + [pl.BlockSpec((None, n_kv, LANES, HEAD_DIM), lambda b, s, pt: (b, 0, 0, 0)),
           pl.BlockSpec((None, n_kv, LANES, HEAD_DIM), lambda b, s, pt: (b, 0, 0, 0)),
           pl.BlockSpec((None, rows, HEAD_DIM), lambda b, s, pt: (b, 0, 0))],
        out_specs=pl.BlockSpec((None, rows, HEAD_DIM), lambda b, s, pt: (b, 0, 0)),
        scratch_shapes=[pltpu.VMEM((rows, LANES), F32), pltpu.VMEM((rows, LANES), F32),
                        pltpu.VMEM((rows, LANES), F32), pltpu.VMEM((nb, rows, HEAD_DIM), F32)],
    )
    return pl.pallas_call(
        kern,
        out_shape=jax.ShapeDtypeStruct((dec_batch, rows, HEAD_DIM), F32),
        grid_spec=grid_spec,
        compiler_params=_params(("parallel", "arbitrary")),
        name="moba_sample",
    )(pt_flat, *([cache_k] * (2 * bps)), *([cache_v] * (2 * bps)), k_new_pad, v_new_pad, q_rows)


def _rope_tables(pos, rot, width, n_rope_lanes=LANES):
    half = rot // 2
    inv = jnp.power(ROPE_THETA, -jnp.arange(half, dtype=F32) / half)
    ang = pos.astype(F32)[:, None] * inv[None, :]
    cos, sin = jnp.cos(ang), jnp.sin(ang)
    t = pos.shape[0]
    zeros = jnp.zeros((t, half), F32)
    rest0 = jnp.zeros((t, width - rot), F32)
    c = jnp.concatenate([cos, cos, jnp.ones((t, width - rot), F32)], axis=1)
    sa = jnp.concatenate([-sin, zeros, rest0], axis=1)
    sb = jnp.concatenate([zeros, sin, rest0], axis=1)
    reps = n_rope_lanes // width
    pad = LANES - n_rope_lanes
    c = jnp.concatenate([jnp.tile(c, (1, reps)), jnp.ones((t, pad), F32)], axis=1)
    sa = jnp.concatenate([jnp.tile(sa, (1, reps)), jnp.zeros((t, pad), F32)], axis=1)
    sb = jnp.concatenate([jnp.tile(sb, (1, reps)), jnp.zeros((t, pad), F32)], axis=1)
    return c, sa, sb


def _kv_major_rows(a, db, n_new, n_kv, groups):
    a = a.reshape(db, n_new, n_kv, groups, HEAD_DIM)
    return jnp.transpose(a, (0, 2, 3, 1, 4)).reshape(db, n_kv * groups * n_new, HEAD_DIM)


def _from_kv_major_rows(o, db, n_new, n_kv, groups):
    o = o.reshape(db, n_kv, groups, n_new, HEAD_DIM)
    return jnp.transpose(o, (0, 3, 1, 2, 4)).reshape(db * n_new, n_kv * groups * HEAD_DIM)


def _new_kv_pad(k, db, n_new, n_kv):
    k = jnp.transpose(k.reshape(db, n_new, n_kv, HEAD_DIM), (0, 2, 1, 3))
    return jnp.pad(k, ((0, 0), (0, 0), (0, LANES - n_new), (0, 0)))


def kernel(x_prompt, x_sample, cache_a_k, cache_a_v, cache_a_idx, cache_b_k, cache_b_v, page_table, w_in, norm1_g,
           c_ln_g, c_ln_b, w_spatial, b_spatial, w_branch, w_out, norm2_g, w_ff1, w_ff2, norm_f_g):
    batch, seq, d_model = x_prompt.shape
    db, n_new, _ = x_sample.shape
    depth, n_phys, page, n_kv, _ = cache_a_k.shape
    n_pages = page_table.shape[1]
    past_len = n_pages * page
    n_heads = d_model // 256
    groups = n_heads // n_kv
    aw = n_heads * HEAD_DIM
    kvw = n_kv * HEAD_DIM
    n_idx_heads = d_model // 128
    iw = n_idx_heads * IDX_DIM
    n_cg = w_spatial.shape[1]
    cw = n_cg * C_GROUP_DIM
    rot, idx_rot = HEAD_DIM // 4, IDX_DIM // 4
    assert page == LANES and aw == 2 * kvw and cw % 512 == 0 and seq % CHUNK == 0 and n_new <= CHUNK

    sizes = (aw, kvw, kvw, iw, IDX_DIM, n_idx_heads, aw, kvw, kvw, cw, cw, N_BRANCH * d_model)
    offs = [0]
    for s in sizes:
        offs.append(offs[-1] + s)
    (o_qa, o_ka, o_va, o_qi, o_ki, o_wi, o_qb, o_kb, o_vb, o_cu, o_cv, o_gl, _) = offs

    pos_p = jnp.arange(seq, dtype=jnp.int32)
    pos_s = jnp.tile(past_len + jnp.arange(n_new, dtype=jnp.int32), db)
    tabs = {
        "p": (_rope_tables(pos_p, rot, HEAD_DIM), _rope_tables(pos_p, idx_rot, IDX_DIM),
              _rope_tables(pos_p, idx_rot, IDX_DIM, IDX_DIM)),
        "s": (_rope_tables(pos_s, rot, HEAD_DIM), _rope_tables(pos_s, idx_rot, IDX_DIM),
              _rope_tables(pos_s, idx_rot, IDX_DIM, IDX_DIM)),
    }

    cache_ak = cache_a_k.reshape(depth, n_phys, page * n_kv, HEAD_DIM)
    cache_av = cache_a_v.reshape(depth, n_phys, page * n_kv, HEAD_DIM)
    cache_bk = cache_b_k.reshape(depth, n_phys, page * n_kv, HEAD_DIM)
    cache_bv = cache_b_v.reshape(depth, n_phys, page * n_kv, HEAD_DIM)
    pt_flat = page_table.reshape(-1).astype(jnp.int32)

    xp = x_prompt.reshape(batch * seq, d_model)
    xs = x_sample.reshape(db * n_new, d_model)
    rs = db * n_new
    tri = jnp.tril(jnp.ones((CHUNK, CHUNK), dtype=bool))
    sel_s = min(IDX_TOPK, (past_len + n_new) // 4)

    outs = {k: [] for k in ("pa_k", "pa_v", "pa_i", "pb_k", "pb_v", "sa_k", "sa_v", "sa_i", "sb_k", "sb_v", "sc_v")}

    w_in_t = jnp.swapaxes(w_in, 1, 2)
    cache_idx_t = jnp.swapaxes(cache_a_idx, 2, 3)

    kv_bufs = {(which, name): None for which in "ps" for name in ("ka", "kb", "va", "vb")}

    def project(x, l, which):
        t128, t64, tkiwi = tabs[which]
        xn = _rms_norm(x, norm1_g[l], BF16)

        def proj(n_off, n, out_dtypes, epi="none", extra=(), half=0, out_scale=1.0, **kw):
            return _matmul_ws(xn, w_in_t, l, out_dtypes, epi, extra, half, n_off=n_off, n=n, w_rows_are_n=True,
                              out_scale=out_scale, **kw)

        def kv_proj(name, n_off, epi="none", extra=(), half=0, want_means=False):
            res = proj(n_off, kvw, (BF16,), epi, extra, half, kv_out=(kv_bufs[which, name], depth, n_kv),
                       want_means=want_means)
            kv_bufs[which, name] = res[1]
            return (res[0], res[2]) if want_means else res[0]

        (hqa,) = proj(o_qa, aw, (BF16,), "rope", t128, rot // 2, Q_SCALE)
        (hqb,) = proj(o_qb, aw, (BF16,), "rope", t128, rot // 2, Q_SCALE)
        hka_bf = kv_proj("ka", o_ka, "rope", t128, rot // 2)
        hkb_bf = kv_proj("kb", o_kb, "rope", t128, rot // 2, want_means=(which == "p"))
        hva_bf = kv_proj("va", o_va)
        hvb_bf = kv_proj("vb", o_vb)
        (hqi,) = proj(o_qi, iw, (BF16,), "rope", t64, idx_rot // 2)
        (hkiwi,) = proj(o_ki, LANES, (F32,), "rope", tkiwi, idx_rot // 2)
        (hc,) = proj(o_cu, 2 * cw + N_BRANCH * d_model, (F32,))
        return hqa, hqb, hka_bf, hkb_bf, hva_bf, hvb_bf, hqi, hkiwi, hc

    def new_rows(name, l):
        return kv_bufs["s", name].reshape(depth, rs, kvw)[l]

    def merge_ffn(x, o_a, o_b, o_c, hc, l):
        merged = _merge(o_a, o_b, o_c, w_branch, l, hc, d_model=d_model, gate_col0=2 * cw)
        (x1,) = _matmul_ws(merged, w_out, l, (F32,), "residual", (x,))
        hn = _rms_norm(x1, norm2_g[l], BF16)
        (h1,) = _matmul_ws(hn, w_ff1, l, (BF16,), "relu2", tm=2048)
        (x2,) = _matmul_kt(h1, w_ff2, l, (F32,), "residual", (x1,))
        return x2

    for l in range(depth):
        hqa, hqb, hka_bf, (hkb_bf, kb_means), hva_bf, hvb_bf, hqi, hkiwi, hc = project(xp, l, "p")
        o_a = _dsa_prompt(hqa, hka_bf, hva_bf, hqi, hkiwi, batch=batch, seq=seq, n_heads=n_heads,
                          n_idx_heads=n_idx_heads, groups=groups)
        o_b = _moba_prompt(hqb, hkb_bf, kb_means, hvb_bf, batch=batch, seq=seq, n_heads=n_heads, groups=groups)
        ws_p = jnp.where(tri[None], w_spatial[l], 0).astype(BF16)
        bs_p = b_spatial[l].T
        (o_c,) = _gmlp(hc, c_ln_g[l], c_ln_b[l], ws_p, bs_p, want_vn=False)
        xp = merge_ffn(xp, o_a, o_b, o_c, hc, l)
        outs["pa_i"].append(hkiwi[:, :IDX_DIM].reshape(batch, seq, IDX_DIM))

        hqa, hqb, hka_bf, hkb_bf, hva_bf, hvb_bf, hqi, hkiwi, hc = project(xs, l, "s")
        qi_rows = jnp.transpose(hqi.astype(F32).reshape(db, n_new, n_idx_heads, IDX_DIM), (0, 2, 1, 3)).reshape(
            db, n_idx_heads * n_new, IDX_DIM)
        wi_rows = jnp.transpose(hkiwi[:, IDX_DIM:IDX_DIM + n_idx_heads].reshape(db, n_new, n_idx_heads),
                                (0, 2, 1)).reshape(db, n_idx_heads * n_new, 1)
        ki_new = jnp.pad(hkiwi[:, :IDX_DIM].reshape(db, n_new, IDX_DIM), ((0, 0), (0, LANES - n_new), (0, 0)))
        scores = _dsa_sample_scores(pt_flat, cache_idx_t, l, qi_rows, wi_rows, dec_batch=db, n_pages=n_pages,
                                    n_new=n_new, n_idx_heads=n_idx_heads)
        sel, sel_new = _dsa_sample_select(scores, qi_rows, wi_rows, ki_new, sel_s, n_idx_heads=n_idx_heads)
        qa_rows = _kv_major_rows(hqa.astype(F32), db, n_new, n_kv, groups)
        o_a = _dsa_sample_attn(pt_flat, cache_ak, cache_av, l, _new_kv_pad(new_rows("ka", l), db, n_new, n_kv),
                               _new_kv_pad(new_rows("va", l), db, n_new, n_kv), qa_rows, sel, sel_new, dec_batch=db,
                               n_pages=n_pages, n_new=n_new, n_kv=n_kv, groups=groups)
        o_a = _from_kv_major_rows(o_a, db, n_new, n_kv, groups).astype(BF16)
        qb_rows = _kv_major_rows(hqb.astype(F32), db, n_new, n_kv, groups)
        o_b = _moba_sample(pt_flat, cache_bk, cache_bv, l, _new_kv_pad(new_rows("kb", l), db, n_new, n_kv),
                           _new_kv_pad(new_rows("vb", l), db, n_new, n_kv), qb_rows, dec_batch=db, n_pages=n_pages,
                           n_new=n_new, n_kv=n_kv, groups=groups, past_len=past_len)
        o_b = _from_kv_major_rows(o_b, db, n_new, n_kv, groups).astype(BF16)
        ws_c = jnp.where(tri[None, :n_new, :n_new], w_spatial[l][:, :n_new, :n_new], 0)
        ws_s = jnp.einsum("ab,gij->gaibj", jnp.eye(db, dtype=F32), ws_c).reshape(n_cg, rs, rs).astype(BF16)
        bs_s = jnp.tile(b_spatial[l][:, :n_new].T, (db, 1))
        o_c, vn = _gmlp(hc, c_ln_g[l], c_ln_b[l], ws_s, bs_s, want_vn=True)
        xs = merge_ffn(xs, o_a, o_b, o_c, hc, l)
        outs["sa_i"].append(hkiwi[:, :IDX_DIM].reshape(db, n_new, IDX_DIM))
        outs["sc_v"].append(vn.reshape(db, n_new, cw))

    y_prompt = _rms_norm(xp, norm_f_g, F32).reshape(batch, seq, d_model)
    y_sample = _rms_norm(xs, norm_f_g, F32).reshape(db, n_new, d_model)

    def kv_result(which, name):
        lead = (depth, batch, seq) if which == "p" else (depth, db, n_new)
        return kv_bufs[which, name].reshape(lead + (n_kv, HEAD_DIM))

    return (y_prompt, y_sample, kv_result("p", "ka"), kv_result("p", "va"), jnp.stack(outs["pa_i"]),
            kv_result("p", "kb"), kv_result("p", "vb"), kv_result("s", "ka"), kv_result("s", "va"),
            jnp.stack(outs["sa_i"]), kv_result("s", "kb"), kv_result("s", "vb"), jnp.stack(outs["sc_v"]))
```
